```python
import jax, jax.numpy as jnp
from jax import lax
import numpy as np

D_MODEL = 2048
BATCH = 4
SEQ = 8192
DEPTH = 1

SGU_CHUNK = 128
SGU_GROUPS = 8
SGU_WIDTH = D_MODEL // 2
SGU_GROUP_DIM = SGU_WIDTH // SGU_GROUPS
GLA_HEADS = 4
GLA_KEY_DIM = D_MODEL // 2
GLA_VAL_DIM = D_MODEL
GLA_HEAD_K = GLA_KEY_DIM // GLA_HEADS
GLA_HEAD_V = GLA_VAL_DIM // GLA_HEADS
GLA_GATE_RANK = 16
GLA_GATE_NORM = 16.0
GLA_CHUNK = 64
N_GROUPS = 8
EXPERTS_PER_GROUP = 8
N_EXPERTS = N_GROUPS * EXPERTS_PER_GROUP
TOP_K = 2
D_EXPERT = D_MODEL // 4
MOE_BLOCK = 128
LN_EPS = 1e-5
DEEPNORM_ALPHA = (2.0 * DEPTH) ** 0.25
DEEPNORM_BETA = (8.0 * DEPTH) ** -0.25
IN_SIZES = (SGU_WIDTH, SGU_WIDTH, GLA_KEY_DIM, GLA_KEY_DIM, GLA_VAL_DIM, GLA_VAL_DIM, GLA_GATE_RANK)
IN_WIDTH = sum(IN_SIZES)

kernel_name = "hybrid_sgu_gla_hmoe_deepnorm"


def layer_norm(x, g, b):
    xf = x.astype(jnp.float32)
    mu = jnp.mean(xf, axis=-1, keepdims=True)
    var = jnp.mean(jnp.square(xf - mu), axis=-1, keepdims=True)
    return ((xf - mu) * lax.rsqrt(var + LN_EPS)).astype(x.dtype) * g + b


def rms_norm(x, g):
    xf = x.astype(jnp.float32)
    return (xf * lax.rsqrt(jnp.mean(jnp.square(xf), axis=-1, keepdims=True) + LN_EPS)).astype(x.dtype) * g


def split_offsets():
    return [sum(IN_SIZES[:i + 1]) for i in range(len(IN_SIZES) - 1)]


def spatial_gating_mixer(u, v, ln_g, ln_b, w_s, b_s):
    B, S, _ = u.shape
    n = S // SGU_CHUNK
    vg = v.reshape(B, S, SGU_GROUPS, SGU_GROUP_DIM)
    vg = layer_norm(vg, ln_g.reshape(SGU_GROUPS, SGU_GROUP_DIM), ln_b.reshape(SGU_GROUPS, SGU_GROUP_DIM))
    vg = vg.reshape(B, n, SGU_CHUNK, SGU_GROUPS, SGU_GROUP_DIM)
    causal = jnp.tril(jnp.ones((SGU_CHUNK, SGU_CHUNK), dtype=bool))
    w = jnp.where(causal[None], w_s, jnp.zeros_like(w_s))
    mixed = jnp.einsum('gij,bnjgd->bnigd', w, vg) + b_s.T[None, None, :, :, None]
    return u * mixed.reshape(B, S, SGU_WIDTH)


def gla_mixer(q, k, v, log_a, g_out, norm_g):
    B, S, _ = q.shape
    n = S // GLA_CHUNK
    C = GLA_CHUNK

    def heads(t, d):
        return t.reshape(B, n, C, GLA_HEADS, d).transpose(0, 3, 1, 2, 4).astype(jnp.float32)

    qh = heads(q, GLA_HEAD_K) * (GLA_HEAD_K ** -0.5)
    kh = heads(k, GLA_HEAD_K)
    vh = heads(v, GLA_HEAD_V)
    b = jnp.cumsum(heads(log_a, GLA_HEAD_K), axis=3)
    b_last = b[:, :, :, -1, :]
    q_dec = qh * jnp.exp(b)
    k_inv = kh * jnp.exp(-b)
    k_to_end = kh * jnp.exp(b_last[:, :, :, None, :] - b)
    causal = jnp.tril(jnp.ones((C, C), dtype=bool))
    attn = jnp.where(causal, jnp.einsum('bhnik,bhnjk->bhnij', q_dec, k_inv), 0.0)
    o_intra = jnp.einsum('bhnij,bhnjv->bhniv', attn, vh)

    def step(state, xs):
        qd, kt, vc, dl = xs
        o = jnp.einsum('bhik,bhkv->bhiv', qd, state)
        state = state * jnp.exp(dl)[..., None] + jnp.einsum('bhjk,bhjv->bhkv', kt, vc)
        return state, o

    xs = (jnp.moveaxis(q_dec, 2, 0), jnp.moveaxis(k_to_end, 2, 0),
          jnp.moveaxis(vh, 2, 0), jnp.moveaxis(b_last, 2, 0))
    state0 = jnp.zeros((B, GLA_HEADS, GLA_HEAD_K, GLA_HEAD_V), jnp.float32)
    _, o_inter = lax.scan(step, state0, xs)
    o = o_intra + jnp.moveaxis(o_inter, 0, 2)
    o = o.transpose(0, 2, 3, 1, 4).reshape(B, S, GLA_HEADS, GLA_HEAD_V)
    o = rms_norm(o, norm_g.reshape(GLA_HEADS, GLA_HEAD_V).astype(jnp.float32))
    return o.reshape(B, S, GLA_VAL_DIM).astype(q.dtype) * jax.nn.silu(g_out)


def mixer_sublayer(x, w_in, w_gate_a2, b_gate_a, sgu_ln_g, sgu_ln_b, sgu_w_s, sgu_b_s,
                   gla_norm_g, w_branch_a, w_branch_b, w_merge, b_merge, w_out):
    h = x @ w_in
    u, v, q, k, vg, g_out, a_lr = jnp.split(h, split_offsets(), axis=-1)
    y_a = spatial_gating_mixer(jax.nn.gelu(u, approximate=False), jax.nn.gelu(v, approximate=False),
                               sgu_ln_g, sgu_ln_b, sgu_w_s, sgu_b_s) @ w_branch_a
    log_a = jax.nn.log_sigmoid((a_lr @ w_gate_a2 + b_gate_a).astype(jnp.float32)) / GLA_GATE_NORM
    y_b = gla_mixer(q, k, vg, log_a, g_out, gla_norm_g) @ w_branch_b
    gates = jax.nn.sigmoid(x @ w_merge + b_merge)
    g_a, g_b = jnp.split(gates, 2, axis=-1)
    return (g_a * y_a + g_b * y_b) @ w_out


def hierarchical_moe(x, w_rg, b_rg, w_re, b_re, w1, w3, w2):
    B, S, D = x.shape
    T = B * S
    xt = x.reshape(T, D)
    g_logits = (xt @ w_rg + b_rg).astype(jnp.float32)
    g_prob = jax.nn.softmax(g_logits, axis=-1)
    g_idx = jnp.argmax(g_logits, axis=-1)
    p_group = jnp.take_along_axis(g_prob, g_idx[:, None], axis=-1)
    e_logits = (xt @ w_re + b_re).astype(jnp.float32).reshape(T, N_GROUPS, EXPERTS_PER_GROUP)
    e_logits = jnp.take_along_axis(e_logits, g_idx[:, None, None], axis=1)[:, 0]
    top_v, top_i = lax.top_k(e_logits, TOP_K)
    weights = p_group * jax.nn.softmax(top_v, axis=-1)
    expert_ids = (g_idx[:, None] * EXPERTS_PER_GROUP + top_i).astype(jnp.int32)

    A = T * TOP_K
    flat_e = expert_ids.reshape(A)
    flat_tok = (jnp.arange(A, dtype=jnp.int32) // TOP_K)
    order = jnp.argsort(flat_e)
    e_sorted = flat_e[order]
    counts = jnp.bincount(flat_e, length=N_EXPERTS)
    starts = jnp.cumsum(counts) - counts
    padded = (counts + MOE_BLOCK - 1) // MOE_BLOCK * MOE_BLOCK
    pad_ends = jnp.cumsum(padded)
    pad_starts = pad_ends - padded
    dest_sorted = pad_starts[e_sorted] + (jnp.arange(A) - starts[e_sorted])
    dest = jnp.zeros((A,), jnp.int32).at[order].set(dest_sorted.astype(jnp.int32))
    P = A + N_EXPERTS * MOE_BLOCK
    n_blocks = P // MOE_BLOCK
    buf_tok = jnp.full((P,), T, jnp.int32).at[dest].set(flat_tok)
    block_expert = jnp.minimum(
        jnp.searchsorted(pad_ends, jnp.arange(n_blocks) * MOE_BLOCK, side='right'), N_EXPERTS - 1)
    x_pad = jnp.concatenate([xt, jnp.zeros((1, D), xt.dtype)], axis=0)

    def expert_block(args):
        tok, e = args
        xb = x_pad[tok]
        return (jax.nn.silu(xb @ w1[e]) * (xb @ w3[e])) @ w2[e]

    y_buf = lax.map(expert_block, (buf_tok.reshape(n_blocks, MOE_BLOCK), block_expert)).reshape(P, D)
    y = jnp.sum(y_buf[dest].reshape(T, TOP_K, D) * weights[..., None].astype(x.dtype), axis=1)
    return y.reshape(B, S, D)


def setup_inputs(seed: int = 0) -> dict:
    key = jax.random.key(seed)
    ks = jax.random.split(key, 32)
    L, D = DEPTH, D_MODEL
    beta = DEEPNORM_BETA

    def nrm(k, shape, scale):
        return jax.random.normal(k, shape, jnp.float32) * scale

    x = nrm(ks[0], (BATCH, SEQ, D), 1.0)
    w_in = jnp.concatenate([
        nrm(ks[1], (L, D, 2 * SGU_WIDTH + 2 * GLA_KEY_DIM), D ** -0.5),
        nrm(ks[2], (L, D, GLA_VAL_DIM), D ** -0.5 * beta),
        nrm(ks[3], (L, D, GLA_VAL_DIM + GLA_GATE_RANK), D ** -0.5),
    ], axis=-1)
    return {
        "x": x,
        "w_in": w_in,
        "w_gate_a2": nrm(ks[4], (L, GLA_GATE_RANK, GLA_KEY_DIM), GLA_GATE_RANK ** -0.5),
        "b_gate_a": nrm(ks[5], (L, GLA_KEY_DIM), 0.1),
        "sgu_ln_g": 1.0 + nrm(ks[6], (L, SGU_WIDTH), 0.02),
        "sgu_ln_b": nrm(ks[7], (L, SGU_WIDTH), 0.02),
        "sgu_w_s": nrm(ks[8], (L, SGU_GROUPS, SGU_CHUNK, SGU_CHUNK), SGU_CHUNK ** -0.5),
        "sgu_b_s": 1.0 + nrm(ks[9], (L, SGU_GROUPS, SGU_CHUNK), 0.02),
        "gla_norm_g": 1.0 + nrm(ks[10], (L, GLA_VAL_DIM), 0.02),
        "w_branch_a": nrm(ks[11], (L, SGU_WIDTH, D), SGU_WIDTH ** -0.5 * beta),
        "w_branch_b": nrm(ks[12], (L, GLA_VAL_DIM, D), GLA_VAL_DIM ** -0.5 * beta),
        "w_merge": nrm(ks[13], (L, D, 2 * D), D ** -0.5),
        "b_merge": nrm(ks[14], (L, 2 * D), 0.02),
        "w_out": nrm(ks[15], (L, D, D), D ** -0.5 * beta),
        "ln1_g": 1.0 + nrm(ks[16], (L, D), 0.02),
        "ln1_b": nrm(ks[17], (L, D), 0.02),
        "w_router_group": nrm(ks[18], (L, D, N_GROUPS), D ** -0.5),
        "b_router_group": nrm(ks[19], (L, N_GROUPS), 0.01),
        "w_router_expert": nrm(ks[20], (L, D, N_EXPERTS), D ** -0.5),
        "b_router_expert": nrm(ks[21], (L, N_EXPERTS), 0.01),
        "w_exp_gate": nrm(ks[22], (L, N_EXPERTS, D, D_EXPERT), D ** -0.5),
        "w_exp_up": nrm(ks[23], (L, N_EXPERTS, D, D_EXPERT), D ** -0.5),
        "w_exp_down": nrm(ks[24], (L, N_EXPERTS, D_EXPERT, D), D_EXPERT ** -0.5 * beta),
        "ln2_g": 1.0 + nrm(ks[25], (L, D), 0.02),
        "ln2_b": nrm(ks[26], (L, D), 0.02),
    }


def reference(x, w_in, w_gate_a2, b_gate_a, sgu_ln_g, sgu_ln_b, sgu_w_s, sgu_b_s, gla_norm_g,
              w_branch_a, w_branch_b, w_merge, b_merge, w_out, ln1_g, ln1_b,
              w_router_group, b_router_group, w_router_expert, b_router_expert,
              w_exp_gate, w_exp_up, w_exp_down, ln2_g, ln2_b):
    h = x
    for l in range(DEPTH):
        mix = mixer_sublayer(h, w_in[l], w_gate_a2[l], b_gate_a[l], sgu_ln_g[l], sgu_ln_b[l],
                             sgu_w_s[l], sgu_b_s[l], gla_norm_g[l], w_branch_a[l], w_branch_b[l],
                             w_merge[l], b_merge[l], w_out[l])
        h = layer_norm(DEEPNORM_ALPHA * h + mix, ln1_g[l], ln1_b[l])
        ffn = hierarchical_moe(h, w_router_group[l], b_router_group[l], w_router_expert[l],
                               b_router_expert[l], w_exp_gate[l], w_exp_up[l], w_exp_down[l])
        h = layer_norm(DEEPNORM_ALPHA * h + ffn, ln2_g[l], ln2_b[l])
    return h
```

```python
import functools

import jax
import jax.numpy as jnp
from jax import lax
from jax.experimental import pallas as pl
from jax.experimental.pallas import tpu as pltpu

F32 = jnp.float32
BF16 = jnp.bfloat16

SGU_CHUNK = 128
SGU_GROUPS = 8
GLA_HEADS = 4
GLA_CHUNK = 64
GLA_GATE_NORM = 16.0
N_GROUPS = 8
EXPERTS_PER_GROUP = 8
N_EXPERTS = N_GROUPS * EXPERTS_PER_GROUP
TOP_K = 2
LN_EPS = 1e-5
DEEPNORM_ALPHA = 2.0 ** 0.25

LANES = 128
MOE_ROWS = 256
V7X_VMEM_BYTES = 64 * 2 ** 20


def _vmem_limit(nbytes):
    return int(min(max(2 * nbytes, 16 * 2 ** 20), V7X_VMEM_BYTES - 8 * 2 ** 20))


def _layer_norm(y, g, b):
    mu = jnp.mean(y, axis=-1, keepdims=True)
    var = jnp.mean(jnp.square(y - mu), axis=-1, keepdims=True)
    return (y - mu) * lax.rsqrt(var + LN_EPS) * g + b


def _gelu(x):
    return 0.5 * x * (1.0 + lax.erf(x * (2.0 ** -0.5)))


def _identity(x):
    return x


def _proj_kernel(x_ref, w_ref, b_ref, o_ref, *, act):
    acc = jnp.dot(x_ref[...], w_ref[...], preferred_element_type=F32)
    o_ref[...] = act(acc + b_ref[...]).astype(o_ref.dtype)


def _proj(x, w, b, act, out_dtype, tm, tn):
    m, k = x.shape
    n = w.shape[1]
    tm, tn = min(tm, m), min(tn, n)
    nbytes = tm * k * 2 + k * tn * 2 + tm * tn * (4 + jnp.dtype(out_dtype).itemsize)
    return pl.pallas_call(
        functools.partial(_proj_kernel, act=act),
        grid=(m // tm, n // tn),
        in_specs=[
            pl.BlockSpec((tm, k), lambda i, j: (i, 0)),
            pl.BlockSpec((k, tn), lambda i, j: (0, j)),
            pl.BlockSpec((1, tn), lambda i, j: (0, j)),
        ],
        out_specs=pl.BlockSpec((tm, tn), lambda i, j: (i, j)),
        out_shape=jax.ShapeDtypeStruct((m, n), out_dtype),
        compiler_params=pltpu.CompilerParams(
            dimension_semantics=("parallel", "arbitrary"), vmem_limit_bytes=_vmem_limit(nbytes)),
        name="proj",
    )(x, w, b)


def _sgu_kernel(u_ref, v_ref, ws_ref, g_ref, b_ref, bs_ref, o_ref):
    c = SGU_CHUNK
    row = lax.broadcasted_iota(jnp.int32, (c, c), 0)
    col = lax.broadcasted_iota(jnp.int32, (c, c), 1)
    w = jnp.where(row >= col, ws_ref[0], 0.0).astype(BF16)
    ln_g, ln_b, bias = g_ref[0], b_ref[0], bs_ref[0]
    for ci in range(u_ref.shape[0] // c):
        rows = slice(ci * c, (ci + 1) * c)
        vn = _layer_norm(v_ref[rows, :].astype(F32), ln_g, ln_b)
        mixed = jnp.dot(w, vn.astype(BF16), preferred_element_type=F32) + bias
        o_ref[rows, :] = (u_ref[rows, :].astype(F32) * mixed).astype(o_ref.dtype)


def _sgu(uv, w_s, ln_g, ln_b, b_s, tm):
    t = uv.shape[0]
    width = uv.shape[1] // 2
    c = SGU_CHUNK
    ng = width // c
    tm = min(tm, t)
    return pl.pallas_call(
        _sgu_kernel,
        grid=(t // tm, ng),
        in_specs=[
            pl.BlockSpec((tm, c), lambda i, g: (i, g)),
            pl.BlockSpec((tm, c), lambda i, g: (i, g + width // c)),
            pl.BlockSpec((1, c, c), lambda i, g: (g, 0, 0)),
            pl.BlockSpec((1, 1, c), lambda i, g: (g, 0, 0)),
            pl.BlockSpec((1, 1, c), lambda i, g: (g, 0, 0)),
            pl.BlockSpec((1, c, 1), lambda i, g: (g, 0, 0)),
        ],
        out_specs=pl.BlockSpec((tm, c), lambda i, g: (i, g)),
        out_shape=jax.ShapeDtypeStruct((t, width), BF16),
        compiler_params=pltpu.CompilerParams(dimension_semantics=("parallel", "arbitrary")),
        name="sgu",
    )(uv, uv, w_s, ln_g.reshape(ng, 1, c), ln_b.reshape(ng, 1, c), b_s.reshape(ng, c, 1))


def _gla_kernel(q_ref, k_ref, v_ref, g_ref, a_ref, wg_ref, bg_ref, ng_ref, o_ref, st_ref):
    c = GLA_CHUNK
    dk = q_ref.shape[1]

    @pl.when(pl.program_id(2) == 0)
    def _():
        st_ref[...] = jnp.zeros_like(st_ref)

    row = lax.broadcasted_iota(jnp.int32, (c, c), 0)
    col = lax.broadcasted_iota(jnp.int32, (c, c), 1)
    causal = row >= col
    ones_tril = causal.astype(BF16)
    scale = dk ** -0.5
    for ci in range(q_ref.shape[0] // c):
        rows = slice(ci * c, (ci + 1) * c)
        z = jnp.dot(a_ref[rows, :], wg_ref[...], preferred_element_type=F32) + bg_ref[...]
        log_a = (jnp.minimum(z, 0.0) - jnp.log1p(jnp.exp(-jnp.abs(z)))) * (1.0 / GLA_GATE_NORM)
        la_hi = log_a.astype(BF16)
        la_lo = (log_a - la_hi.astype(F32)).astype(BF16)
        b = (jnp.dot(ones_tril, la_hi, preferred_element_type=F32)
             + jnp.dot(ones_tril, la_lo, preferred_element_type=F32))
        b_last = b[c - 1:c, :]
        q = q_ref[rows, :].astype(F32) * scale
        k = k_ref[rows, :].astype(F32)
        q_dec = (q * jnp.exp(b)).astype(BF16)
        k_inv = (k * jnp.exp(-b)).astype(BF16)
        k_to_end = (k * jnp.exp(b_last - b)).astype(BF16)
        v = v_ref[rows, :]
        attn = lax.dot_general(q_dec, k_inv, (((1,), (1,)), ((), ())), preferred_element_type=F32)
        attn = jnp.where(causal, attn, 0.0).astype(BF16)
        state_t = st_ref[...]
        o = jnp.dot(attn, v, preferred_element_type=F32) + lax.dot_general(
            q_dec, state_t.astype(BF16), (((1,), (1,)), ((), ())), preferred_element_type=F32)
        st_ref[...] = state_t * jnp.exp(b_last) + lax.dot_general(
            v, k_to_end, (((0,), (0,)), ((), ())), preferred_element_type=F32)
        o = o * lax.rsqrt(jnp.mean(jnp.square(o), axis=-1, keepdims=True) + LN_EPS) * ng_ref[...]
        gate = g_ref[rows, :].astype(F32)
        o_ref[rows, :] = (o * (gate * jax.nn.sigmoid(gate))).astype(o_ref.dtype)


def _gla(h2, a_lr, w_gate, b_gate, norm_g, batch, ts):
    t = h2.shape[0]
    seq = t // batch
    nh = GLA_HEADS
    key_dim = w_gate.shape[1]
    dk = key_dim // nh
    val_dim = norm_g.shape[1]
    dv = val_dim // nh
    ts = min(ts, seq)
    ns = seq // ts
    kb, vb, gb = key_dim // dk, 2 * key_dim // dv, (2 * key_dim + val_dim) // dv
    tok = lambda b, h, s: b * ns + s
    return pl.pallas_call(
        _gla_kernel,
        grid=(batch, nh, ns),
        in_specs=[
            pl.BlockSpec((ts, dk), lambda b, h, s: (tok(b, h, s), h)),
            pl.BlockSpec((ts, dk), lambda b, h, s: (tok(b, h, s), kb + h)),
            pl.BlockSpec((ts, dv), lambda b, h, s: (tok(b, h, s), vb + h)),
            pl.BlockSpec((ts, dv), lambda b, h, s: (tok(b, h, s), gb + h)),
            pl.BlockSpec((ts, LANES), lambda b, h, s: (tok(b, h, s), 0)),
            pl.BlockSpec((LANES, dk), lambda b, h, s: (0, h)),
            pl.BlockSpec((1, dk), lambda b, h, s: (0, h)),
            pl.BlockSpec((1, dv), lambda b, h, s: (0, h)),
        ],
        out_specs=pl.BlockSpec((ts, dv), lambda b, h, s: (tok(b, h, s), h)),
        out_shape=jax.ShapeDtypeStruct((t, val_dim), BF16),
        scratch_shapes=[pltpu.VMEM((dv, dk), F32)],
        compiler_params=pltpu.CompilerParams(
            dimension_semantics=("parallel", "parallel", "arbitrary")),
        name="gla",
    )(h2, h2, h2, h2, a_lr, w_gate, b_gate, norm_g)


def _merge_kernel(s_ref, o_ref, wa_ref, wb_ref, ga_ref, gb_ref, out_ref):
    ya = jnp.dot(s_ref[...], wa_ref[...], preferred_element_type=F32)
    yb = jnp.dot(o_ref[...], wb_ref[...], preferred_element_type=F32)
    out_ref[...] = (ga_ref[...].astype(F32) * ya + gb_ref[...].astype(F32) * yb).astype(out_ref.dtype)


def _merge(s, o, w_a, w_b, gates, tm, tn):
    t, ka = s.shape
    kb = o.shape[1]
    d = w_a.shape[1]
    tm, tn = min(tm, t), min(tn, d)
    nbytes = tm * (ka + kb) * 2 + (ka + kb) * tn * 2 + tm * tn * (2 * 2 + 2 + 8)
    return pl.pallas_call(
        _merge_kernel,
        grid=(t // tm, d // tn),
        in_specs=[
            pl.BlockSpec((tm, ka), lambda i, j: (i, 0)),
            pl.BlockSpec((tm, kb), lambda i, j: (i, 0)),
            pl.BlockSpec((ka, tn), lambda i, j: (0, j)),
            pl.BlockSpec((kb, tn), lambda i, j: (0, j)),
            pl.BlockSpec((tm, tn), lambda i, j: (i, j)),
            pl.BlockSpec((tm, tn), lambda i, j: (i, j + d // tn)),
        ],
        out_specs=pl.BlockSpec((tm, tn), lambda i, j: (i, j)),
        out_shape=jax.ShapeDtypeStruct((t, d), BF16),
        compiler_params=pltpu.CompilerParams(
            dimension_semantics=("parallel", "arbitrary"), vmem_limit_bytes=_vmem_limit(nbytes)),
        name="merge",
    )(s, o, w_a, w_b, gates, gates)


def _route(logits):
    lane = lax.broadcasted_iota(jnp.int32, logits.shape, 1).astype(F32)
    neg = float("-inf")
    big = float(LANES)
    gl = jnp.where(lane < N_GROUPS, logits, neg)
    gmax = jnp.max(gl, axis=1, keepdims=True)
    gidx = jnp.min(jnp.where(gl == gmax, lane, big), axis=1, keepdims=True)
    p_group = 1.0 / jnp.sum(jnp.exp(gl - gmax), axis=1, keepdims=True)
    lo = N_GROUPS + EXPERTS_PER_GROUP * gidx
    el = jnp.where((lane >= lo) & (lane < lo + EXPERTS_PER_GROUP), logits, neg)
    v1 = jnp.max(el, axis=1, keepdims=True)
    i1 = jnp.min(jnp.where(el == v1, lane, big), axis=1, keepdims=True)
    el2 = jnp.where(lane == i1, neg, el)
    v2 = jnp.max(el2, axis=1, keepdims=True)
    i2 = jnp.min(jnp.where(el2 == v2, lane, big), axis=1, keepdims=True)
    t = jnp.exp(v2 - v1)
    w1 = p_group / (1.0 + t)
    w2 = p_group * t / (1.0 + t)
    return jnp.where(lane == 0, i1 - N_GROUPS,
                     jnp.where(lane == 1, i2 - N_GROUPS,
                               jnp.where(lane == 2, w1, jnp.where(lane == 3, w2, 0.0))))


def _out_kernel(m_ref, w_ref, x_ref, g_ref, b_ref, wr_ref, br_ref, h_ref, r_ref):
    mix = jnp.dot(m_ref[...], w_ref[...], preferred_element_type=F32)
    h = _layer_norm(DEEPNORM_ALPHA * x_ref[...] + mix, g_ref[...], b_ref[...])
    h_ref[...] = h
    logits = jnp.dot(h.astype(BF16), wr_ref[...], preferred_element_type=F32) + br_ref[...]
    r_ref[...] = _route(logits)


def _out(merged, w_out, x, ln_g, ln_b, w_r, b_r, tm):
    t, d = x.shape
    tm = min(tm, t)
    nbytes = d * d * 2 + tm * d * (2 + 4 + 4 + 8) + d * LANES * 2
    return pl.pallas_call(
        _out_kernel,
        grid=(t // tm,),
        in_specs=[
            pl.BlockSpec((tm, d), lambda i: (i, 0)),
            pl.BlockSpec((d, d), lambda i: (0, 0)),
            pl.BlockSpec((tm, d), lambda i: (i, 0)),
            pl.BlockSpec((1, d), lambda i: (0, 0)),
            pl.BlockSpec((1, d), lambda i: (0, 0)),
            pl.BlockSpec((d, LANES), lambda i: (0, 0)),
            pl.BlockSpec((1, LANES), lambda i: (0, 0)),
        ],
        out_specs=[pl.BlockSpec((tm, d), lambda i: (i, 0)), pl.BlockSpec((tm, LANES), lambda i: (i, 0))],
        out_shape=[jax.ShapeDtypeStruct((t, d), F32), jax.ShapeDtypeStruct((t, LANES), F32)],
        compiler_params=pltpu.CompilerParams(
            dimension_semantics=("parallel",), vmem_limit_bytes=_vmem_limit(nbytes)),
        name="out_ln_route",
    )(merged, w_out, x, ln_g, ln_b, w_r, b_r)


def _rank_kernel(r_ref, rank_ref, cnt_ref, carry_ref):
    rows = r_ref.shape[0]

    @pl.when(pl.program_id(0) == 0)
    def _():
        carry_ref[...] = jnp.zeros_like(carry_ref)

    lane = lax.broadcasted_iota(jnp.int32, (rows, LANES), 1).astype(F32)
    route = r_ref[...]
    hit0 = lane == route[:, 0:1]
    hit1 = lane == route[:, 1:2]
    cnt = hit0.astype(F32) + hit1.astype(F32)
    row = lax.broadcasted_iota(jnp.int32, (rows, rows), 0)
    col = lax.broadcasted_iota(jnp.int32, (rows, rows), 1)
    earlier = (row > col).astype(BF16)
    before = jnp.dot(earlier, cnt.astype(BF16), preferred_element_type=F32) + carry_ref[...]
    rank0 = jnp.sum(jnp.where(hit0, before, 0.0), axis=1, keepdims=True)
    rank1 = jnp.sum(jnp.where(hit1, before, 0.0), axis=1, keepdims=True)
    rank_ref[...] = jnp.where(lane == 0, rank0, jnp.where(lane == 1, rank1, 0.0))
    carry_ref[...] += jnp.sum(cnt, axis=0, keepdims=True)
    cnt_ref[...] = carry_ref[...]


def _rank(route, tm):
    t = route.shape[0]
    tm = min(tm, t)
    return pl.pallas_call(
        _rank_kernel,
        grid=(t // tm,),
        in_specs=[pl.BlockSpec((tm, LANES), lambda i: (i, 0))],
        out_specs=[pl.BlockSpec((tm, LANES), lambda i: (i, 0)), pl.BlockSpec((1, LANES), lambda i: (0, 0))],
        out_shape=[jax.ShapeDtypeStruct((t, LANES), F32), jax.ShapeDtypeStruct((1, LANES), F32)],
        scratch_shapes=[pltpu.VMEM((1, LANES), F32)],
        compiler_params=pltpu.CompilerParams(dimension_semantics=("arbitrary",)),
        name="rank",
    )(route)


def _row_copy(src, src_row, dst, dst_row, sem):
    return pltpu.make_async_copy(src.at[pl.ds(src_row, 1), :], dst.at[pl.ds(dst_row, 1), :], sem)


def _dispatch_kernel(zrow_ref, dest_ref, h_ref, xs_ref, zero_ref, sem, zsem):
    rows = h_ref.shape[0]
    zrows = zero_ref.shape[0]

    def zero_copy(e):
        start_row = pl.multiple_of(jnp.maximum(zrow_ref[e], 0), zrows)
        return pltpu.make_async_copy(zero_ref, xs_ref.at[pl.ds(start_row, zrows), :], zsem)

    @pl.when(pl.program_id(0) == 0)
    def _():
        zero_ref[...] = jnp.zeros_like(zero_ref)

        def start(e, carry):
            @pl.when(zrow_ref[e] >= 0)
            def _():
                zero_copy(e).start()
            return carry

        def wait(e, carry):
            @pl.when(zrow_ref[e] >= 0)
            def _():
                zero_copy(e).wait()
            return carry

        lax.fori_loop(0, zrow_ref.shape[0], start, 0)
        lax.fori_loop(0, zrow_ref.shape[0], wait, 0)

    def start(r, carry):
        for k in range(TOP_K):
            _row_copy(h_ref, r, xs_ref, dest_ref[0, 0, TOP_K * r + k], sem).start()
        return carry

    def wait(r, carry):
        for k in range(TOP_K):
            _row_copy(h_ref, r, xs_ref, dest_ref[0, 0, TOP_K * r + k], sem).wait()
        return carry

    lax.fori_loop(0, rows, start, 0)
    lax.fori_loop(0, rows, wait, 0)


def _dispatch(h, dest, zrow, n_rows, tm):
    t, d = h.shape
    tm = min(tm, t)
    grid_spec = pltpu.PrefetchScalarGridSpec(
        num_scalar_prefetch=1,
        grid=(t // tm,),
        in_specs=[
            pl.BlockSpec((1, 1, TOP_K * tm), lambda i, z: (i, 0, 0), memory_space=pltpu.SMEM),
            pl.BlockSpec((tm, d), lambda i, z: (i, 0)),
        ],
        out_specs=pl.BlockSpec(memory_space=pl.ANY),
        scratch_shapes=[pltpu.VMEM((MOE_ROWS, d), F32), pltpu.SemaphoreType.DMA, pltpu.SemaphoreType.DMA],
    )
    return pl.pallas_call(
        _dispatch_kernel,
        grid_spec=grid_spec,
        out_shape=jax.ShapeDtypeStruct((n_rows, d), F32),
        compiler_params=pltpu.CompilerParams(dimension_semantics=("arbitrary",)),
        name="dispatch",
    )(zrow, dest.reshape(t // tm, 1, TOP_K * tm), h)


def _expert_kernel(be_ref, nu_ref, x_ref, w1_ref, w3_ref, w2_ref, y_ref):
    @pl.when(pl.program_id(0) < nu_ref[0])
    def _():
        x = x_ref[...].astype(BF16)
        a = jnp.dot(x, w1_ref[0], preferred_element_type=F32)
        b = jnp.dot(x, w3_ref[0], preferred_element_type=F32)
        mid = (a * jax.nn.sigmoid(a) * b).astype(BF16)
        y_ref[...] = jnp.dot(mid, w2_ref[0], preferred_element_type=F32)

    @pl.when(pl.program_id(0) >= nu_ref[0])
    def _():
        y_ref[...] = jnp.zeros_like(y_ref)


def _experts(xs, block_expert, n_used, w1, w3, w2):
    p, d = xs.shape
    de = w1.shape[2]
    nb = p // MOE_ROWS
    blk = lambda i, be, nu: (jnp.minimum(i, nu[0] - 1), 0)
    wsel = lambda i, be, nu: (be[i], 0, 0)
    nbytes = MOE_ROWS * d * (4 + 4) + 3 * d * de * 2 + MOE_ROWS * de * 12
    grid_spec = pltpu.PrefetchScalarGridSpec(
        num_scalar_prefetch=2,
        grid=(nb,),
        in_specs=[
            pl.BlockSpec((MOE_ROWS, d), blk),
            pl.BlockSpec((1, d, de), wsel),
            pl.BlockSpec((1, d, de), wsel),
            pl.BlockSpec((1, de, d), wsel),
        ],
        out_specs=pl.BlockSpec((MOE_ROWS, d), lambda i, be, nu: (i, 0)),
    )
    return pl.pallas_call(
        _expert_kernel,
        grid_spec=grid_spec,
        out_shape=jax.ShapeDtypeStruct((p, d), F32),
        compiler_params=pltpu.CompilerParams(
            dimension_semantics=("arbitrary",), vmem_limit_bytes=_vmem_limit(nbytes)),
        name="experts",
    )(block_expert, n_used, xs, w1, w3, w2)


def _combine_kernel(dest_ref, h_ref, r_ref, y_ref, g_ref, b_ref, o_ref, buf_ref, sem):
    rows = h_ref.shape[0]

    def start(r, carry):
        for k in range(TOP_K):
            _row_copy(y_ref, dest_ref[0, 0, TOP_K * r + k], buf_ref.at[k], r, sem).start()
        return carry

    def wait(r, carry):
        for k in range(TOP_K):
            _row_copy(y_ref, dest_ref[0, 0, TOP_K * r + k], buf_ref.at[k], r, sem).wait()
        return carry

    lax.fori_loop(0, rows, start, 0)
    lax.fori_loop(0, rows, wait, 0)
    route = r_ref[...]
    moe = route[:, 2:3] * buf_ref[0] + route[:, 3:4] * buf_ref[1]
    o_ref[...] = _layer_norm(DEEPNORM_ALPHA * h_ref[...] + moe, g_ref[...], b_ref[...])


def _combine(h, route, dest, y, ln_g, ln_b, tm):
    t, d = h.shape
    tm = min(tm, t)
    return pl.pallas_call(
        _combine_kernel,
        grid=(t // tm,),
        in_specs=[
            pl.BlockSpec((1, 1, TOP_K * tm), lambda i: (i, 0, 0), memory_space=pltpu.SMEM),
            pl.BlockSpec((tm, d), lambda i: (i, 0)),
            pl.BlockSpec((tm, LANES), lambda i: (i, 0)),
            pl.BlockSpec(memory_space=pl.ANY),
            pl.BlockSpec((1, d), lambda i: (0, 0)),
            pl.BlockSpec((1, d), lambda i: (0, 0)),
        ],
        out_specs=pl.BlockSpec((tm, d), lambda i: (i, 0)),
        out_shape=jax.ShapeDtypeStruct((t, d), F32),
        scratch_shapes=[pltpu.VMEM((TOP_K, tm, d), F32), pltpu.SemaphoreType.DMA],
        compiler_params=pltpu.CompilerParams(dimension_semantics=("arbitrary",)),
        name="combine",
    )(dest.reshape(t // tm, 1, TOP_K * tm), h, route, y, ln_g, ln_b)


def kernel(x, w_in, w_gate_a2, b_gate_a, sgu_ln_g, sgu_ln_b, sgu_w_s, sgu_b_s, gla_norm_g, w_branch_a, w_branch_b, w_merge, b_merge, w_out, ln1_g, ln1_b, w_router_group, b_router_group, w_router_expert, b_router_expert, w_exp_gate, w_exp_up, w_exp_down, ln2_g, ln2_b):
    batch, seq, d = x.shape
    t = batch * seq
    assert w_in.shape[0] == 1, "one layer"
    assert seq % SGU_CHUNK == 0 and seq % GLA_CHUNK == 0 and t % MOE_ROWS == 0
    sgu_width = sgu_ln_g.shape[1]
    key_dim = w_gate_a2.shape[2]
    val_dim = gla_norm_g.shape[1]
    rank = w_gate_a2.shape[1]
    xf = x.reshape(t, d)
    xb = xf.astype(BF16)
    w_in_b = w_in[0].astype(BF16)
    n_uv, n_h2 = 2 * sgu_width, 2 * key_dim + 2 * val_dim

    uv = _proj(xb, w_in_b[:, :n_uv], jnp.zeros((1, n_uv), F32), _gelu, BF16, 1024, 512)
    h2 = _proj(xb, w_in_b[:, n_uv:n_uv + n_h2], jnp.zeros((1, n_h2), F32), _identity, BF16, 1024, 512)
    w_a = jnp.pad(w_in_b[:, n_uv + n_h2:], ((0, 0), (0, LANES - rank)))
    a_lr = _proj(xb, w_a, jnp.zeros((1, LANES), F32), _identity, BF16, 1024, LANES)
    gates = _proj(xb, w_merge[0].astype(BF16), b_merge, jax.nn.sigmoid, BF16, 1024, 512)

    s = _sgu(uv, sgu_w_s[0], sgu_ln_g[0], sgu_ln_b[0], sgu_b_s[0], 1024)
    w_gate = jnp.pad(w_gate_a2[0].astype(BF16), ((0, LANES - rank), (0, 0)))
    o = _gla(h2, a_lr, w_gate, b_gate_a, gla_norm_g, batch, 512)
    merged = _merge(s, o, w_branch_a[0].astype(BF16), w_branch_b[0].astype(BF16), gates, 1024, 512)

    w_r = jnp.concatenate([w_router_group[0], w_router_expert[0]], axis=1)
    n_r = w_r.shape[1]
    w_r = jnp.pad(w_r, ((0, 0), (0, LANES - n_r))).astype(BF16)
    b_r = jnp.pad(jnp.concatenate([b_router_group, b_router_expert], axis=1), ((0, 0), (0, LANES - n_r)))
    h1, route = _out(merged, w_out[0].astype(BF16), xf, ln1_g, ln1_b, w_r, b_r, 256)

    ranks, counts = _rank(route, 256)
    expert_ids = route[:, :TOP_K].astype(jnp.int32)
    counts = counts[0, :N_EXPERTS].astype(jnp.int32)
    blocks_per_expert = (counts + MOE_ROWS - 1) // MOE_ROWS
    block_ends = jnp.cumsum(blocks_per_expert)
    row_starts = (block_ends - blocks_per_expert) * MOE_ROWS
    dest = row_starts[expert_ids] + ranks[:, :TOP_K].astype(jnp.int32)
    n_blocks = t * TOP_K // MOE_ROWS + N_EXPERTS
    n_used = block_ends[-1:]
    block_ids = jnp.minimum(jnp.arange(n_blocks, dtype=jnp.int32), n_used[0] - 1)
    block_expert = jnp.minimum(
        jnp.searchsorted(block_ends, block_ids, side="right"), N_EXPERTS - 1).astype(jnp.int32)
    tail_ids = n_used[0] + jnp.arange(N_EXPERTS, dtype=jnp.int32)
    zrow = jnp.concatenate([
        jnp.where(blocks_per_expert > 0, (block_ends - 1) * MOE_ROWS, -1),
        jnp.where(tail_ids < n_blocks, tail_ids * MOE_ROWS, -1)]).astype(jnp.int32)

    xs = _dispatch(h1, dest, zrow, n_blocks * MOE_ROWS, 256)
    y = _experts(xs, block_expert, n_used.astype(jnp.int32), w_exp_gate[0].astype(BF16),
                 w_exp_up[0].astype(BF16), w_exp_down[0].astype(BF16))
    out = _combine(h1, route, dest, y, ln2_g, ln2_b, 256)
    return out.reshape(batch, seq, d)
```

```python
import functools

import jax
import jax.numpy as jnp
from jax import lax
from jax.experimental import pallas as pl
from jax.experimental.pallas import tpu as pltpu

F32 = jnp.float32
BF16 = jnp.bfloat16

SGU_CHUNK = 128
SGU_GROUPS = 8
GLA_HEADS = 4
GLA_CHUNK = 64
GLA_GATE_NORM = 16.0
N_GROUPS = 8
EXPERTS_PER_GROUP = 8
N_EXPERTS = N_GROUPS * EXPERTS_PER_GROUP
TOP_K = 2
LN_EPS = 1e-5
DEEPNORM_ALPHA = 2.0 ** 0.25

LANES = 128
MOE_ROWS = 256
ROW_DMA_UNROLL = 8
V7X_VMEM_BYTES = 64 * 2 ** 20


def _vmem_limit(nbytes):
    return int(min(max(2 * nbytes, 16 * 2 ** 20), V7X_VMEM_BYTES - 8 * 2 ** 20))


def _layer_norm(y, g, b):
    mu = jnp.mean(y, axis=-1, keepdims=True)
    var = jnp.mean(jnp.square(y - mu), axis=-1, keepdims=True)
    return (y - mu) * lax.rsqrt(var + LN_EPS) * g + b


def _gelu(x):
    return 0.5 * x * (1.0 + lax.erf(x * (2.0 ** -0.5)))


def _identity(x):
    return x


def _proj_kernel(x_ref, w_ref, b_ref, o_ref, *, act):
    acc = jnp.dot(x_ref[...], w_ref[...], preferred_element_type=F32)
    o_ref[...] = act(acc + b_ref[...]).astype(o_ref.dtype)


def _proj(x, w, b, act, out_dtype, tm, tn):
    m, k = x.shape
    n = w.shape[1]
    tm, tn = min(tm, m), min(tn, n)
    nbytes = tm * k * 2 + k * tn * 2 + tm * tn * (4 + jnp.dtype(out_dtype).itemsize)
    return pl.pallas_call(
        functools.partial(_proj_kernel, act=act),
        grid=(m // tm, n // tn),
        in_specs=[
            pl.BlockSpec((tm, k), lambda i, j: (i, 0)),
            pl.BlockSpec((k, tn), lambda i, j: (0, j)),
            pl.BlockSpec((1, tn), lambda i, j: (0, j)),
        ],
        out_specs=pl.BlockSpec((tm, tn), lambda i, j: (i, j)),
        out_shape=jax.ShapeDtypeStruct((m, n), out_dtype),
        compiler_params=pltpu.CompilerParams(
            dimension_semantics=("parallel", "arbitrary"), vmem_limit_bytes=_vmem_limit(nbytes)),
        name="proj",
    )(x, w, b)


def _sgu_kernel(u_ref, v_ref, ws_ref, g_ref, b_ref, bs_ref, o_ref):
    c = SGU_CHUNK
    row = lax.broadcasted_iota(jnp.int32, (c, c), 0)
    col = lax.broadcasted_iota(jnp.int32, (c, c), 1)
    w = jnp.where(row >= col, ws_ref[0], 0.0).astype(BF16)
    ln_g, ln_b, bias = g_ref[0], b_ref[0], bs_ref[0]
    for ci in range(u_ref.shape[0] // c):
        rows = slice(ci * c, (ci + 1) * c)
        vn = _layer_norm(v_ref[rows, :].astype(F32), ln_g, ln_b)
        mixed = jnp.dot(w, vn.astype(BF16), preferred_element_type=F32) + bias
        o_ref[rows, :] = (u_ref[rows, :].astype(F32) * mixed).astype(o_ref.dtype)


def _sgu(uv, w_s, ln_g, ln_b, b_s, tm):
    t = uv.shape[0]
    width = uv.shape[1] // 2
    c = SGU_CHUNK
    ng = width // c
    tm = min(tm, t)
    return pl.pallas_call(
        _sgu_kernel,
        grid=(t // tm, ng),
        in_specs=[
            pl.BlockSpec((tm, c), lambda i, g: (i, g)),
            pl.BlockSpec((tm, c), lambda i, g: (i, g + width // c)),
            pl.BlockSpec((1, c, c), lambda i, g: (g, 0, 0)),
            pl.BlockSpec((1, 1, c), lambda i, g: (g, 0, 0)),
            pl.BlockSpec((1, 1, c), lambda i, g: (g, 0, 0)),
            pl.BlockSpec((1, c, 1), lambda i, g: (g, 0, 0)),
        ],
        out_specs=pl.BlockSpec((tm, c), lambda i, g: (i, g)),
        out_shape=jax.ShapeDtypeStruct((t, width), BF16),
        compiler_params=pltpu.CompilerParams(dimension_semantics=("parallel", "arbitrary")),
        name="sgu",
    )(uv, uv, w_s, ln_g.reshape(ng, 1, c), ln_b.reshape(ng, 1, c), b_s.reshape(ng, c, 1))


def _gla_kernel(q_ref, k_ref, v_ref, g_ref, a_ref, wg_ref, bg_ref, ng_ref, o_ref, st_ref):
    c = GLA_CHUNK
    dk = q_ref.shape[1]

    @pl.when(pl.program_id(2) == 0)
    def _():
        st_ref[...] = jnp.zeros_like(st_ref)

    row = lax.broadcasted_iota(jnp.int32, (c, c), 0)
    col = lax.broadcasted_iota(jnp.int32, (c, c), 1)
    causal = row >= col
    ones_tril = causal.astype(BF16)
    scale = dk ** -0.5
    for ci in range(q_ref.shape[0] // c):
        rows = slice(ci * c, (ci + 1) * c)
        z = jnp.dot(a_ref[rows, :], wg_ref[...], preferred_element_type=F32) + bg_ref[...]
        log_a = (jnp.minimum(z, 0.0) - jnp.log1p(jnp.exp(-jnp.abs(z)))) * (1.0 / GLA_GATE_NORM)
        la_hi = log_a.astype(BF16)
        la_lo = (log_a - la_hi.astype(F32)).astype(BF16)
        b = (jnp.dot(ones_tril, la_hi, preferred_element_type=F32)
             + jnp.dot(ones_tril, la_lo, preferred_element_type=F32))
        b_last = b[c - 1:c, :]
        q = q_ref[rows, :].astype(F32) * scale
        k = k_ref[rows, :].astype(F32)
        q_dec = (q * jnp.exp(b)).astype(BF16)
        k_inv = (k * jnp.exp(-b)).astype(BF16)
        k_to_end = (k * jnp.exp(b_last - b)).astype(BF16)
        v = v_ref[rows, :]
        attn = lax.dot_general(q_dec, k_inv, (((1,), (1,)), ((), ())), preferred_element_type=F32)
        attn = jnp.where(causal, attn, 0.0).astype(BF16)
        state_t = st_ref[...]
        o = jnp.dot(attn, v, preferred_element_type=F32) + lax.dot_general(
            q_dec, state_t.astype(BF16), (((1,), (1,)), ((), ())), preferred_element_type=F32)
        st_ref[...] = state_t * jnp.exp(b_last) + lax.dot_general(
            v, k_to_end, (((0,), (0,)), ((), ())), preferred_element_type=F32)
        o = o * lax.rsqrt(jnp.mean(jnp.square(o), axis=-1, keepdims=True) + LN_EPS) * ng_ref[...]
        gate = g_ref[rows, :].astype(F32)
        o_ref[rows, :] = (o * (gate * jax.nn.sigmoid(gate))).astype(o_ref.dtype)


def _gla(h2, a_lr, w_gate, b_gate, norm_g, batch, ts):
    t = h2.shape[0]
    seq = t // batch
    nh = GLA_HEADS
    key_dim = w_gate.shape[1]
    dk = key_dim // nh
    val_dim = norm_g.shape[1]
    dv = val_dim // nh
    ts = min(ts, seq)
    ns = seq // ts
    kb, vb, gb = key_dim // dk, 2 * key_dim // dv, (2 * key_dim + val_dim) // dv
    tok = lambda b, h, s: b * ns + s
    return pl.pallas_call(
        _gla_kernel,
        grid=(batch, nh, ns),
        in_specs=[
            pl.BlockSpec((ts, dk), lambda b, h, s: (tok(b, h, s), h)),
            pl.BlockSpec((ts, dk), lambda b, h, s: (tok(b, h, s), kb + h)),
            pl.BlockSpec((ts, dv), lambda b, h, s: (tok(b, h, s), vb + h)),
            pl.BlockSpec((ts, dv), lambda b, h, s: (tok(b, h, s), gb + h)),
            pl.BlockSpec((ts, LANES), lambda b, h, s: (tok(b, h, s), 0)),
            pl.BlockSpec((LANES, dk), lambda b, h, s: (0, h)),
            pl.BlockSpec((1, dk), lambda b, h, s: (0, h)),
            pl.BlockSpec((1, dv), lambda b, h, s: (0, h)),
        ],
        out_specs=pl.BlockSpec((ts, dv), lambda b, h, s: (tok(b, h, s), h)),
        out_shape=jax.ShapeDtypeStruct((t, val_dim), BF16),
        scratch_shapes=[pltpu.VMEM((dv, dk), F32)],
        compiler_params=pltpu.CompilerParams(
            dimension_semantics=("parallel", "parallel", "arbitrary")),
        name="gla",
    )(h2, h2, h2, h2, a_lr, w_gate, b_gate, norm_g)


def _merge_kernel(s_ref, o_ref, wa_ref, wb_ref, ga_ref, gb_ref, out_ref):
    ya = jnp.dot(s_ref[...], wa_ref[...], preferred_element_type=F32)
    yb = jnp.dot(o_ref[...], wb_ref[...], preferred_element_type=F32)
    out_ref[...] = (ga_ref[...].astype(F32) * ya + gb_ref[...].astype(F32) * yb).astype(out_ref.dtype)


def _merge(s, o, w_a, w_b, gates, tm, tn):
    t, ka = s.shape
    kb = o.shape[1]
    d = w_a.shape[1]
    tm, tn = min(tm, t), min(tn, d)
    nbytes = tm * (ka + kb) * 2 + (ka + kb) * tn * 2 + tm * tn * (2 * 2 + 2 + 8)
    return pl.pallas_call(
        _merge_kernel,
        grid=(t // tm, d // tn),
        in_specs=[
            pl.BlockSpec((tm, ka), lambda i, j: (i, 0)),
            pl.BlockSpec((tm, kb), lambda i, j: (i, 0)),
            pl.BlockSpec((ka, tn), lambda i, j: (0, j)),
            pl.BlockSpec((kb, tn), lambda i, j: (0, j)),
            pl.BlockSpec((tm, tn), lambda i, j: (i, j)),
            pl.BlockSpec((tm, tn), lambda i, j: (i, j + d // tn)),
        ],
        out_specs=pl.BlockSpec((tm, tn), lambda i, j: (i, j)),
        out_shape=jax.ShapeDtypeStruct((t, d), BF16),
        compiler_params=pltpu.CompilerParams(
            dimension_semantics=("parallel", "arbitrary"), vmem_limit_bytes=_vmem_limit(nbytes)),
        name="merge",
    )(s, o, w_a, w_b, gates, gates)


def _route(logits):
    lane = lax.broadcasted_iota(jnp.int32, logits.shape, 1).astype(F32)
    neg = float("-inf")
    big = float(LANES)
    gl = jnp.where(lane < N_GROUPS, logits, neg)
    gmax = jnp.max(gl, axis=1, keepdims=True)
    gidx = jnp.min(jnp.where(gl == gmax, lane, big), axis=1, keepdims=True)
    p_group = 1.0 / jnp.sum(jnp.exp(gl - gmax), axis=1, keepdims=True)
    lo = N_GROUPS + EXPERTS_PER_GROUP * gidx
    el = jnp.where((lane >= lo) & (lane < lo + EXPERTS_PER_GROUP), logits, neg)
    v1 = jnp.max(el, axis=1, keepdims=True)
    i1 = jnp.min(jnp.where(el == v1, lane, big), axis=1, keepdims=True)
    el2 = jnp.where(lane == i1, neg, el)
    v2 = jnp.max(el2, axis=1, keepdims=True)
    i2 = jnp.min(jnp.where(el2 == v2, lane, big), axis=1, keepdims=True)
    t = jnp.exp(v2 - v1)
    w1 = p_group / (1.0 + t)
    w2 = p_group * t / (1.0 + t)
    return jnp.where(lane == 0, i1 - N_GROUPS,
                     jnp.where(lane == 1, i2 - N_GROUPS,
                               jnp.where(lane == 2, w1, jnp.where(lane == 3, w2, 0.0))))


def _expert_hits(route):
    lane = lax.broadcasted_iota(jnp.int32, route.shape, 1).astype(F32)
    return [lane == route[:, k:k + 1] for k in range(TOP_K)]


def _out_kernel(m_ref, w_ref, x_ref, g_ref, b_ref, wr_ref, br_ref, h_ref, r_ref, cnt_ref):
    mix = jnp.dot(m_ref[...], w_ref[...], preferred_element_type=F32)
    h = _layer_norm(DEEPNORM_ALPHA * x_ref[...] + mix, g_ref[...], b_ref[...])
    h_ref[...] = h
    logits = jnp.dot(h.astype(BF16), wr_ref[...], preferred_element_type=F32) + br_ref[...]
    route = _route(logits)
    r_ref[...] = route

    @pl.when(pl.program_id(0) == 0)
    def _():
        cnt_ref[...] = jnp.zeros_like(cnt_ref)

    cnt_ref[...] += sum(jnp.sum(hit.astype(F32), axis=0, keepdims=True) for hit in _expert_hits(route))


def _out(merged, w_out, x, ln_g, ln_b, w_r, b_r, tm):
    t, d = x.shape
    tm = min(tm, t)
    nbytes = d * d * 2 + tm * d * (2 + 4 + 4 + 8) + d * LANES * 2
    return pl.pallas_call(
        _out_kernel,
        grid=(t // tm,),
        in_specs=[
            pl.BlockSpec((tm, d), lambda i: (i, 0)),
            pl.BlockSpec((d, d), lambda i: (0, 0)),
            pl.BlockSpec((tm, d), lambda i: (i, 0)),
            pl.BlockSpec((1, d), lambda i: (0, 0)),
            pl.BlockSpec((1, d), lambda i: (0, 0)),
            pl.BlockSpec((d, LANES), lambda i: (0, 0)),
            pl.BlockSpec((1, LANES), lambda i: (0, 0)),
        ],
        out_specs=[pl.BlockSpec((tm, d), lambda i: (i, 0)), pl.BlockSpec((tm, LANES), lambda i: (i, 0)),
                   pl.BlockSpec((1, LANES), lambda i: (0, 0))],
        out_shape=[jax.ShapeDtypeStruct((t, d), F32), jax.ShapeDtypeStruct((t, LANES), F32),
                   jax.ShapeDtypeStruct((1, LANES), F32)],
        compiler_params=pltpu.CompilerParams(
            dimension_semantics=("arbitrary",), vmem_limit_bytes=_vmem_limit(nbytes)),
        name="out_ln_route",
    )(merged, w_out, x, ln_g, ln_b, w_r, b_r)


def _rank_kernel(r_ref, cnt_ref, dest_ref, next_ref):
    rows = r_ref.shape[0]

    @pl.when(pl.program_id(0) == 0)
    def _():
        blocks = jnp.floor((cnt_ref[...] + (MOE_ROWS - 1)) * (1.0 / MOE_ROWS))
        k = lax.broadcasted_iota(jnp.int32, (LANES, LANES), 0)
        e = lax.broadcasted_iota(jnp.int32, (LANES, LANES), 1)
        blocks8 = jnp.broadcast_to(blocks, (8, LANES)).astype(BF16)
        first_block = jnp.dot(blocks8, (k < e).astype(BF16), preferred_element_type=F32)
        next_ref[...] = first_block[0:1, :] * MOE_ROWS

    hits = _expert_hits(r_ref[...])
    cnt = sum(hit.astype(F32) for hit in hits)
    row = lax.broadcasted_iota(jnp.int32, (rows, rows), 0)
    col = lax.broadcasted_iota(jnp.int32, (rows, rows), 1)
    earlier = (row > col).astype(BF16)
    slot = jnp.dot(earlier, cnt.astype(BF16), preferred_element_type=F32) + next_ref[...]
    dest = [jnp.sum(jnp.where(hit, slot, 0.0), axis=1, keepdims=True) for hit in hits]
    lane = lax.broadcasted_iota(jnp.int32, (rows, LANES), 1)
    dest_ref[...] = jnp.where(lane == 0, dest[0], jnp.where(lane == 1, dest[1], 0.0)).astype(jnp.int32)
    next_ref[...] += jnp.sum(cnt, axis=0, keepdims=True)


def _rank(route, counts, tm):
    t = route.shape[0]
    tm = min(tm, t)
    return pl.pallas_call(
        _rank_kernel,
        grid=(t // tm,),
        in_specs=[pl.BlockSpec((tm, LANES), lambda i: (i, 0)), pl.BlockSpec((1, LANES), lambda i: (0, 0))],
        out_specs=pl.BlockSpec((tm, LANES), lambda i: (i, 0)),
        out_shape=jax.ShapeDtypeStruct((t, LANES), jnp.int32),
        scratch_shapes=[pltpu.VMEM((1, LANES), F32)],
        compiler_params=pltpu.CompilerParams(dimension_semantics=("arbitrary",)),
        name="rank",
    )(route, counts)


def _row_copy(src, src_row, dst, dst_row, sem):
    return pltpu.make_async_copy(src.at[pl.ds(src_row, 1), :], dst.at[pl.ds(dst_row, 1), :], sem)


def _dispatch_kernel(zrow_ref, dest_ref, h_ref, xs_ref, zero_ref, sem, zsem):
    rows = h_ref.shape[0]
    zrows = zero_ref.shape[0]

    def zero_copy(e):
        start_row = pl.multiple_of(jnp.maximum(zrow_ref[e], 0), zrows)
        return pltpu.make_async_copy(zero_ref, xs_ref.at[pl.ds(start_row, zrows), :], zsem)

    @pl.when(pl.program_id(0) == 0)
    def _():
        zero_ref[...] = jnp.zeros_like(zero_ref)

        def start(e, carry):
            @pl.when(zrow_ref[e] >= 0)
            def _():
                zero_copy(e).start()
            return carry

        def wait(e, carry):
            @pl.when(zrow_ref[e] >= 0)
            def _():
                zero_copy(e).wait()
            return carry

        lax.fori_loop(0, zrow_ref.shape[0], start, 0)
        lax.fori_loop(0, zrow_ref.shape[0], wait, 0)

    def start(r, carry):
        for k in range(TOP_K):
            _row_copy(h_ref, r, xs_ref, dest_ref[0, 0, TOP_K * r + k], sem).start(priority=k)
        return carry

    lax.fori_loop(0, rows, start, 0, unroll=ROW_DMA_UNROLL)
    for k in range(TOP_K):
        pltpu.make_async_copy(h_ref, xs_ref.at[pl.ds(0, rows), :], sem).wait()


def _dispatch(h, dest, zrow, n_rows, tm):
    t, d = h.shape
    tm = min(tm, t)
    grid_spec = pltpu.PrefetchScalarGridSpec(
        num_scalar_prefetch=1,
        grid=(t // tm,),
        in_specs=[
            pl.BlockSpec((1, 1, TOP_K * tm), lambda i, z: (i, 0, 0), memory_space=pltpu.SMEM),
            pl.BlockSpec((tm, d), lambda i, z: (i, 0)),
        ],
        out_specs=pl.BlockSpec(memory_space=pl.ANY),
        scratch_shapes=[pltpu.VMEM((MOE_ROWS, d), F32), pltpu.SemaphoreType.DMA, pltpu.SemaphoreType.DMA],
    )
    return pl.pallas_call(
        _dispatch_kernel,
        grid_spec=grid_spec,
        out_shape=jax.ShapeDtypeStruct((n_rows, d), F32),
        compiler_params=pltpu.CompilerParams(dimension_semantics=("arbitrary",)),
        name="dispatch",
    )(zrow, dest.reshape(t // tm, 1, TOP_K * tm), h)


def _expert_kernel(be_ref, nu_ref, x_ref, w1_ref, w3_ref, w2_ref, y_ref, w1b_ref, w3b_ref, w2b_ref):
    i = pl.program_id(0)

    @pl.when((i == 0) | (be_ref[i] != be_ref[jnp.maximum(i - 1, 0)]))
    def _():
        w1b_ref[...] = w1_ref[0].astype(BF16)
        w3b_ref[...] = w3_ref[0].astype(BF16)
        w2b_ref[...] = w2_ref[0].astype(BF16)

    @pl.when(i < nu_ref[0])
    def _():
        x = x_ref[...].astype(BF16)
        a = jnp.dot(x, w1b_ref[...], preferred_element_type=F32)
        b = jnp.dot(x, w3b_ref[...], preferred_element_type=F32)
        mid = (a * jax.nn.sigmoid(a) * b).astype(BF16)
        y_ref[...] = jnp.dot(mid, w2b_ref[...], preferred_element_type=F32)

    @pl.when(i >= nu_ref[0])
    def _():
        y_ref[...] = jnp.zeros_like(y_ref)


def _experts(xs, block_expert, n_used, w1, w3, w2):
    p, d = xs.shape
    de = w1.shape[2]
    nb = p // MOE_ROWS
    blk = lambda i, be, nu: (jnp.minimum(i, nu[0] - 1), 0)
    wsel = lambda i, be, nu: (be[i], 0, 0)
    nbytes = MOE_ROWS * d * (4 + 4) + 3 * d * de * (4 + 1) + MOE_ROWS * de * 12
    grid_spec = pltpu.PrefetchScalarGridSpec(
        num_scalar_prefetch=2,
        grid=(nb,),
        in_specs=[
            pl.BlockSpec((MOE_ROWS, d), blk),
            pl.BlockSpec((1, d, de), wsel),
            pl.BlockSpec((1, d, de), wsel),
            pl.BlockSpec((1, de, d), wsel),
        ],
        out_specs=pl.BlockSpec((MOE_ROWS, d), lambda i, be, nu: (i, 0)),
        scratch_shapes=[pltpu.VMEM((d, de), BF16), pltpu.VMEM((d, de), BF16), pltpu.VMEM((de, d), BF16)],
    )
    return pl.pallas_call(
        _expert_kernel,
        grid_spec=grid_spec,
        out_shape=jax.ShapeDtypeStruct((p, d), F32),
        compiler_params=pltpu.CompilerParams(
            dimension_semantics=("arbitrary",), vmem_limit_bytes=_vmem_limit(nbytes)),
        name="experts",
    )(block_expert, n_used, xs, w1, w3, w2)


def _combine_kernel(dest_ref, h_ref, r_ref, y_ref, g_ref, b_ref, o_ref, buf_ref, sem):
    rows = h_ref.shape[0]

    def start(r, carry):
        for k in range(TOP_K):
            _row_copy(y_ref, dest_ref[0, 0, TOP_K * r + k], buf_ref.at[k], r, sem).start(priority=k)
        return carry

    lax.fori_loop(0, rows, start, 0, unroll=ROW_DMA_UNROLL)
    for k in range(TOP_K):
        pltpu.make_async_copy(y_ref.at[pl.ds(0, rows), :], buf_ref.at[k], sem).wait()
    route = r_ref[...]
    moe = route[:, 2:3] * buf_ref[0] + route[:, 3:4] * buf_ref[1]
    o_ref[...] = _layer_norm(DEEPNORM_ALPHA * h_ref[...] + moe, g_ref[...], b_ref[...])


def _combine(h, route, dest, y, ln_g, ln_b, tm):
    t, d = h.shape
    tm = min(tm, t)
    return pl.pallas_call(
        _combine_kernel,
        grid=(t // tm,),
        in_specs=[
            pl.BlockSpec((1, 1, TOP_K * tm), lambda i: (i, 0, 0), memory_space=pltpu.SMEM),
            pl.BlockSpec((tm, d), lambda i: (i, 0)),
            pl.BlockSpec((tm, LANES), lambda i: (i, 0)),
            pl.BlockSpec(memory_space=pl.ANY),
            pl.BlockSpec((1, d), lambda i: (0, 0)),
            pl.BlockSpec((1, d), lambda i: (0, 0)),
        ],
        out_specs=pl.BlockSpec((tm, d), lambda i: (i, 0)),
        out_shape=jax.ShapeDtypeStruct((t, d), F32),
        scratch_shapes=[pltpu.VMEM((TOP_K, tm, d), F32), pltpu.SemaphoreType.DMA],
        compiler_params=pltpu.CompilerParams(dimension_semantics=("arbitrary",)),
        name="combine",
    )(dest.reshape(t // tm, 1, TOP_K * tm), h, route, y, ln_g, ln_b)


def kernel(x, w_in, w_gate_a2, b_gate_a, sgu_ln_g, sgu_ln_b, sgu_w_s, sgu_b_s, gla_norm_g, w_branch_a, w_branch_b, w_merge, b_merge, w_out, ln1_g, ln1_b, w_router_group, b_router_group, w_router_expert, b_router_expert, w_exp_gate, w_exp_up, w_exp_down, ln2_g, ln2_b):
    batch, seq, d = x.shape
    t = batch * seq
    assert w_in.shape[0] == 1, "one layer"
    assert seq % SGU_CHUNK == 0 and seq % GLA_CHUNK == 0 and t % MOE_ROWS == 0
    sgu_width = sgu_ln_g.shape[1]
    key_dim = w_gate_a2.shape[2]
    val_dim = gla_norm_g.shape[1]
    rank = w_gate_a2.shape[1]
    xf = x.reshape(t, d)
    xb = xf.astype(BF16)
    w_in_b = w_in[0].astype(BF16)
    n_uv, n_h2 = 2 * sgu_width, 2 * key_dim + 2 * val_dim

    uv = _proj(xb, w_in_b[:, :n_uv], jnp.zeros((1, n_uv), F32), _gelu, BF16, 1024, 512)
    h2 = _proj(xb, w_in_b[:, n_uv:n_uv + n_h2], jnp.zeros((1, n_h2), F32), _identity, BF16, 1024, 512)
    w_a = jnp.pad(w_in_b[:, n_uv + n_h2:], ((0, 0), (0, LANES - rank)))
    a_lr = _proj(xb, w_a, jnp.zeros((1, LANES), F32), _identity, BF16, 1024, LANES)
    gates = _proj(xb, w_merge[0].astype(BF16), b_merge, jax.nn.sigmoid, BF16, 1024, 512)

    s = _sgu(uv, sgu_w_s[0], sgu_ln_g[0], sgu_ln_b[0], sgu_b_s[0], 1024)
    w_gate = jnp.pad(w_gate_a2[0].astype(BF16), ((0, LANES - rank), (0, 0)))
    o = _gla(h2, a_lr, w_gate, b_gate_a, gla_norm_g, batch, 512)
    merged = _merge(s, o, w_branch_a[0].astype(BF16), w_branch_b[0].astype(BF16), gates, 1024, 512)

    w_r = jnp.concatenate([w_router_group[0], w_router_expert[0]], axis=1)
    n_r = w_r.shape[1]
    w_r = jnp.pad(w_r, ((0, 0), (0, LANES - n_r))).astype(BF16)
    b_r = jnp.pad(jnp.concatenate([b_router_group, b_router_expert], axis=1), ((0, 0), (0, LANES - n_r)))
    h1, route, counts = _out(merged, w_out[0].astype(BF16), xf, ln1_g, ln1_b, w_r, b_r, 256)

    assert t * TOP_K // MOE_ROWS <= 256, "per-expert block counts must stay exact in bf16"
    dest = _rank(route, counts, 256)[:, :TOP_K]
    blocks_per_expert = (counts[0, :N_EXPERTS].astype(jnp.int32) + MOE_ROWS - 1) // MOE_ROWS
    block_ends = jnp.cumsum(blocks_per_expert)
    n_blocks = t * TOP_K // MOE_ROWS + N_EXPERTS
    n_used = block_ends[-1:]
    block_ids = jnp.minimum(jnp.arange(n_blocks, dtype=jnp.int32), n_used[0] - 1)
    block_expert = jnp.minimum(
        jnp.sum(block_ends[None, :] <= block_ids[:, None], axis=1), N_EXPERTS - 1).astype(jnp.int32)
    tail_ids = n_used[0] + jnp.arange(N_EXPERTS, dtype=jnp.int32)
    zrow = jnp.concatenate([
        jnp.where(blocks_per_expert > 0, (block_ends - 1) * MOE_ROWS, -1),
        jnp.where(tail_ids < n_blocks, tail_ids * MOE_ROWS, -1)]).astype(jnp.int32)

    xs = _dispatch(h1, dest, zrow, n_blocks * MOE_ROWS, 256)
    y = _experts(xs, block_expert, n_used.astype(jnp.int32), w_exp_gate[0], w_exp_up[0], w_exp_down[0])
    out = _combine(h1, route, dest, y, ln2_g, ln2_b, 256)
    return out.reshape(batch, seq, d)
```

```python
import functools

import jax
import jax.numpy as jnp
from jax import lax
from jax.experimental import pallas as pl
from jax.experimental.pallas import tpu as pltpu

F32 = jnp.float32
BF16 = jnp.bfloat16

SGU_CHUNK = 128
SGU_GROUPS = 8
GLA_HEADS = 4
GLA_CHUNK = 64
GLA_GATE_NORM = 16.0
N_GROUPS = 8
EXPERTS_PER_GROUP = 8
N_EXPERTS = N_GROUPS * EXPERTS_PER_GROUP
TOP_K = 2
LN_EPS = 1e-5
DEEPNORM_ALPHA = 2.0 ** 0.25

LANES = 128
MOE_ROWS = 256
ROW_DMA_UNROLL = 8
OUT_SUBTILES = 2
V7X_VMEM_BYTES = 64 * 2 ** 20


def _vmem_limit(nbytes):
    return int(min(max(2 * nbytes, 16 * 2 ** 20), V7X_VMEM_BYTES - 8 * 2 ** 20))


def _layer_norm(y, g, b):
    mu = jnp.mean(y, axis=-1, keepdims=True)
    var = jnp.mean(jnp.square(y - mu), axis=-1, keepdims=True)
    return (y - mu) * lax.rsqrt(var + LN_EPS) * g + b


def _gelu(x):
    return 0.5 * x * (1.0 + lax.erf(x * (2.0 ** -0.5)))


SLAB_ROWS = 8
HIGH_HALF = 0xFFFF0000


def _bf16_bits(x):
    return lax.bitcast_convert_type(x.astype(BF16).astype(F32), jnp.uint32)


def _store_slabs(ref, row0, x):
    rows, d = x.shape
    assert d == 2 * SLAB_ROWS * LANES
    for s in range(SLAB_ROWS):
        lo = _bf16_bits(x[:, s * LANES:(s + 1) * LANES])
        hi = _bf16_bits(x[:, d // 2 + s * LANES:d // 2 + (s + 1) * LANES])
        word = jnp.right_shift(lo, jnp.uint32(16)) | (hi & jnp.uint32(HIGH_HALF))
        ref[pl.ds(row0 * SLAB_ROWS + s, rows, stride=SLAB_ROWS), :] = word


def _load_slabs(ref, rows):
    out = []
    for s in range(SLAB_ROWS):
        word = ref[pl.ds(s, rows, stride=SLAB_ROWS), :]
        lo = lax.bitcast_convert_type(jnp.left_shift(word, jnp.uint32(16)), F32)
        hi = lax.bitcast_convert_type(word & jnp.uint32(HIGH_HALF), F32)
        out.append((lo, hi))
    return out


def _proj_kernel(x_ref, w_ref, b_ref, wa_ref, o_ref, a_ref, xb_ref, *, gelu_blocks, plain_blocks):
    j = pl.program_id(1)

    @pl.when(j == 0)
    def _():
        xb_ref[...] = x_ref[...].astype(BF16)
        a_ref[...] = jnp.dot(xb_ref[...], wa_ref[...], preferred_element_type=F32).astype(a_ref.dtype)

    def block(act):
        half = o_ref.shape[1] // 2
        for cols in (slice(0, half), slice(half, 2 * half)):
            acc = jnp.dot(xb_ref[...], w_ref[:, cols], preferred_element_type=F32)
            o_ref[:, cols] = act(acc, cols).astype(o_ref.dtype)

    @pl.when(j < gelu_blocks)
    def _():
        block(lambda acc, cols: _gelu(acc))

    @pl.when((j >= gelu_blocks) & (j < gelu_blocks + plain_blocks))
    def _():
        block(lambda acc, cols: acc)

    @pl.when(j >= gelu_blocks + plain_blocks)
    def _():
        block(lambda acc, cols: jax.nn.sigmoid(acc + b_ref[:, cols]))


def _proj(x, w, b, w_a, n_gelu, n_plain, tm, tn):
    m, k = x.shape
    n = w.shape[1]
    tm, tn = min(tm, m), min(tn, n)
    assert n_gelu % tn == 0 and n_plain % tn == 0 and n % tn == 0
    nbytes = tm * k * (4 + 1) + k * tn * 2 + tm * tn * (2 + 2) + k * LANES * 2
    return pl.pallas_call(
        functools.partial(_proj_kernel, gelu_blocks=n_gelu // tn, plain_blocks=n_plain // tn),
        grid=(m // tm, n // tn),
        in_specs=[
            pl.BlockSpec((tm, k), lambda i, j: (i, 0)),
            pl.BlockSpec((k, tn), lambda i, j: (0, j)),
            pl.BlockSpec((1, tn), lambda i, j: (0, j)),
            pl.BlockSpec((k, LANES), lambda i, j: (0, 0)),
        ],
        out_specs=[pl.BlockSpec((tm, tn), lambda i, j: (i, j)), pl.BlockSpec((tm, LANES), lambda i, j: (i, 0))],
        out_shape=[jax.ShapeDtypeStruct((m, n), BF16), jax.ShapeDtypeStruct((m, LANES), BF16)],
        scratch_shapes=[pltpu.VMEM((tm, k), BF16)],
        compiler_params=pltpu.CompilerParams(
            dimension_semantics=("parallel", "arbitrary"), vmem_limit_bytes=_vmem_limit(nbytes)),
        name="proj",
    )(x, w, b, w_a)


def _sgu_kernel(u_ref, v_ref, ws_ref, g_ref, b_ref, bs_ref, o_ref):
    c = SGU_CHUNK
    row = lax.broadcasted_iota(jnp.int32, (c, c), 0)
    col = lax.broadcasted_iota(jnp.int32, (c, c), 1)
    w = jnp.where(row >= col, ws_ref[0], 0.0).astype(BF16)
    ln_g, ln_b, bias = g_ref[0], b_ref[0], bs_ref[0]
    for ci in range(u_ref.shape[0] // c):
        rows = slice(ci * c, (ci + 1) * c)
        vn = _layer_norm(v_ref[rows, :].astype(F32), ln_g, ln_b)
        mixed = jnp.dot(w, vn.astype(BF16), preferred_element_type=F32) + bias
        o_ref[rows, :] = (u_ref[rows, :].astype(F32) * mixed).astype(o_ref.dtype)


def _sgu(uv, width, w_s, ln_g, ln_b, b_s, tm):
    t = uv.shape[0]
    c = SGU_CHUNK
    ng = width // c
    tm = min(tm, t)
    return pl.pallas_call(
        _sgu_kernel,
        grid=(t // tm, ng),
        in_specs=[
            pl.BlockSpec((tm, c), lambda i, g: (i, g)),
            pl.BlockSpec((tm, c), lambda i, g: (i, g + width // c)),
            pl.BlockSpec((1, c, c), lambda i, g: (g, 0, 0)),
            pl.BlockSpec((1, 1, c), lambda i, g: (g, 0, 0)),
            pl.BlockSpec((1, 1, c), lambda i, g: (g, 0, 0)),
            pl.BlockSpec((1, c, 1), lambda i, g: (g, 0, 0)),
        ],
        out_specs=pl.BlockSpec((tm, c), lambda i, g: (i, g)),
        out_shape=jax.ShapeDtypeStruct((t, width), BF16),
        compiler_params=pltpu.CompilerParams(dimension_semantics=("parallel", "arbitrary")),
        name="sgu",
    )(uv, uv, w_s, ln_g.reshape(ng, 1, c), ln_b.reshape(ng, 1, c), b_s.reshape(ng, c, 1))


def _gla_kernel(q_ref, k_ref, v_ref, g_ref, a_ref, wg_ref, bg_ref, ng_ref, o_ref, st_ref, sb_ref):
    c = GLA_CHUNK
    ts, dk = q_ref.shape
    grp = min(ts, 4 * c)
    contract_last = (((1,), (1,)), ((), ()))
    contract_first = (((0,), (0,)), ((), ()))

    @pl.when(pl.program_id(2) == 0)
    def _():
        st_ref[...] = jnp.zeros_like(st_ref)

    z = jnp.dot(a_ref[...], wg_ref[...], preferred_element_type=F32) + bg_ref[...]
    log_a = (jnp.minimum(z, 0.0) - jnp.log1p(jnp.exp(-jnp.abs(z)))) * (1.0 / GLA_GATE_NORM)
    la_hi = log_a.astype(BF16)
    la_lo = (log_a - la_hi.astype(F32)).astype(BF16)
    la_split = jnp.concatenate([la_hi, la_lo], axis=1)
    row = lax.broadcasted_iota(jnp.int32, (grp, grp), 0)
    col = lax.broadcasted_iota(jnp.int32, (grp, grp), 1)
    shift = c.bit_length() - 1
    causal = (row >= col) & (jnp.right_shift(row, shift) == jnp.right_shift(col, shift))
    ones_tril = causal.astype(BF16)
    b_parts = []
    for gi in range(ts // grp):
        r = jnp.dot(ones_tril, la_split[gi * grp:(gi + 1) * grp, :], preferred_element_type=F32)
        b_parts.append(r[:, :dk] + r[:, dk:])
    b = jnp.concatenate(b_parts, axis=0) if len(b_parts) > 1 else b_parts[0]

    q = q_ref[...].astype(F32) * (dk ** -0.5)
    k = k_ref[...].astype(F32)
    q_dec = (q * jnp.exp(b)).astype(BF16)
    k_inv = (k * jnp.exp(-b)).astype(BF16)

    o_intra = []
    for gi in range(ts // grp):
        rows = slice(gi * grp, (gi + 1) * grp)
        attn = lax.dot_general(q_dec[rows, :], k_inv[rows, :], contract_last, preferred_element_type=F32)
        attn = jnp.where(causal, attn, 0.0).astype(BF16)
        o_intra.append(jnp.dot(attn, v_ref[rows, :], preferred_element_type=F32))

    state_t = st_ref[...]
    for ci in range(ts // c):
        rows = slice(ci * c, (ci + 1) * c)
        b_last = b[ci * c + c - 1:ci * c + c, :]
        k_to_end = (k[rows, :] * jnp.exp(b_last - b[rows, :])).astype(BF16)
        sb_ref[ci] = state_t.astype(BF16)
        state_t = state_t * jnp.exp(b_last) + lax.dot_general(
            v_ref[rows, :], k_to_end, contract_first, preferred_element_type=F32)
    st_ref[...] = state_t

    for ci in range(ts // c):
        rows = slice(ci * c, (ci + 1) * c)
        gi, off = divmod(ci * c, grp)
        o = o_intra[gi][off:off + c, :] + lax.dot_general(
            q_dec[rows, :], sb_ref[ci], contract_last, preferred_element_type=F32)
        o = o * lax.rsqrt(jnp.mean(jnp.square(o), axis=-1, keepdims=True) + LN_EPS) * ng_ref[...]
        gate = g_ref[rows, :].astype(F32)
        o_ref[rows, :] = (o * (gate * jax.nn.sigmoid(gate))).astype(o_ref.dtype)


def _gla(h2, col0, a_lr, w_gate, b_gate, norm_g, batch, ts):
    t = h2.shape[0]
    seq = t // batch
    nh = GLA_HEADS
    key_dim = w_gate.shape[1]
    dk = key_dim // nh
    val_dim = norm_g.shape[1]
    dv = val_dim // nh
    ts = min(ts, seq)
    ns = seq // ts
    qb = col0 // dk
    kb, vb, gb = qb + key_dim // dk, (col0 + 2 * key_dim) // dv, (col0 + 2 * key_dim + val_dim) // dv
    tok = lambda b, h, s: b * ns + s
    return pl.pallas_call(
        _gla_kernel,
        grid=(batch, nh, ns),
        in_specs=[
            pl.BlockSpec((ts, dk), lambda b, h, s: (tok(b, h, s), qb + h)),
            pl.BlockSpec((ts, dk), lambda b, h, s: (tok(b, h, s), kb + h)),
            pl.BlockSpec((ts, dv), lambda b, h, s: (tok(b, h, s), vb + h)),
            pl.BlockSpec((ts, dv), lambda b, h, s: (tok(b, h, s), gb + h)),
            pl.BlockSpec((ts, LANES), lambda b, h, s: (tok(b, h, s), 0)),
            pl.BlockSpec((LANES, dk), lambda b, h, s: (0, h)),
            pl.BlockSpec((1, dk), lambda b, h, s: (0, h)),
            pl.BlockSpec((1, dv), lambda b, h, s: (0, h)),
        ],
        out_specs=pl.BlockSpec((ts, dv), lambda b, h, s: (tok(b, h, s), h)),
        out_shape=jax.ShapeDtypeStruct((t, val_dim), BF16),
        scratch_shapes=[pltpu.VMEM((dv, dk), F32), pltpu.VMEM((ts // GLA_CHUNK, dv, dk), BF16)],
        compiler_params=pltpu.CompilerParams(
            dimension_semantics=("parallel", "parallel", "arbitrary"),
            vmem_limit_bytes=_vmem_limit(ts * (dk * 48 + dv * 16) + ts // GLA_CHUNK * dv * dk * 2)),
        name="gla",
    )(h2, h2, h2, h2, a_lr, w_gate, b_gate, norm_g)


def _merge_kernel(s_ref, o_ref, wa_ref, wb_ref, ga_ref, gb_ref, out_ref):
    half = out_ref.shape[1] // 2
    for cols in (slice(0, half), slice(half, 2 * half)):
        ya = jnp.dot(s_ref[...], wa_ref[:, cols], preferred_element_type=F32)
        yb = jnp.dot(o_ref[...], wb_ref[:, cols], preferred_element_type=F32)
        out_ref[:, cols] = (ga_ref[:, cols].astype(F32) * ya
                            + gb_ref[:, cols].astype(F32) * yb).astype(out_ref.dtype)


def _merge(s, o, w_a, w_b, gates, gate_col0, tm, tn):
    t, ka = s.shape
    kb = o.shape[1]
    d = w_a.shape[1]
    tm, tn = min(tm, t), min(tn, d)
    gj = gate_col0 // tn
    nbytes = tm * (ka + kb) * 2 + (ka + kb) * tn * 2 + tm * tn * (2 * 2 + 2 + 8)
    return pl.pallas_call(
        _merge_kernel,
        grid=(t // tm, d // tn),
        in_specs=[
            pl.BlockSpec((tm, ka), lambda i, j: (i, 0)),
            pl.BlockSpec((tm, kb), lambda i, j: (i, 0)),
            pl.BlockSpec((ka, tn), lambda i, j: (0, j)),
            pl.BlockSpec((kb, tn), lambda i, j: (0, j)),
            pl.BlockSpec((tm, tn), lambda i, j: (i, gj + j)),
            pl.BlockSpec((tm, tn), lambda i, j: (i, gj + j + d // tn)),
        ],
        out_specs=pl.BlockSpec((tm, tn), lambda i, j: (i, j)),
        out_shape=jax.ShapeDtypeStruct((t, d), BF16),
        compiler_params=pltpu.CompilerParams(
            dimension_semantics=("parallel", "arbitrary"), vmem_limit_bytes=_vmem_limit(nbytes)),
        name="merge",
    )(s, o, w_a, w_b, gates, gates)


def _route(logits):
    lane = lax.broadcasted_iota(jnp.int32, logits.shape, 1).astype(F32)
    neg = float("-inf")
    big = float(LANES)
    gl = jnp.where(lane < N_GROUPS, logits, neg)
    gmax = jnp.max(gl, axis=1, keepdims=True)
    gidx = jnp.min(jnp.where(gl == gmax, lane, big), axis=1, keepdims=True)
    p_group = 1.0 / jnp.sum(jnp.exp(gl - gmax), axis=1, keepdims=True)
    lo = N_GROUPS + EXPERTS_PER_GROUP * gidx
    el = jnp.where((lane >= lo) & (lane < lo + EXPERTS_PER_GROUP), logits, neg)
    v1 = jnp.max(el, axis=1, keepdims=True)
    i1 = jnp.min(jnp.where(el == v1, lane, big), axis=1, keepdims=True)
    el2 = jnp.where(lane == i1, neg, el)
    v2 = jnp.max(el2, axis=1, keepdims=True)
    i2 = jnp.min(jnp.where(el2 == v2, lane, big), axis=1, keepdims=True)
    t = jnp.exp(v2 - v1)
    w1 = p_group / (1.0 + t)
    w2 = p_group * t / (1.0 + t)
    return jnp.where(lane == 0, i1 - N_GROUPS,
                     jnp.where(lane == 1, i2 - N_GROUPS,
                               jnp.where(lane == 2, w1, jnp.where(lane == 3, w2, 0.0))))


def _expert_hits(route):
    lane = lax.broadcasted_iota(jnp.int32, route.shape, 1).astype(F32)
    return [lane == route[:, k:k + 1] for k in range(TOP_K)]


def _out_kernel(m_ref, w_ref, x_ref, g_ref, b_ref, wr_ref, br_ref, h_ref, hs_ref, r_ref, cnt_ref):
    @pl.when(pl.program_id(0) == 0)
    def _():
        cnt_ref[...] = jnp.zeros_like(cnt_ref)

    sub = m_ref.shape[0] // OUT_SUBTILES
    for si in range(OUT_SUBTILES):
        rows = slice(si * sub, (si + 1) * sub)
        mix = jnp.dot(m_ref[rows, :], w_ref[...], preferred_element_type=F32)
        h = _layer_norm(DEEPNORM_ALPHA * x_ref[rows, :] + mix, g_ref[...], b_ref[...])
        h_ref[rows, :] = h
        _store_slabs(hs_ref, si * sub, h)
        logits = jnp.dot(h.astype(BF16), wr_ref[...], preferred_element_type=F32) + br_ref[...]
        route = _route(logits)
        r_ref[rows, :] = route
        cnt_ref[...] += sum(jnp.sum(hit.astype(F32), axis=0, keepdims=True) for hit in _expert_hits(route))


def _out(merged, w_out, x, ln_g, ln_b, w_r, b_r, tm):
    t, d = x.shape
    tm = min(tm, t)
    nbytes = d * d * 2 + tm * d * (2 + 4 + 4 + 8) + d * LANES * 2
    return pl.pallas_call(
        _out_kernel,
        grid=(t // tm,),
        in_specs=[
            pl.BlockSpec((tm, d), lambda i: (i, 0)),
            pl.BlockSpec((d, d), lambda i: (0, 0)),
            pl.BlockSpec((tm, d), lambda i: (i, 0)),
            pl.BlockSpec((1, d), lambda i: (0, 0)),
            pl.BlockSpec((1, d), lambda i: (0, 0)),
            pl.BlockSpec((d, LANES), lambda i: (0, 0)),
            pl.BlockSpec((1, LANES), lambda i: (0, 0)),
        ],
        out_specs=[pl.BlockSpec((tm, d), lambda i: (i, 0)),
                   pl.BlockSpec((tm * SLAB_ROWS, LANES), lambda i: (i, 0)),
                   pl.BlockSpec((tm, LANES), lambda i: (i, 0)),
                   pl.BlockSpec((1, LANES), lambda i: (0, 0))],
        out_shape=[jax.ShapeDtypeStruct((t, d), F32),
                   jax.ShapeDtypeStruct((t * SLAB_ROWS, LANES), jnp.uint32),
                   jax.ShapeDtypeStruct((t, LANES), F32),
                   jax.ShapeDtypeStruct((1, LANES), F32)],
        compiler_params=pltpu.CompilerParams(
            dimension_semantics=("arbitrary",), vmem_limit_bytes=_vmem_limit(nbytes)),
        name="out_ln_route",
    )(merged, w_out, x, ln_g, ln_b, w_r, b_r)


def _rank_kernel(r_ref, cnt_ref, dest_ref, next_ref):
    rows = r_ref.shape[0]

    @pl.when(pl.program_id(0) == 0)
    def _():
        blocks = jnp.floor((cnt_ref[...] + (MOE_ROWS - 1)) * (1.0 / MOE_ROWS))
        k = lax.broadcasted_iota(jnp.int32, (LANES, LANES), 0)
        e = lax.broadcasted_iota(jnp.int32, (LANES, LANES), 1)
        blocks8 = jnp.broadcast_to(blocks, (8, LANES)).astype(BF16)
        first_block = jnp.dot(blocks8, (k < e).astype(BF16), preferred_element_type=F32)
        next_ref[...] = first_block[0:1, :] * MOE_ROWS

    hits = _expert_hits(r_ref[...])
    cnt = sum(hit.astype(F32) for hit in hits)
    row = lax.broadcasted_iota(jnp.int32, (rows, rows), 0)
    col = lax.broadcasted_iota(jnp.int32, (rows, rows), 1)
    earlier = (row > col).astype(BF16)
    slot = jnp.dot(earlier, cnt.astype(BF16), preferred_element_type=F32) + next_ref[...]
    dest = [jnp.sum(jnp.where(hit, slot, 0.0), axis=1, keepdims=True) for hit in hits]
    lane = lax.broadcasted_iota(jnp.int32, (rows, LANES), 1)
    dest_ref[...] = jnp.where(lane == 0, dest[0], jnp.where(lane == 1, dest[1], 0.0)).astype(jnp.int32)
    next_ref[...] += jnp.sum(cnt, axis=0, keepdims=True)


def _rank(route, counts, tm):
    t = route.shape[0]
    tm = min(tm, t)
    return pl.pallas_call(
        _rank_kernel,
        grid=(t // tm,),
        in_specs=[pl.BlockSpec((tm, LANES), lambda i: (i, 0)), pl.BlockSpec((1, LANES), lambda i: (0, 0))],
        out_specs=pl.BlockSpec((tm, LANES), lambda i: (i, 0)),
        out_shape=jax.ShapeDtypeStruct((t, LANES), jnp.int32),
        scratch_shapes=[pltpu.VMEM((1, LANES), F32)],
        compiler_params=pltpu.CompilerParams(dimension_semantics=("arbitrary",)),
        name="rank",
    )(route, counts)


def _slab_copy(src, src_row, dst, dst_row, sem):
    s0 = pl.multiple_of(src_row * SLAB_ROWS, SLAB_ROWS)
    d0 = pl.multiple_of(dst_row * SLAB_ROWS, SLAB_ROWS)
    return pltpu.make_async_copy(src.at[pl.ds(s0, SLAB_ROWS), :], dst.at[pl.ds(d0, SLAB_ROWS), :], sem)


def _dispatch_kernel(zrow_ref, dest_ref, h_ref, xs_ref, zero_ref, sem, zsem):
    rows = h_ref.shape[0] // SLAB_ROWS
    zrows = zero_ref.shape[0]

    def zero_copy(e):
        start_row = pl.multiple_of(jnp.maximum(zrow_ref[e], 0) * SLAB_ROWS, zrows)
        return pltpu.make_async_copy(zero_ref, xs_ref.at[pl.ds(start_row, zrows), :], zsem)

    @pl.when(pl.program_id(0) == 0)
    def _():
        zero_ref[...] = jnp.zeros_like(zero_ref)

        def start(e, carry):
            @pl.when(zrow_ref[e] >= 0)
            def _():
                zero_copy(e).start()
            return carry

        def wait(e, carry):
            @pl.when(zrow_ref[e] >= 0)
            def _():
                zero_copy(e).wait()
            return carry

        lax.fori_loop(0, zrow_ref.shape[0], start, 0)
        lax.fori_loop(0, zrow_ref.shape[0], wait, 0)

    def start(r, carry):
        for k in range(TOP_K):
            _slab_copy(h_ref, r, xs_ref, dest_ref[0, 0, TOP_K * r + k], sem).start(priority=k)
        return carry

    lax.fori_loop(0, rows, start, 0, unroll=ROW_DMA_UNROLL)
    for k in range(TOP_K):
        pltpu.make_async_copy(h_ref, xs_ref.at[pl.ds(0, rows * SLAB_ROWS), :], sem).wait()


def _dispatch(hs, dest, zrow, n_rows, tm):
    t = hs.shape[0] // SLAB_ROWS
    tm = min(tm, t)
    grid_spec = pltpu.PrefetchScalarGridSpec(
        num_scalar_prefetch=1,
        grid=(t // tm,),
        in_specs=[
            pl.BlockSpec((1, 1, TOP_K * tm), lambda i, z: (i, 0, 0), memory_space=pltpu.SMEM),
            pl.BlockSpec((tm * SLAB_ROWS, LANES), lambda i, z: (i, 0)),
        ],
        out_specs=pl.BlockSpec(memory_space=pl.ANY),
        scratch_shapes=[pltpu.VMEM((MOE_ROWS * SLAB_ROWS, LANES), jnp.uint32),
                        pltpu.SemaphoreType.DMA, pltpu.SemaphoreType.DMA],
    )
    return pl.pallas_call(
        _dispatch_kernel,
        grid_spec=grid_spec,
        out_shape=jax.ShapeDtypeStruct((n_rows * SLAB_ROWS, LANES), jnp.uint32),
        compiler_params=pltpu.CompilerParams(dimension_semantics=("arbitrary",)),
        name="dispatch",
    )(zrow, dest.reshape(t // tm, 1, TOP_K * tm), hs)


def _expert_kernel(be_ref, nu_ref, x_ref, w1_ref, w3_ref, w2_ref, y_ref, w1b_ref, w3b_ref, w2b_ref):
    i = pl.program_id(0)

    d = w1_ref.shape[1]
    rows = x_ref.shape[0] // SLAB_ROWS

    @pl.when((i == 0) | (be_ref[i] != be_ref[jnp.maximum(i - 1, 0)]))
    def _():
        for s in range(SLAB_ROWS):
            for half, src0 in enumerate((s * LANES, d // 2 + s * LANES)):
                dst0 = (2 * s + half) * LANES
                w1b_ref[dst0:dst0 + LANES, :] = w1_ref[0, src0:src0 + LANES, :].astype(BF16)
                w3b_ref[dst0:dst0 + LANES, :] = w3_ref[0, src0:src0 + LANES, :].astype(BF16)
        w2b_ref[...] = w2_ref[0].astype(BF16)

    @pl.when(i < nu_ref[0])
    def _():
        x = jnp.concatenate([part.astype(BF16) for pair in _load_slabs(x_ref, rows) for part in pair], axis=1)
        a = jnp.dot(x, w1b_ref[...], preferred_element_type=F32)
        b = jnp.dot(x, w3b_ref[...], preferred_element_type=F32)
        mid = (a * jax.nn.sigmoid(a) * b).astype(BF16)
        _store_slabs(y_ref, 0, jnp.dot(mid, w2b_ref[...], preferred_element_type=F32))

    @pl.when(i >= nu_ref[0])
    def _():
        y_ref[...] = jnp.zeros_like(y_ref)


def _experts(xs, block_expert, n_used, w1, w3, w2):
    _, d, de = w1.shape
    nb = xs.shape[0] // (MOE_ROWS * SLAB_ROWS)
    blk = lambda i, be, nu: (jnp.minimum(i, nu[0] - 1), 0)
    wsel = lambda i, be, nu: (be[i], 0, 0)
    nbytes = MOE_ROWS * d * (2 + 2 + 2 + 4) + 3 * d * de * (4 + 1) + MOE_ROWS * de * 12
    grid_spec = pltpu.PrefetchScalarGridSpec(
        num_scalar_prefetch=2,
        grid=(nb,),
        in_specs=[
            pl.BlockSpec((MOE_ROWS * SLAB_ROWS, LANES), blk),
            pl.BlockSpec((1, d, de), wsel),
            pl.BlockSpec((1, d, de), wsel),
            pl.BlockSpec((1, de, d), wsel),
        ],
        out_specs=pl.BlockSpec((MOE_ROWS * SLAB_ROWS, LANES), lambda i, be, nu: (i, 0)),
        scratch_shapes=[pltpu.VMEM((d, de), BF16), pltpu.VMEM((d, de), BF16), pltpu.VMEM((de, d), BF16)],
    )
    return pl.pallas_call(
        _expert_kernel,
        grid_spec=grid_spec,
        out_shape=jax.ShapeDtypeStruct(xs.shape, jnp.uint32),
        compiler_params=pltpu.CompilerParams(
            dimension_semantics=("arbitrary",), vmem_limit_bytes=_vmem_limit(nbytes)),
        name="experts",
    )(block_expert, n_used, xs, w1, w3, w2)


def _combine_kernel(dest_ref, h_ref, r_ref, y_ref, g_ref, b_ref, o_ref, buf_ref, sem):
    rows = h_ref.shape[0]

    def start(r, carry):
        for k in range(TOP_K):
            _slab_copy(y_ref, dest_ref[0, 0, TOP_K * r + k], buf_ref.at[k], r, sem).start(priority=k)
        return carry

    lax.fori_loop(0, rows, start, 0, unroll=ROW_DMA_UNROLL)
    for k in range(TOP_K):
        pltpu.make_async_copy(y_ref.at[pl.ds(0, rows * SLAB_ROWS), :], buf_ref.at[k], sem).wait()
    route = r_ref[...]
    lo, hi = [], []
    slabs = [_load_slabs(buf_ref.at[k], rows) for k in range(TOP_K)]
    for s in range(SLAB_ROWS):
        for half, out in enumerate((lo, hi)):
            out.append(sum(route[:, 2 + k:3 + k] * slabs[k][s][half] for k in range(TOP_K)))
    moe = jnp.concatenate(lo + hi, axis=1)
    o_ref[...] = _layer_norm(DEEPNORM_ALPHA * h_ref[...] + moe, g_ref[...], b_ref[...])


def _combine(h, route, dest, y, ln_g, ln_b, tm):
    t, d = h.shape
    tm = min(tm, t)
    return pl.pallas_call(
        _combine_kernel,
        grid=(t // tm,),
        in_specs=[
            pl.BlockSpec((1, 1, TOP_K * tm), lambda i: (i, 0, 0), memory_space=pltpu.SMEM),
            pl.BlockSpec((tm, d), lambda i: (i, 0)),
            pl.BlockSpec((tm, LANES), lambda i: (i, 0)),
            pl.BlockSpec(memory_space=pl.ANY),
            pl.BlockSpec((1, d), lambda i: (0, 0)),
            pl.BlockSpec((1, d), lambda i: (0, 0)),
        ],
        out_specs=pl.BlockSpec((tm, d), lambda i: (i, 0)),
        out_shape=jax.ShapeDtypeStruct((t, d), F32),
        scratch_shapes=[pltpu.VMEM((TOP_K, tm * SLAB_ROWS, LANES), jnp.uint32), pltpu.SemaphoreType.DMA],
        compiler_params=pltpu.CompilerParams(dimension_semantics=("arbitrary",)),
        name="combine",
    )(dest.reshape(t // tm, 1, TOP_K * tm), h, route, y, ln_g, ln_b)


def kernel(x, w_in, w_gate_a2, b_gate_a, sgu_ln_g, sgu_ln_b, sgu_w_s, sgu_b_s, gla_norm_g, w_branch_a, w_branch_b, w_merge, b_merge, w_out, ln1_g, ln1_b, w_router_group, b_router_group, w_router_expert, b_router_expert, w_exp_gate, w_exp_up, w_exp_down, ln2_g, ln2_b):
    batch, seq, d = x.shape
    t = batch * seq
    assert w_in.shape[0] == 1, "one layer"
    assert seq % SGU_CHUNK == 0 and seq % GLA_CHUNK == 0 and t % MOE_ROWS == 0
    sgu_width = sgu_ln_g.shape[1]
    key_dim = w_gate_a2.shape[2]
    val_dim = gla_norm_g.shape[1]
    rank = w_gate_a2.shape[1]
    xf = x.reshape(t, d)
    w_in_b = w_in[0].astype(BF16)
    n_uv, n_h2 = 2 * sgu_width, 2 * key_dim + 2 * val_dim

    w_all = jnp.concatenate([w_in_b[:, :n_uv + n_h2], w_merge[0].astype(BF16)], axis=1)
    b_all = jnp.concatenate([jnp.zeros((1, n_uv + n_h2), F32), b_merge], axis=1)
    w_a = jnp.pad(w_in_b[:, n_uv + n_h2:], ((0, 0), (0, LANES - rank)))
    p, a_lr = _proj(xf, w_all, b_all, w_a, n_uv, n_h2, 1024, 1024)

    s = _sgu(p, sgu_width, sgu_w_s[0], sgu_ln_g[0], sgu_ln_b[0], sgu_b_s[0], 1024)
    w_gate = jnp.pad(w_gate_a2[0].astype(BF16), ((0, LANES - rank), (0, 0)))
    o = _gla(p, n_uv, a_lr, w_gate, b_gate_a, gla_norm_g, batch, 512)
    merged = _merge(s, o, w_branch_a[0].astype(BF16), w_branch_b[0].astype(BF16), p, n_uv + n_h2, 1024, 1024)

    w_r = jnp.concatenate([w_router_group[0], w_router_expert[0]], axis=1)
    n_r = w_r.shape[1]
    w_r = jnp.pad(w_r, ((0, 0), (0, LANES - n_r))).astype(BF16)
    b_r = jnp.pad(jnp.concatenate([b_router_group, b_router_expert], axis=1), ((0, 0), (0, LANES - n_r)))
    h1, h1_slabs, route, counts = _out(merged, w_out[0].astype(BF16), xf, ln1_g, ln1_b, w_r, b_r, 512)

    assert t * TOP_K // MOE_ROWS <= 256, "per-expert block counts must stay exact in bf16"
    dest = _rank(route, counts, 256)[:, :TOP_K]
    blocks_per_expert = (counts[0, :N_EXPERTS].astype(jnp.int32) + MOE_ROWS - 1) // MOE_ROWS
    block_ends = jnp.cumsum(blocks_per_expert)
    n_blocks = t * TOP_K // MOE_ROWS + N_EXPERTS
    n_used = block_ends[-1:]
    block_ids = jnp.minimum(jnp.arange(n_blocks, dtype=jnp.int32), n_used[0] - 1)
    block_expert = jnp.minimum(
        jnp.sum(block_ends[None, :] <= block_ids[:, None], axis=1), N_EXPERTS - 1).astype(jnp.int32)
    tail_ids = n_used[0] + jnp.arange(N_EXPERTS, dtype=jnp.int32)
    zrow = jnp.concatenate([
        jnp.where(blocks_per_expert > 0, (block_ends - 1) * MOE_ROWS, -1),
        jnp.where(tail_ids < n_blocks, tail_ids * MOE_ROWS, -1)]).astype(jnp.int32)

    xs = _dispatch(h1_slabs, dest, zrow, n_blocks * MOE_ROWS, 256)
    y = _experts(xs, block_expert, n_used.astype(jnp.int32), w_exp_gate[0], w_exp_up[0], w_exp_down[0])
    out = _combine(h1, route, dest, y, ln2_g, ln2_b, 256)
    return out.reshape(batch, seq, d)
```

```python
import functools

import jax
import jax.numpy as jnp
from jax import lax
from jax.experimental import pallas as pl
from jax.experimental.pallas import tpu as pltpu

F32 = jnp.float32
BF16 = jnp.bfloat16

SGU_CHUNK = 128
SGU_GROUPS = 8
GLA_HEADS = 4
GLA_CHUNK = 64
GLA_GATE_NORM = 16.0
N_GROUPS = 8
EXPERTS_PER_GROUP = 8
N_EXPERTS = N_GROUPS * EXPERTS_PER_GROUP
TOP_K = 2
LN_EPS = 1e-5
DEEPNORM_ALPHA = 2.0 ** 0.25

LANES = 128
MOE_ROWS = 256
ROW_DMA_UNROLL = 8
OUT_SUBTILES = 2
COMBINE_CHUNK = 32
V7X_VMEM_BYTES = 64 * 2 ** 20


def _vmem_limit(nbytes):
    return int(min(max(2 * nbytes, 16 * 2 ** 20), V7X_VMEM_BYTES - 8 * 2 ** 20))


def _layer_norm(y, g, b):
    mu = jnp.mean(y, axis=-1, keepdims=True)
    var = jnp.mean(jnp.square(y - mu), axis=-1, keepdims=True)
    return (y - mu) * lax.rsqrt(var + LN_EPS) * g + b


def _gelu(x):
    return 0.5 * x * (1.0 + lax.erf(x * (2.0 ** -0.5)))


SLAB_ROWS = 8
HIGH_HALF = 0xFFFF0000


def _bf16_bits(x):
    return lax.bitcast_convert_type(x.astype(BF16).astype(F32), jnp.uint32)


def _store_slabs(ref, row0, x):
    rows, d = x.shape
    assert d == 2 * SLAB_ROWS * LANES
    for s in range(SLAB_ROWS):
        lo = _bf16_bits(x[:, s * LANES:(s + 1) * LANES])
        hi = _bf16_bits(x[:, d // 2 + s * LANES:d // 2 + (s + 1) * LANES])
        word = jnp.right_shift(lo, jnp.uint32(16)) | (hi & jnp.uint32(HIGH_HALF))
        ref[pl.ds(row0 * SLAB_ROWS + s, rows, stride=SLAB_ROWS), :] = word


def _load_slabs(ref, rows, row0=0):
    out = []
    for s in range(SLAB_ROWS):
        word = ref[pl.ds(row0 * SLAB_ROWS + s, rows, stride=SLAB_ROWS), :]
        lo = lax.bitcast_convert_type(jnp.left_shift(word, jnp.uint32(16)), F32)
        hi = lax.bitcast_convert_type(word & jnp.uint32(HIGH_HALF), F32)
        out.append((lo, hi))
    return out


def _proj_kernel(x_ref, w_ref, b_ref, wa_ref, o_ref, a_ref, xb_ref, *, gelu_blocks, plain_blocks, silu_blocks):
    j = pl.program_id(1)
    silu_start = gelu_blocks + plain_blocks
    sigmoid_start = silu_start + silu_blocks

    @pl.when(j == 0)
    def _():
        xb_ref[...] = x_ref[...].astype(BF16)
        a_ref[...] = jnp.dot(xb_ref[...], wa_ref[...], preferred_element_type=F32).astype(a_ref.dtype)

    def block(act):
        half = o_ref.shape[1] // 2
        for cols in (slice(0, half), slice(half, 2 * half)):
            acc = jnp.dot(xb_ref[...], w_ref[:, cols], preferred_element_type=F32)
            o_ref[:, cols] = act(acc, cols).astype(o_ref.dtype)

    @pl.when(j < gelu_blocks)
    def _():
        block(lambda acc, cols: _gelu(acc))

    @pl.when((j >= gelu_blocks) & (j < silu_start))
    def _():
        block(lambda acc, cols: acc)

    @pl.when((j >= silu_start) & (j < sigmoid_start))
    def _():
        block(lambda acc, cols: acc * jax.nn.sigmoid(acc))

    @pl.when(j >= sigmoid_start)
    def _():
        block(lambda acc, cols: jax.nn.sigmoid(acc + b_ref[:, cols]))


def _proj(x, w, b, w_a, n_gelu, n_plain, n_silu, tm, tn):
    m, k = x.shape
    n = w.shape[1]
    tm, tn = min(tm, m), min(tn, n)
    assert n_gelu % tn == 0 and n_plain % tn == 0 and n_silu % tn == 0 and n % tn == 0
    nbytes = tm * k * (4 + 1) + k * tn * 2 + tm * tn * (2 + 2) + k * LANES * 2
    return pl.pallas_call(
        functools.partial(_proj_kernel, gelu_blocks=n_gelu // tn, plain_blocks=n_plain // tn,
                          silu_blocks=n_silu // tn),
        grid=(m // tm, n // tn),
        in_specs=[
            pl.BlockSpec((tm, k), lambda i, j: (i, 0)),
            pl.BlockSpec((k, tn), lambda i, j: (0, j)),
            pl.BlockSpec((1, tn), lambda i, j: (0, j)),
            pl.BlockSpec((k, LANES), lambda i, j: (0, 0)),
        ],
        out_specs=[pl.BlockSpec((tm, tn), lambda i, j: (i, j)), pl.BlockSpec((tm, LANES), lambda i, j: (i, 0))],
        out_shape=[jax.ShapeDtypeStruct((m, n), BF16), jax.ShapeDtypeStruct((m, LANES), BF16)],
        scratch_shapes=[pltpu.VMEM((tm, k), BF16)],
        compiler_params=pltpu.CompilerParams(
            dimension_semantics=("parallel", "arbitrary"), vmem_limit_bytes=_vmem_limit(nbytes)),
        name="proj",
    )(x, w, b, w_a)


def _sgu_kernel(u_ref, v_ref, ws_ref, g_ref, b_ref, bs_ref, o_ref):
    c = SGU_CHUNK
    row = lax.broadcasted_iota(jnp.int32, (c, c), 0)
    col = lax.broadcasted_iota(jnp.int32, (c, c), 1)
    causal = row >= col
    for g in range(ws_ref.shape[0]):
        cols = slice(g * c, (g + 1) * c)
        w = jnp.where(causal, ws_ref[g], 0.0).astype(BF16)
        ln_g, ln_b, bias = g_ref[:, cols], b_ref[:, cols], bs_ref[g]
        for ci in range(u_ref.shape[0] // c):
            rows = slice(ci * c, (ci + 1) * c)
            vn = _layer_norm(v_ref[rows, cols].astype(F32), ln_g, ln_b)
            mixed = jnp.dot(w, vn.astype(BF16), preferred_element_type=F32) + bias
            o_ref[rows, cols] = (u_ref[rows, cols].astype(F32) * mixed).astype(o_ref.dtype)


def _sgu(uv, width, w_s, ln_g, ln_b, b_s, tm):
    t = uv.shape[0]
    ng, c, _ = w_s.shape
    tm = min(tm, t)
    return pl.pallas_call(
        _sgu_kernel,
        grid=(t // tm,),
        in_specs=[
            pl.BlockSpec((tm, width), lambda i: (i, 0)),
            pl.BlockSpec((tm, width), lambda i: (i, 1)),
            pl.BlockSpec((ng, c, c), lambda i: (0, 0, 0)),
            pl.BlockSpec((1, width), lambda i: (0, 0)),
            pl.BlockSpec((1, width), lambda i: (0, 0)),
            pl.BlockSpec((ng, c, 1), lambda i: (0, 0, 0)),
        ],
        out_specs=pl.BlockSpec((tm, width), lambda i: (i, 0)),
        out_shape=jax.ShapeDtypeStruct((t, width), BF16),
        compiler_params=pltpu.CompilerParams(dimension_semantics=("parallel",)),
        name="sgu",
    )(uv, uv, w_s, ln_g.reshape(1, width), ln_b.reshape(1, width), b_s.reshape(ng, c, 1))


def _gla_kernel(q_ref, k_ref, v_ref, g_ref, a_ref, wg_ref, bg_ref, ng_ref, o_ref, st_ref, sb_ref):
    c = GLA_CHUNK
    ts, dk = q_ref.shape
    grp = min(ts, 4 * c)
    contract_last = (((1,), (1,)), ((), ()))
    contract_first = (((0,), (0,)), ((), ()))

    @pl.when(pl.program_id(2) == 0)
    def _():
        st_ref[...] = jnp.zeros_like(st_ref)

    z = jnp.dot(a_ref[...], wg_ref[...], preferred_element_type=F32) + bg_ref[...]
    log_a = (jnp.minimum(z, 0.0) - jnp.log1p(jnp.exp(-jnp.abs(z)))) * (1.0 / GLA_GATE_NORM)
    la_hi = log_a.astype(BF16)
    la_lo = (log_a - la_hi.astype(F32)).astype(BF16)
    la_split = jnp.concatenate([la_hi, la_lo], axis=1)
    row = lax.broadcasted_iota(jnp.int32, (grp, grp), 0)
    col = lax.broadcasted_iota(jnp.int32, (grp, grp), 1)
    shift = c.bit_length() - 1
    causal = (row >= col) & (jnp.right_shift(row, shift) == jnp.right_shift(col, shift))
    ones_tril = causal.astype(BF16)
    b_parts = []
    for gi in range(ts // grp):
        r = jnp.dot(ones_tril, la_split[gi * grp:(gi + 1) * grp, :], preferred_element_type=F32)
        b_parts.append(r[:, :dk] + r[:, dk:])
    b = jnp.concatenate(b_parts, axis=0) if len(b_parts) > 1 else b_parts[0]

    q = q_ref[...].astype(F32) * (dk ** -0.5)
    k = k_ref[...].astype(F32)
    q_dec = (q * jnp.exp(b)).astype(BF16)
    k_inv = (k * jnp.exp(-b)).astype(BF16)

    o_intra = []
    for gi in range(ts // grp):
        rows = slice(gi * grp, (gi + 1) * grp)
        attn = lax.dot_general(q_dec[rows, :], k_inv[rows, :], contract_last, preferred_element_type=F32)
        attn = jnp.where(causal, attn, 0.0).astype(BF16)
        o_intra.append(jnp.dot(attn, v_ref[rows, :], preferred_element_type=F32))

    state_t = st_ref[...]
    for ci in range(ts // c):
        rows = slice(ci * c, (ci + 1) * c)
        b_last = b[ci * c + c - 1:ci * c + c, :]
        k_to_end = (k[rows, :] * jnp.exp(b_last - b[rows, :])).astype(BF16)
        sb_ref[ci] = state_t.astype(BF16)
        state_t = state_t * jnp.exp(b_last) + lax.dot_general(
            v_ref[rows, :], k_to_end, contract_first, preferred_element_type=F32)
    st_ref[...] = state_t

    for ci in range(ts // c):
        rows = slice(ci * c, (ci + 1) * c)
        gi, off = divmod(ci * c, grp)
        o = o_intra[gi][off:off + c, :] + lax.dot_general(
            q_dec[rows, :], sb_ref[ci], contract_last, preferred_element_type=F32)
        o = o * lax.rsqrt(jnp.mean(jnp.square(o), axis=-1, keepdims=True) + LN_EPS) * ng_ref[...]
        o_ref[rows, :] = (o * g_ref[rows, :].astype(F32)).astype(o_ref.dtype)


def _gla(h2, col0, a_lr, w_gate, b_gate, norm_g, batch, ts):
    t = h2.shape[0]
    seq = t // batch
    nh = GLA_HEADS
    key_dim = w_gate.shape[1]
    dk = key_dim // nh
    val_dim = norm_g.shape[1]
    dv = val_dim // nh
    ts = min(ts, seq)
    ns = seq // ts
    qb = col0 // dk
    kb, vb, gb = qb + key_dim // dk, (col0 + 2 * key_dim) // dv, (col0 + 2 * key_dim + val_dim) // dv
    tok = lambda b, h, s: b * ns + s
    return pl.pallas_call(
        _gla_kernel,
        grid=(batch, nh, ns),
        in_specs=[
            pl.BlockSpec((ts, dk), lambda b, h, s: (tok(b, h, s), qb + h)),
            pl.BlockSpec((ts, dk), lambda b, h, s: (tok(b, h, s), kb + h)),
            pl.BlockSpec((ts, dv), lambda b, h, s: (tok(b, h, s), vb + h)),
            pl.BlockSpec((ts, dv), lambda b, h, s: (tok(b, h, s), gb + h)),
            pl.BlockSpec((ts, LANES), lambda b, h, s: (tok(b, h, s), 0)),
            pl.BlockSpec((LANES, dk), lambda b, h, s: (0, h)),
            pl.BlockSpec((1, dk), lambda b, h, s: (0, h)),
            pl.BlockSpec((1, dv), lambda b, h, s: (0, h)),
        ],
        out_specs=pl.BlockSpec((ts, dv), lambda b, h, s: (tok(b, h, s), h)),
        out_shape=jax.ShapeDtypeStruct((t, val_dim), BF16),
        scratch_shapes=[pltpu.VMEM((dv, dk), F32), pltpu.VMEM((ts // GLA_CHUNK, dv, dk), BF16)],
        compiler_params=pltpu.CompilerParams(
            dimension_semantics=("parallel", "parallel", "arbitrary"),
            vmem_limit_bytes=_vmem_limit(ts * (dk * 48 + dv * 16) + ts // GLA_CHUNK * dv * dk * 2)),
        name="gla",
    )(h2, h2, h2, h2, a_lr, w_gate, b_gate, norm_g)


def _merge_kernel(s_ref, o_ref, wa_ref, wb_ref, ga_ref, gb_ref, out_ref):
    half = out_ref.shape[1] // 2
    for cols in (slice(0, half), slice(half, 2 * half)):
        ya = jnp.dot(s_ref[...], wa_ref[:, cols], preferred_element_type=F32)
        yb = jnp.dot(o_ref[...], wb_ref[:, cols], preferred_element_type=F32)
        out_ref[:, cols] = (ga_ref[:, cols].astype(F32) * ya
                            + gb_ref[:, cols].astype(F32) * yb).astype(out_ref.dtype)


def _merge(s, o, w_a, w_b, gates, gate_col0, tm, tn):
    t, ka = s.shape
    kb = o.shape[1]
    d = w_a.shape[1]
    tm, tn = min(tm, t), min(tn, d)
    gj = gate_col0 // tn
    nbytes = tm * (ka + kb) * 2 + (ka + kb) * tn * 2 + tm * tn * (2 * 2 + 2 + 8)
    return pl.pallas_call(
        _merge_kernel,
        grid=(t // tm, d // tn),
        in_specs=[
            pl.BlockSpec((tm, ka), lambda i, j: (i, 0)),
            pl.BlockSpec((tm, kb), lambda i, j: (i, 0)),
            pl.BlockSpec((ka, tn), lambda i, j: (0, j)),
            pl.BlockSpec((kb, tn), lambda i, j: (0, j)),
            pl.BlockSpec((tm, tn), lambda i, j: (i, gj + j)),
            pl.BlockSpec((tm, tn), lambda i, j: (i, gj + j + d // tn)),
        ],
        out_specs=pl.BlockSpec((tm, tn), lambda i, j: (i, j)),
        out_shape=jax.ShapeDtypeStruct((t, d), BF16),
        compiler_params=pltpu.CompilerParams(
            dimension_semantics=("parallel", "arbitrary"), vmem_limit_bytes=_vmem_limit(nbytes)),
        name="merge",
    )(s, o, w_a, w_b, gates, gates)


def _route(logits):
    lane = lax.broadcasted_iota(jnp.int32, logits.shape, 1).astype(F32)
    neg = float("-inf")
    big = float(LANES)
    gl = jnp.where(lane < N_GROUPS, logits, neg)
    gmax = jnp.max(gl, axis=1, keepdims=True)
    gidx = jnp.min(jnp.where(gl == gmax, lane, big), axis=1, keepdims=True)
    p_group = 1.0 / jnp.sum(jnp.exp(gl - gmax), axis=1, keepdims=True)
    lo = N_GROUPS + EXPERTS_PER_GROUP * gidx
    el = jnp.where((lane >= lo) & (lane < lo + EXPERTS_PER_GROUP), logits, neg)
    v1 = jnp.max(el, axis=1, keepdims=True)
    i1 = jnp.min(jnp.where(el == v1, lane, big), axis=1, keepdims=True)
    el2 = jnp.where(lane == i1, neg, el)
    v2 = jnp.max(el2, axis=1, keepdims=True)
    i2 = jnp.min(jnp.where(el2 == v2, lane, big), axis=1, keepdims=True)
    t = jnp.exp(v2 - v1)
    w1 = p_group / (1.0 + t)
    w2 = p_group * t / (1.0 + t)
    return jnp.where(lane == 0, i1 - N_GROUPS,
                     jnp.where(lane == 1, i2 - N_GROUPS,
                               jnp.where(lane == 2, w1, jnp.where(lane == 3, w2, 0.0))))


def _expert_hits(route):
    lane = lax.broadcasted_iota(jnp.int32, route.shape, 1).astype(F32)
    return [lane == route[:, k:k + 1] for k in range(TOP_K)]


def _out_kernel(m_ref, w_ref, x_ref, g_ref, b_ref, wr_ref, br_ref, h_ref, hs_ref, r_ref, cnt_ref, mix_ref):
    i = pl.program_id(0)

    @pl.when(i == 0)
    def _():
        cnt_ref[...] = jnp.zeros_like(cnt_ref)
        mix_ref[1] = jnp.zeros(mix_ref.shape[1:], F32)

    has_prev = (i > 0).astype(F32)
    sub = m_ref.shape[0] // OUT_SUBTILES

    def step(cur):
        for si in range(OUT_SUBTILES):
            rows = slice(si * sub, (si + 1) * sub)
            mix_ref[cur, rows, :] = jnp.dot(m_ref[rows, :], w_ref[...], preferred_element_type=F32)
            h = _layer_norm(DEEPNORM_ALPHA * x_ref[rows, :] + mix_ref[1 - cur, rows, :], g_ref[...], b_ref[...])
            h_ref[rows, :] = h
            _store_slabs(hs_ref, si * sub, h)
            logits = jnp.dot(h.astype(BF16), wr_ref[...], preferred_element_type=F32) + br_ref[...]
            route = _route(logits)
            r_ref[rows, :] = route
            cnt_ref[...] += has_prev * sum(
                jnp.sum(hit.astype(F32), axis=0, keepdims=True) for hit in _expert_hits(route))

    for parity in range(2):
        pl.when(lax.rem(i, 2) == parity)(functools.partial(step, parity))


def _out(merged, w_out, x, ln_g, ln_b, w_r, b_r, tm):
    t, d = x.shape
    tm = min(tm, t)
    n = t // tm
    nbytes = d * d * 2 + tm * d * (2 + 4 + 4 + 2 + 8 + 8) + d * LANES * 2
    cur = lambda i: (jnp.minimum(i, n - 1), 0)
    prev = lambda i: (jnp.maximum(i - 1, 0), 0)
    return pl.pallas_call(
        _out_kernel,
        grid=(n + 1,),
        in_specs=[
            pl.BlockSpec((tm, d), cur),
            pl.BlockSpec((d, d), lambda i: (0, 0)),
            pl.BlockSpec((tm, d), prev),
            pl.BlockSpec((1, d), lambda i: (0, 0)),
            pl.BlockSpec((1, d), lambda i: (0, 0)),
            pl.BlockSpec((d, LANES), lambda i: (0, 0)),
            pl.BlockSpec((1, LANES), lambda i: (0, 0)),
        ],
        out_specs=[pl.BlockSpec((tm, d), prev),
                   pl.BlockSpec((tm * SLAB_ROWS, LANES), prev),
                   pl.BlockSpec((tm, LANES), prev),
                   pl.BlockSpec((1, LANES), lambda i: (0, 0))],
        out_shape=[jax.ShapeDtypeStruct((t, d), F32),
                   jax.ShapeDtypeStruct((t * SLAB_ROWS, LANES), jnp.uint32),
                   jax.ShapeDtypeStruct((t, LANES), F32),
                   jax.ShapeDtypeStruct((1, LANES), F32)],
        scratch_shapes=[pltpu.VMEM((2, tm, d), F32)],
        compiler_params=pltpu.CompilerParams(
            dimension_semantics=("arbitrary",), vmem_limit_bytes=_vmem_limit(nbytes)),
        name="out_ln_route",
    )(merged, w_out, x, ln_g, ln_b, w_r, b_r)


def _rank_kernel(r_ref, cnt_ref, dest_ref, next_ref):
    rows = r_ref.shape[0]

    @pl.when(pl.program_id(0) == 0)
    def _():
        blocks = jnp.floor((cnt_ref[...] + (MOE_ROWS - 1)) * (1.0 / MOE_ROWS))
        k = lax.broadcasted_iota(jnp.int32, (LANES, LANES), 0)
        e = lax.broadcasted_iota(jnp.int32, (LANES, LANES), 1)
        blocks8 = jnp.broadcast_to(blocks, (8, LANES)).astype(BF16)
        first_block = jnp.dot(blocks8, (k < e).astype(BF16), preferred_element_type=F32)
        next_ref[...] = first_block[0:1, :] * MOE_ROWS

    hits = _expert_hits(r_ref[...])
    cnt = sum(hit.astype(F32) for hit in hits)
    row = lax.broadcasted_iota(jnp.int32, (rows, rows), 0)
    col = lax.broadcasted_iota(jnp.int32, (rows, rows), 1)
    earlier = (row > col).astype(BF16)
    slot = jnp.dot(earlier, cnt.astype(BF16), preferred_element_type=F32) + next_ref[...]
    dest = [jnp.sum(jnp.where(hit, slot, 0.0), axis=1, keepdims=True) for hit in hits]
    lane = lax.broadcasted_iota(jnp.int32, (rows, LANES), 1)
    dest_ref[...] = jnp.where(lane == 0, dest[0], jnp.where(lane == 1, dest[1], 0.0)).astype(jnp.int32)
    next_ref[...] += jnp.sum(cnt, axis=0, keepdims=True)


def _rank(route, counts, tm):
    t = route.shape[0]
    tm = min(tm, t)
    return pl.pallas_call(
        _rank_kernel,
        grid=(t // tm,),
        in_specs=[pl.BlockSpec((tm, LANES), lambda i: (i, 0)), pl.BlockSpec((1, LANES), lambda i: (0, 0))],
        out_specs=pl.BlockSpec((tm, LANES), lambda i: (i, 0)),
        out_shape=jax.ShapeDtypeStruct((t, LANES), jnp.int32),
        scratch_shapes=[pltpu.VMEM((1, LANES), F32)],
        compiler_params=pltpu.CompilerParams(dimension_semantics=("arbitrary",)),
        name="rank",
    )(route, counts)


def _slab_copy(src, src_row, dst, dst_row, sem):
    s0 = pl.multiple_of(src_row * SLAB_ROWS, SLAB_ROWS)
    d0 = pl.multiple_of(dst_row * SLAB_ROWS, SLAB_ROWS)
    return pltpu.make_async_copy(src.at[pl.ds(s0, SLAB_ROWS), :], dst.at[pl.ds(d0, SLAB_ROWS), :], sem)


def _dispatch_kernel(zrow_ref, dest_ref, h_ref, xs_ref, zero_ref, sem, zsem):
    rows = h_ref.shape[0] // SLAB_ROWS
    zrows = zero_ref.shape[0]

    def zero_copy(e):
        start_row = pl.multiple_of(jnp.maximum(zrow_ref[e], 0) * SLAB_ROWS, zrows)
        return pltpu.make_async_copy(zero_ref, xs_ref.at[pl.ds(start_row, zrows), :], zsem)

    @pl.when(pl.program_id(0) == 0)
    def _():
        zero_ref[...] = jnp.zeros_like(zero_ref)

        def start(e, carry):
            @pl.when(zrow_ref[e] >= 0)
            def _():
                zero_copy(e).start()
            return carry

        def wait(e, carry):
            @pl.when(zrow_ref[e] >= 0)
            def _():
                zero_copy(e).wait()
            return carry

        lax.fori_loop(0, zrow_ref.shape[0], start, 0)
        lax.fori_loop(0, zrow_ref.shape[0], wait, 0)

    def start(r, carry):
        for k in range(TOP_K):
            _slab_copy(h_ref, r, xs_ref, dest_ref[0, 0, TOP_K * r + k], sem).start(priority=k)
        return carry

    lax.fori_loop(0, rows, start, 0, unroll=ROW_DMA_UNROLL)
    for k in range(TOP_K):
        pltpu.make_async_copy(h_ref, xs_ref.at[pl.ds(0, rows * SLAB_ROWS), :], sem).wait()


def _dispatch(hs, dest, zrow, n_rows, tm):
    t = hs.shape[0] // SLAB_ROWS
    tm = min(tm, t)
    grid_spec = pltpu.PrefetchScalarGridSpec(
        num_scalar_prefetch=1,
        grid=(t // tm,),
        in_specs=[
            pl.BlockSpec((1, 1, TOP_K * tm), lambda i, z: (i, 0, 0), memory_space=pltpu.SMEM),
            pl.BlockSpec((tm * SLAB_ROWS, LANES), lambda i, z: (i, 0)),
        ],
        out_specs=pl.BlockSpec(memory_space=pl.ANY),
        scratch_shapes=[pltpu.VMEM((MOE_ROWS * SLAB_ROWS, LANES), jnp.uint32),
                        pltpu.SemaphoreType.DMA, pltpu.SemaphoreType.DMA],
    )
    return pl.pallas_call(
        _dispatch_kernel,
        grid_spec=grid_spec,
        out_shape=jax.ShapeDtypeStruct((n_rows * SLAB_ROWS, LANES), jnp.uint32),
        compiler_params=pltpu.CompilerParams(dimension_semantics=("arbitrary",)),
        name="dispatch",
    )(zrow, dest.reshape(t // tm, 1, TOP_K * tm), hs)


def _expert_kernel(be_ref, nx_ref, nu_ref, x_ref, w1_ref, w3_ref, w2_ref, y_ref,
                   w1s_ref, w3s_ref, w2s_ref, w1b_ref, w3b_ref, w2b_ref, wsem):
    i = pl.program_id(0)
    d = w1_ref.shape[1]
    rows = x_ref.shape[0] // SLAB_ROWS

    def fetch(e):
        return [pltpu.make_async_copy(src.at[e], dst, wsem.at[n])
                for n, (src, dst) in enumerate(((w1_ref, w1s_ref), (w3_ref, w3s_ref), (w2_ref, w2s_ref)))]

    @pl.when(i == 0)
    def _():
        for copy in fetch(be_ref[0]):
            copy.start()

    @pl.when((i == 0) | (be_ref[i] != be_ref[jnp.maximum(i - 1, 0)]))
    def _():
        for copy in fetch(be_ref[i]):
            copy.wait()
        for s in range(SLAB_ROWS):
            for half, src0 in enumerate((s * LANES, d // 2 + s * LANES)):
                dst0 = (2 * s + half) * LANES
                w1b_ref[dst0:dst0 + LANES, :] = w1s_ref[src0:src0 + LANES, :].astype(BF16)
                w3b_ref[dst0:dst0 + LANES, :] = w3s_ref[src0:src0 + LANES, :].astype(BF16)
        w2b_ref[...] = w2s_ref[...].astype(BF16)

        @pl.when(nx_ref[i] >= 0)
        def _():
            for copy in fetch(nx_ref[i]):
                copy.start()

    @pl.when(i < nu_ref[0])
    def _():
        x = jnp.concatenate([part.astype(BF16) for pair in _load_slabs(x_ref, rows) for part in pair], axis=1)
        a = jnp.dot(x, w1b_ref[...], preferred_element_type=F32)
        b = jnp.dot(x, w3b_ref[...], preferred_element_type=F32)
        mid = (a * jax.nn.sigmoid(a) * b).astype(BF16)
        _store_slabs(y_ref, 0, jnp.dot(mid, w2b_ref[...], preferred_element_type=F32))

    @pl.when(i >= nu_ref[0])
    def _():
        y_ref[...] = jnp.zeros_like(y_ref)


def _experts(xs, block_expert, next_expert, n_used, w1, w3, w2):
    _, d, de = w1.shape
    nb = xs.shape[0] // (MOE_ROWS * SLAB_ROWS)
    nbytes = MOE_ROWS * d * (2 + 2 + 2 + 4) + 3 * d * de * (2 + 1) + MOE_ROWS * de * 12
    grid_spec = pltpu.PrefetchScalarGridSpec(
        num_scalar_prefetch=3,
        grid=(nb,),
        in_specs=[
            pl.BlockSpec((MOE_ROWS * SLAB_ROWS, LANES), lambda i, be, nx, nu: (jnp.minimum(i, nu[0] - 1), 0)),
            pl.BlockSpec(memory_space=pl.ANY),
            pl.BlockSpec(memory_space=pl.ANY),
            pl.BlockSpec(memory_space=pl.ANY),
        ],
        out_specs=pl.BlockSpec((MOE_ROWS * SLAB_ROWS, LANES), lambda i, be, nx, nu: (i, 0)),
        scratch_shapes=[pltpu.VMEM((d, de), F32), pltpu.VMEM((d, de), F32), pltpu.VMEM((de, d), F32),
                        pltpu.VMEM((d, de), BF16), pltpu.VMEM((d, de), BF16), pltpu.VMEM((de, d), BF16),
                        pltpu.SemaphoreType.DMA((3,))],
    )
    return pl.pallas_call(
        _expert_kernel,
        grid_spec=grid_spec,
        out_shape=jax.ShapeDtypeStruct(xs.shape, jnp.uint32),
        compiler_params=pltpu.CompilerParams(
            dimension_semantics=("arbitrary",), vmem_limit_bytes=_vmem_limit(nbytes)),
        name="experts",
    )(block_expert, next_expert, n_used, xs, w1, w3, w2)


def _combine_kernel(dest_ref, dnext_ref, h_ref, r_ref, y_ref, g_ref, b_ref, o_ref, buf_ref, sem, *, n_steps):
    i = pl.program_id(0)
    rows = h_ref.shape[0]
    chunk = min(rows, COMBINE_CHUNK)

    def gather(dref, slot, r, k):
        return _slab_copy(y_ref, dref[0, 0, TOP_K * r + k], buf_ref.at[slot, k], r, sem.at[slot])

    def wait(slot):
        for k in range(TOP_K):
            pltpu.make_async_copy(y_ref.at[pl.ds(0, rows * SLAB_ROWS), :], buf_ref.at[slot, k],
                                  sem.at[slot]).wait()

    @pl.when(i == 0)
    def _():
        def start(r, carry):
            for k in range(TOP_K):
                gather(dest_ref, 0, r, k).start(priority=k)
            return carry

        lax.fori_loop(0, rows, start, 0, unroll=ROW_DMA_UNROLL)

    def step(slot):
        wait(slot)
        for c0 in range(0, rows, chunk):
            route = r_ref[c0:c0 + chunk, :]
            slabs = [_load_slabs(buf_ref.at[slot, k], chunk, c0) for k in range(TOP_K)]
            lo, hi = [], []
            for s in range(SLAB_ROWS):
                for half, out in enumerate((lo, hi)):
                    out.append(sum(route[:, 2 + k:3 + k] * slabs[k][s][half] for k in range(TOP_K)))
            moe = jnp.concatenate(lo + hi, axis=1)
            o_ref[c0:c0 + chunk, :] = _layer_norm(
                DEEPNORM_ALPHA * h_ref[c0:c0 + chunk, :] + moe, g_ref[...], b_ref[...])
            for r in range(c0, c0 + chunk):
                for k in range(TOP_K):
                    gather(dnext_ref, 1 - slot, r, k).start(priority=k)

    for parity in range(2):
        pl.when(lax.rem(i, 2) == parity)(functools.partial(step, parity))

    @pl.when(i == n_steps - 1)
    def _():
        wait(n_steps % 2)


def _combine(h, route, dest, y, ln_g, ln_b, tm):
    t, d = h.shape
    tm = min(tm, t)
    n = t // tm
    dest = dest.reshape(n, 1, TOP_K * tm)
    return pl.pallas_call(
        functools.partial(_combine_kernel, n_steps=n),
        grid=(n,),
        in_specs=[
            pl.BlockSpec((1, 1, TOP_K * tm), lambda i: (i, 0, 0), memory_space=pltpu.SMEM),
            pl.BlockSpec((1, 1, TOP_K * tm), lambda i: (jnp.minimum(i + 1, n - 1), 0, 0), memory_space=pltpu.SMEM),
            pl.BlockSpec((tm, d), lambda i: (i, 0)),
            pl.BlockSpec((tm, LANES), lambda i: (i, 0)),
            pl.BlockSpec(memory_space=pl.ANY),
            pl.BlockSpec((1, d), lambda i: (0, 0)),
            pl.BlockSpec((1, d), lambda i: (0, 0)),
        ],
        out_specs=pl.BlockSpec((tm, d), lambda i: (i, 0)),
        out_shape=jax.ShapeDtypeStruct((t, d), F32),
        scratch_shapes=[pltpu.VMEM((2, TOP_K, tm * SLAB_ROWS, LANES), jnp.uint32), pltpu.SemaphoreType.DMA((2,))],
        compiler_params=pltpu.CompilerParams(dimension_semantics=("arbitrary",)),
        name="combine",
    )(dest, dest, h, route, y, ln_g, ln_b)


def kernel(x, w_in, w_gate_a2, b_gate_a, sgu_ln_g, sgu_ln_b, sgu_w_s, sgu_b_s, gla_norm_g, w_branch_a, w_branch_b, w_merge, b_merge, w_out, ln1_g, ln1_b, w_router_group, b_router_group, w_router_expert, b_router_expert, w_exp_gate, w_exp_up, w_exp_down, ln2_g, ln2_b):
    batch, seq, d = x.shape
    t = batch * seq
    assert w_in.shape[0] == 1, "one layer"
    assert seq % SGU_CHUNK == 0 and seq % GLA_CHUNK == 0 and t % MOE_ROWS == 0
    sgu_width = sgu_ln_g.shape[1]
    key_dim = w_gate_a2.shape[2]
    val_dim = gla_norm_g.shape[1]
    rank = w_gate_a2.shape[1]
    xf = x.reshape(t, d)
    w_in_b = w_in[0].astype(BF16)
    n_uv, n_h2 = 2 * sgu_width, 2 * key_dim + 2 * val_dim

    w_all = jnp.concatenate([w_in_b[:, :n_uv + n_h2], w_merge[0].astype(BF16)], axis=1)
    b_all = jnp.concatenate([jnp.zeros((1, n_uv + n_h2), F32), b_merge], axis=1)
    w_a = jnp.pad(w_in_b[:, n_uv + n_h2:], ((0, 0), (0, LANES - rank)))
    p, a_lr = _proj(xf, w_all, b_all, w_a, n_uv, n_h2 - val_dim, val_dim, 1024, 1024)

    s = _sgu(p, sgu_width, sgu_w_s[0], sgu_ln_g[0], sgu_ln_b[0], sgu_b_s[0], 512)
    w_gate = jnp.pad(w_gate_a2[0].astype(BF16), ((0, LANES - rank), (0, 0)))
    o = _gla(p, n_uv, a_lr, w_gate, b_gate_a, gla_norm_g, batch, 512)
    merged = _merge(s, o, w_branch_a[0].astype(BF16), w_branch_b[0].astype(BF16), p, n_uv + n_h2, 1024, 1024)

    w_r = jnp.concatenate([w_router_group[0], w_router_expert[0]], axis=1)
    n_r = w_r.shape[1]
    w_r = jnp.pad(w_r, ((0, 0), (0, LANES - n_r))).astype(BF16)
    b_r = jnp.pad(jnp.concatenate([b_router_group, b_router_expert], axis=1), ((0, 0), (0, LANES - n_r)))
    h1, h1_slabs, route, counts = _out(merged, w_out[0].astype(BF16), xf, ln1_g, ln1_b, w_r, b_r, 512)

    assert t * TOP_K // MOE_ROWS <= 256, "per-expert block counts must stay exact in bf16"
    dest = _rank(route, counts, 256)[:, :TOP_K]
    blocks_per_expert = (counts[0, :N_EXPERTS].astype(jnp.int32) + MOE_ROWS - 1) // MOE_ROWS
    block_ends = jnp.cumsum(blocks_per_expert)
    n_blocks = t * TOP_K // MOE_ROWS + N_EXPERTS
    n_used = block_ends[-1:]
    block_ids = jnp.minimum(jnp.arange(n_blocks, dtype=jnp.int32), n_used[0] - 1)
    block_expert = jnp.minimum(
        jnp.sum(block_ends[None, :] <= block_ids[:, None], axis=1), N_EXPERTS - 1).astype(jnp.int32)
    run_end = block_ends[block_expert]
    next_expert = jnp.where(run_end < n_used[0], block_expert[jnp.minimum(run_end, n_blocks - 1)], -1)
    tail_ids = n_used[0] + jnp.arange(N_EXPERTS, dtype=jnp.int32)
    zrow = jnp.concatenate([
        jnp.where(blocks_per_expert > 0, (block_ends - 1) * MOE_ROWS, -1),
        jnp.where(tail_ids < n_blocks, tail_ids * MOE_ROWS, -1)]).astype(jnp.int32)

    xs = _dispatch(h1_slabs, dest, zrow, n_blocks * MOE_ROWS, 256)
    y = _experts(xs, block_expert, next_expert.astype(jnp.int32), n_used.astype(jnp.int32),
                 w_exp_gate[0], w_exp_up[0], w_exp_down[0])
    out = _combine(h1, route, dest, y, ln2_g, ln2_b, 256)
    return out.reshape(batch, seq, d)
```

```python
import functools

import jax
import jax.numpy as jnp
from jax import lax
from jax.experimental import pallas as pl
from jax.experimental.pallas import tpu as pltpu

F32 = jnp.float32
BF16 = jnp.bfloat16

SGU_CHUNK = 128
SGU_GROUPS = 8
GLA_HEADS = 4
GLA_CHUNK = 64
GLA_GATE_NORM = 16.0
GLA_HEADS_PER_STEP = 2
GLA_GROUP_CHUNKS = 4
N_GROUPS = 8
EXPERTS_PER_GROUP = 8
N_EXPERTS = N_GROUPS * EXPERTS_PER_GROUP
TOP_K = 2
LN_EPS = 1e-5
DEEPNORM_ALPHA = 2.0 ** 0.25

LANES = 128
MOE_ROWS = 256
MOE_TOKEN_TILE = 256
ROW_DMA_UNROLL = 8
PROJ_SUBTILES = 2
OUT_SUBTILES = 2
EXPERT_SUBTILES = 1
COMBINE_CHUNK = 32
V7X_VMEM_BYTES = 64 * 2 ** 20


def _vmem_limit(nbytes):
    return int(min(max(2 * nbytes, 16 * 2 ** 20), V7X_VMEM_BYTES - 8 * 2 ** 20))


def _layer_norm(y, g, b):
    mu = jnp.mean(y, axis=-1, keepdims=True)
    var = jnp.mean(jnp.square(y - mu), axis=-1, keepdims=True)
    return (y - mu) * lax.rsqrt(var + LN_EPS) * g + b


def _gelu(x):
    return 0.5 * x * (1.0 + lax.erf(x * (2.0 ** -0.5)))


def _sigmoid(x):
    return 0.5 * (jnp.tanh(0.5 * x) + 1.0)


SLAB_ROWS = 8
HIGH_HALF = 0xFFFF0000


def _bf16_bits(x):
    return lax.bitcast_convert_type(x.astype(BF16).astype(F32), jnp.uint32)


def _store_slabs(ref, row0, x):
    rows, d = x.shape
    assert d == 2 * SLAB_ROWS * LANES
    for s in range(SLAB_ROWS):
        lo = _bf16_bits(x[:, s * LANES:(s + 1) * LANES])
        hi = _bf16_bits(x[:, d // 2 + s * LANES:d // 2 + (s + 1) * LANES])
        word = jnp.right_shift(lo, jnp.uint32(16)) | (hi & jnp.uint32(HIGH_HALF))
        ref[pl.ds(row0 * SLAB_ROWS + s, rows, stride=SLAB_ROWS), :] = word


def _load_slabs(ref, rows, row0=0):
    out = []
    for s in range(SLAB_ROWS):
        word = ref[pl.ds(row0 * SLAB_ROWS + s, rows, stride=SLAB_ROWS), :]
        lo = lax.bitcast_convert_type(jnp.left_shift(word, jnp.uint32(16)), F32)
        hi = lax.bitcast_convert_type(word & jnp.uint32(HIGH_HALF), F32)
        out.append((lo, hi))
    return out


def _proj_kernel(x_ref, w_ref, wm_ref, bm_ref, wa_ref, o_ref, a_ref, xb_ref, *,
                 gelu_blocks, plain_blocks, silu_blocks):
    j = pl.program_id(1)
    silu_start = gelu_blocks + plain_blocks
    sigmoid_start = silu_start + silu_blocks

    @pl.when(j == 0)
    def _():
        xb_ref[...] = x_ref[...].astype(BF16)
        a_ref[...] = jnp.dot(xb_ref[...], wa_ref[...], preferred_element_type=F32).astype(a_ref.dtype)

    def block(weights, act):
        sub = o_ref.shape[1] // PROJ_SUBTILES
        for cols in (slice(si * sub, (si + 1) * sub) for si in range(PROJ_SUBTILES)):
            acc = jnp.dot(xb_ref[...], weights[:, cols], preferred_element_type=F32)
            o_ref[:, cols] = act(acc, cols).astype(o_ref.dtype)

    @pl.when(j < gelu_blocks)
    def _():
        block(w_ref, lambda acc, cols: _gelu(acc))

    @pl.when((j >= gelu_blocks) & (j < silu_start))
    def _():
        block(w_ref, lambda acc, cols: acc)

    @pl.when((j >= silu_start) & (j < sigmoid_start))
    def _():
        block(w_ref, lambda acc, cols: acc * _sigmoid(acc))

    @pl.when(j >= sigmoid_start)
    def _():
        block(wm_ref, lambda acc, cols: _sigmoid(acc + bm_ref[:, cols]))


def _proj(x, w, w_m, b_m, w_a, n_gelu, n_plain, n_silu, tm, tn):
    m, k = x.shape
    n_in, n_m = n_gelu + n_plain + n_silu, w_m.shape[1]
    tm, tn = min(tm, m), min(tn, n_m)
    assert n_gelu % tn == 0 and n_plain % tn == 0 and n_silu % tn == 0 and n_m % tn == 0
    in_blocks = n_in // tn
    nbytes = tm * k * (4 + 1) + 2 * k * tn * 2 + tm * tn * (2 + 2) + k * LANES * 2
    return pl.pallas_call(
        functools.partial(_proj_kernel, gelu_blocks=n_gelu // tn, plain_blocks=n_plain // tn,
                          silu_blocks=n_silu // tn),
        grid=(m // tm, (n_in + n_m) // tn),
        in_specs=[
            pl.BlockSpec((tm, k), lambda i, j: (i, 0)),
            pl.BlockSpec((k, tn), lambda i, j: (0, jnp.minimum(j, in_blocks - 1))),
            pl.BlockSpec((k, tn), lambda i, j: (0, jnp.maximum(j - in_blocks, 0))),
            pl.BlockSpec((1, tn), lambda i, j: (0, jnp.maximum(j - in_blocks, 0))),
            pl.BlockSpec((k, LANES), lambda i, j: (0, 0)),
        ],
        out_specs=[pl.BlockSpec((tm, tn), lambda i, j: (i, j)), pl.BlockSpec((tm, LANES), lambda i, j: (i, 0))],
        out_shape=[jax.ShapeDtypeStruct((m, n_in + n_m), BF16), jax.ShapeDtypeStruct((m, LANES), BF16)],
        scratch_shapes=[pltpu.VMEM((tm, k), BF16)],
        compiler_params=pltpu.CompilerParams(
            dimension_semantics=("parallel", "arbitrary"), vmem_limit_bytes=_vmem_limit(nbytes)),
        name="proj",
    )(x, w, w_m, b_m, w_a)


def _sgu_kernel(u_ref, v_ref, ws_ref, g_ref, b_ref, bs_ref, o_ref):
    c = SGU_CHUNK
    row = lax.broadcasted_iota(jnp.int32, (c, c), 0)
    col = lax.broadcasted_iota(jnp.int32, (c, c), 1)
    causal = row >= col
    for g in range(ws_ref.shape[0]):
        cols = slice(g * c, (g + 1) * c)
        w = jnp.where(causal, ws_ref[g], 0.0).astype(BF16)
        ln_g, ln_b, bias = g_ref[:, cols], b_ref[:, cols], bs_ref[g]
        for ci in range(u_ref.shape[0] // c):
            rows = slice(ci * c, (ci + 1) * c)
            vn = _layer_norm(v_ref[rows, cols].astype(F32), ln_g, ln_b)
            mixed = jnp.dot(w, vn.astype(BF16), preferred_element_type=F32) + bias
            o_ref[rows, cols] = (u_ref[rows, cols].astype(F32) * mixed).astype(o_ref.dtype)


def _sgu(uv, width, w_s, ln_g, ln_b, b_s, tm):
    t = uv.shape[0]
    ng, c, _ = w_s.shape
    tm = min(tm, t)
    return pl.pallas_call(
        _sgu_kernel,
        grid=(t // tm,),
        in_specs=[
            pl.BlockSpec((tm, width), lambda i: (i, 0)),
            pl.BlockSpec((tm, width), lambda i: (i, 1)),
            pl.BlockSpec((ng, c, c), lambda i: (0, 0, 0)),
            pl.BlockSpec((1, width), lambda i: (0, 0)),
            pl.BlockSpec((1, width), lambda i: (0, 0)),
            pl.BlockSpec((ng, c, 1), lambda i: (0, 0, 0)),
        ],
        out_specs=pl.BlockSpec((tm, width), lambda i: (i, 0)),
        out_shape=jax.ShapeDtypeStruct((t, width), BF16),
        compiler_params=pltpu.CompilerParams(dimension_semantics=("parallel",)),
        name="sgu",
    )(uv, uv, w_s, ln_g.reshape(1, width), ln_b.reshape(1, width), b_s.reshape(ng, c, 1))


def _gla_kernel(q_ref, k_ref, v_ref, g_ref, a_ref, wg_ref, bg_ref, ng_ref, o_ref, st_ref):
    c = GLA_CHUNK
    ts = q_ref.shape[0]
    heads, dv, dk = st_ref.shape
    grp = min(ts, GLA_GROUP_CHUNKS * c)
    ncg = grp // c
    contract_last = (((1,), (1,)), ((), ()))
    contract_first = (((0,), (0,)), ((), ()))

    @pl.when(pl.program_id(2) == 0)
    def _():
        st_ref[...] = jnp.zeros_like(st_ref)

    row = lax.broadcasted_iota(jnp.int32, (grp, grp), 0)
    col = lax.broadcasted_iota(jnp.int32, (grp, grp), 1)
    shift = c.bit_length() - 1
    ones_tril = ((row >= col) & (jnp.right_shift(row, shift) == jnp.right_shift(col, shift))).astype(BF16)
    qrow = lax.broadcasted_iota(jnp.int32, (c, grp), 0)
    kcol = lax.broadcasted_iota(jnp.int32, (c, grp), 1)
    visible = [(kcol < j * c) | ((kcol < (j + 1) * c) & (kcol - j * c <= qrow)) for j in range(ncg)]
    zero_keys = jnp.zeros((c, dk), BF16)

    for hd in range(heads):
        kcols = slice(hd * dk, (hd + 1) * dk)
        vcols = slice(hd * dv, (hd + 1) * dv)
        z = jnp.dot(a_ref[...], wg_ref[:, kcols], preferred_element_type=F32) + bg_ref[:, kcols]
        log_a = (jnp.minimum(z, 0.0) - jnp.log1p(jnp.exp(-jnp.abs(z)))) * (1.0 / GLA_GATE_NORM)
        la_hi = log_a.astype(BF16)
        la_lo = (log_a - la_hi.astype(F32)).astype(BF16)
        la_split = jnp.concatenate([la_hi, la_lo], axis=1)
        state_t = st_ref[hd]
        for gi in range(ts // grp):
            rows = slice(gi * grp, (gi + 1) * grp)
            r = jnp.dot(ones_tril, la_split[rows, :], preferred_element_type=F32)
            b = r[:, :dk] + r[:, dk:]
            q = q_ref[rows, kcols].astype(F32) * (dk ** -0.5)
            k = k_ref[rows, kcols].astype(F32)
            v = v_ref[rows, vcols]
            q_dec = q * jnp.exp(b)
            k_inv = (k * jnp.exp(-b)).astype(BF16)
            chunk = [slice(j * c, (j + 1) * c) for j in range(ncg)]
            b_last = [b[(j + 1) * c - 1:(j + 1) * c, :] for j in range(ncg)]
            before = [jnp.zeros_like(b_last[0])]
            for j in range(ncg):
                before.append(before[j] + b_last[j])
            k_to_end = [k[chunk[j], :] * jnp.exp(b_last[j] - b[chunk[j], :]) for j in range(ncg)]

            q_grp = jnp.concatenate(
                [q_dec[chunk[j], :] * jnp.exp(before[j]) for j in range(ncg)], axis=0).astype(BF16)
            o = lax.dot_general(q_grp, state_t.astype(BF16), contract_last, preferred_element_type=F32)

            q_dec = q_dec.astype(BF16)
            attn = []
            for j in range(ncg):
                keys = [(k_to_end[i] * jnp.exp(before[j] - before[i + 1])).astype(BF16) for i in range(j)]
                keys += [k_inv[chunk[j], :]] + [zero_keys] * (ncg - 1 - j)
                keys = jnp.concatenate(keys, axis=0) if ncg > 1 else keys[0]
                scores = lax.dot_general(q_dec[chunk[j], :], keys, contract_last, preferred_element_type=F32)
                attn.append(jnp.where(visible[j], scores, 0.0).astype(BF16))
            attn = jnp.concatenate(attn, axis=0) if ncg > 1 else attn[0]
            o = o + jnp.dot(attn, v, preferred_element_type=F32)

            k_grp = jnp.concatenate(
                [(k_to_end[j] * jnp.exp(before[ncg] - before[j + 1])).astype(BF16) for j in range(ncg)], axis=0)
            state_t = state_t * jnp.exp(before[ncg]) + lax.dot_general(
                v, k_grp, contract_first, preferred_element_type=F32)

            o = o * lax.rsqrt(jnp.mean(jnp.square(o), axis=-1, keepdims=True) + LN_EPS) * ng_ref[:, vcols]
            o_ref[rows, vcols] = (o * g_ref[rows, vcols].astype(F32)).astype(o_ref.dtype)
        st_ref[hd] = state_t


def _gla(h2, col0, a_lr, w_gate, b_gate, norm_g, batch, ts):
    t = h2.shape[0]
    seq = t // batch
    nh = GLA_HEADS
    key_dim = w_gate.shape[1]
    dk = key_dim // nh
    val_dim = norm_g.shape[1]
    dv = val_dim // nh
    ts = min(ts, seq)
    ns = seq // ts
    hps = GLA_HEADS_PER_STEP
    wk, wv = hps * dk, hps * dv
    assert nh % hps == 0 and col0 % wk == 0 and (col0 + 2 * key_dim) % wv == 0
    qb = col0 // wk
    kb, vb, gb = qb + key_dim // wk, (col0 + 2 * key_dim) // wv, (col0 + 2 * key_dim + val_dim) // wv
    tok = lambda b, h, s: b * ns + s
    return pl.pallas_call(
        _gla_kernel,
        grid=(batch, nh // hps, ns),
        in_specs=[
            pl.BlockSpec((ts, wk), lambda b, h, s: (tok(b, h, s), qb + h)),
            pl.BlockSpec((ts, wk), lambda b, h, s: (tok(b, h, s), kb + h)),
            pl.BlockSpec((ts, wv), lambda b, h, s: (tok(b, h, s), vb + h)),
            pl.BlockSpec((ts, wv), lambda b, h, s: (tok(b, h, s), gb + h)),
            pl.BlockSpec((ts, LANES), lambda b, h, s: (tok(b, h, s), 0)),
            pl.BlockSpec((LANES, wk), lambda b, h, s: (0, h)),
            pl.BlockSpec((1, wk), lambda b, h, s: (0, h)),
            pl.BlockSpec((1, wv), lambda b, h, s: (0, h)),
        ],
        out_specs=pl.BlockSpec((ts, wv), lambda b, h, s: (tok(b, h, s), h)),
        out_shape=jax.ShapeDtypeStruct((t, val_dim), BF16),
        scratch_shapes=[pltpu.VMEM((hps, dv, dk), F32)],
        compiler_params=pltpu.CompilerParams(
            dimension_semantics=("parallel", "parallel", "arbitrary"),
            vmem_limit_bytes=_vmem_limit(hps * ts * (dk * 48 + dv * 16))),
        name="gla",
    )(h2, h2, h2, h2, a_lr, w_gate, b_gate, norm_g)


def _merge_kernel(s_ref, o_ref, wa_ref, wb_ref, ga_ref, gb_ref, out_ref):
    half = out_ref.shape[1] // 2
    for cols in (slice(0, half), slice(half, 2 * half)):
        ya = jnp.dot(s_ref[...], wa_ref[:, cols], preferred_element_type=F32)
        yb = jnp.dot(o_ref[...], wb_ref[:, cols], preferred_element_type=F32)
        out_ref[:, cols] = (ga_ref[:, cols].astype(F32) * ya
                            + gb_ref[:, cols].astype(F32) * yb).astype(out_ref.dtype)


def _merge(s, o, w_a, w_b, gates, gate_col0, tm, tn):
    t, ka = s.shape
    kb = o.shape[1]
    d = w_a.shape[1]
    tm, tn = min(tm, t), min(tn, d)
    gj = gate_col0 // tn
    nbytes = tm * (ka + kb) * 2 + (ka + kb) * tn * 2 + tm * tn * (2 * 2 + 2 + 8)
    return pl.pallas_call(
        _merge_kernel,
        grid=(t // tm, d // tn),
        in_specs=[
            pl.BlockSpec((tm, ka), lambda i, j: (i, 0)),
            pl.BlockSpec((tm, kb), lambda i, j: (i, 0)),
            pl.BlockSpec((ka, tn), lambda i, j: (0, j)),
            pl.BlockSpec((kb, tn), lambda i, j: (0, j)),
            pl.BlockSpec((tm, tn), lambda i, j: (i, gj + j)),
            pl.BlockSpec((tm, tn), lambda i, j: (i, gj + j + d // tn)),
        ],
        out_specs=pl.BlockSpec((tm, tn), lambda i, j: (i, j)),
        out_shape=jax.ShapeDtypeStruct((t, d), BF16),
        compiler_params=pltpu.CompilerParams(
            dimension_semantics=("parallel", "arbitrary"), vmem_limit_bytes=_vmem_limit(nbytes)),
        name="merge",
    )(s, o, w_a, w_b, gates, gates)


def _route(logits):
    lane = lax.broadcasted_iota(jnp.int32, logits.shape, 1).astype(F32)
    neg = float("-inf")
    big = float(LANES)
    gl = jnp.where(lane < N_GROUPS, logits, neg)
    gmax = jnp.max(gl, axis=1, keepdims=True)
    gidx = jnp.min(jnp.where(gl == gmax, lane, big), axis=1, keepdims=True)
    p_group = 1.0 / jnp.sum(jnp.exp(gl - gmax), axis=1, keepdims=True)
    lo = N_GROUPS + EXPERTS_PER_GROUP * gidx
    el = jnp.where((lane >= lo) & (lane < lo + EXPERTS_PER_GROUP), logits, neg)
    v1 = jnp.max(el, axis=1, keepdims=True)
    i1 = jnp.min(jnp.where(el == v1, lane, big), axis=1, keepdims=True)
    el2 = jnp.where(lane == i1, neg, el)
    v2 = jnp.max(el2, axis=1, keepdims=True)
    i2 = jnp.min(jnp.where(el2 == v2, lane, big), axis=1, keepdims=True)
    t = jnp.exp(v2 - v1)
    w1 = p_group / (1.0 + t)
    w2 = p_group * t / (1.0 + t)
    return jnp.where(lane == 0, i1 - N_GROUPS,
                     jnp.where(lane == 1, i2 - N_GROUPS,
                               jnp.where(lane == 2, w1, jnp.where(lane == 3, w2, 0.0))))


def _expert_hits(route):
    lane = lax.broadcasted_iota(jnp.int32, route.shape, 1).astype(F32)
    return [lane == route[:, k:k + 1] for k in range(TOP_K)]


def _out_kernel(m_ref, w_ref, x_ref, g_ref, b_ref, wr_ref, br_ref, h_ref, hs_ref, r_ref, cnt_ref, mix_ref):
    i = pl.program_id(0)

    @pl.when(i == 0)
    def _():
        cnt_ref[...] = jnp.zeros_like(cnt_ref)
        mix_ref[1] = jnp.zeros(mix_ref.shape[1:], F32)

    has_prev = (i > 0).astype(F32)
    sub = m_ref.shape[0] // OUT_SUBTILES

    def step(cur):
        for si in range(OUT_SUBTILES):
            rows = slice(si * sub, (si + 1) * sub)
            mix_ref[cur, rows, :] = jnp.dot(m_ref[rows, :], w_ref[...], preferred_element_type=F32)
            h = _layer_norm(DEEPNORM_ALPHA * x_ref[rows, :] + mix_ref[1 - cur, rows, :], g_ref[...], b_ref[...])
            h_ref[rows, :] = h
            _store_slabs(hs_ref, si * sub, h)
            logits = jnp.dot(h.astype(BF16), wr_ref[...], preferred_element_type=F32) + br_ref[...]
            route = _route(logits)
            r_ref[rows, :] = route
            cnt_ref[...] += has_prev * sum(
                jnp.sum(hit.astype(F32), axis=0, keepdims=True) for hit in _expert_hits(route))

    for parity in range(2):
        pl.when(lax.rem(i, 2) == parity)(functools.partial(step, parity))


def _out(merged, w_out, x, ln_g, ln_b, w_r, b_r, tm):
    t, d = x.shape
    tm = min(tm, t)
    n = t // tm
    nbytes = d * d * 2 + tm * d * (2 + 4 + 4 + 2 + 8 + 8) + d * LANES * 2
    cur = lambda i: (jnp.minimum(i, n - 1), 0)
    prev = lambda i: (jnp.maximum(i - 1, 0), 0)
    return pl.pallas_call(
        _out_kernel,
        grid=(n + 1,),
        in_specs=[
            pl.BlockSpec((tm, d), cur),
            pl.BlockSpec((d, d), lambda i: (0, 0)),
            pl.BlockSpec((tm, d), prev),
            pl.BlockSpec((1, d), lambda i: (0, 0)),
            pl.BlockSpec((1, d), lambda i: (0, 0)),
            pl.BlockSpec((d, LANES), lambda i: (0, 0)),
            pl.BlockSpec((1, LANES), lambda i: (0, 0)),
        ],
        out_specs=[pl.BlockSpec((tm, d), prev),
                   pl.BlockSpec((tm * SLAB_ROWS, LANES), prev),
                   pl.BlockSpec((tm, LANES), prev),
                   pl.BlockSpec((1, LANES), lambda i: (0, 0))],
        out_shape=[jax.ShapeDtypeStruct((t, d), F32),
                   jax.ShapeDtypeStruct((t * SLAB_ROWS, LANES), jnp.uint32),
                   jax.ShapeDtypeStruct((t, LANES), F32),
                   jax.ShapeDtypeStruct((1, LANES), F32)],
        scratch_shapes=[pltpu.VMEM((2, tm, d), F32)],
        compiler_params=pltpu.CompilerParams(
            dimension_semantics=("arbitrary",), vmem_limit_bytes=_vmem_limit(nbytes)),
        name="out_ln_route",
    )(merged, w_out, x, ln_g, ln_b, w_r, b_r)


def _rank_kernel(r_ref, cnt_ref, dest_ref, next_ref):
    rows = r_ref.shape[0]

    @pl.when(pl.program_id(0) == 0)
    def _():
        blocks = jnp.floor((cnt_ref[...] + (MOE_ROWS - 1)) * (1.0 / MOE_ROWS))
        k = lax.broadcasted_iota(jnp.int32, (LANES, LANES), 0)
        e = lax.broadcasted_iota(jnp.int32, (LANES, LANES), 1)
        blocks8 = jnp.broadcast_to(blocks, (8, LANES)).astype(BF16)
        first_block = jnp.dot(blocks8, (k < e).astype(BF16), preferred_element_type=F32)
        next_ref[...] = first_block[0:1, :] * MOE_ROWS

    hits = _expert_hits(r_ref[...])
    cnt = sum(hit.astype(F32) for hit in hits)
    row = lax.broadcasted_iota(jnp.int32, (rows, rows), 0)
    col = lax.broadcasted_iota(jnp.int32, (rows, rows), 1)
    earlier = (row > col).astype(BF16)
    slot = jnp.dot(earlier, cnt.astype(BF16), preferred_element_type=F32) + next_ref[...]
    dest = [jnp.sum(jnp.where(hit, slot, 0.0), axis=1, keepdims=True) for hit in hits]
    lane = lax.broadcasted_iota(jnp.int32, (rows, LANES), 1)
    by_token = jnp.where(lane == 0, dest[0], jnp.where(lane == 1, dest[1], 0.0))
    dest_ref[0] = jnp.transpose(by_token)[:TOP_K, :].astype(jnp.int32)
    next_ref[...] += jnp.sum(cnt, axis=0, keepdims=True)


def _rank(route, counts, tm):
    t = route.shape[0]
    tm = min(tm, t)
    return pl.pallas_call(
        _rank_kernel,
        grid=(t // tm,),
        in_specs=[pl.BlockSpec((tm, LANES), lambda i: (i, 0)), pl.BlockSpec((1, LANES), lambda i: (0, 0))],
        out_specs=pl.BlockSpec((1, TOP_K, tm), lambda i: (i, 0, 0)),
        out_shape=jax.ShapeDtypeStruct((t // tm, TOP_K, tm), jnp.int32),
        scratch_shapes=[pltpu.VMEM((1, LANES), F32)],
        compiler_params=pltpu.CompilerParams(dimension_semantics=("arbitrary",)),
        name="rank",
    )(route, counts)


def _slab_copy(src, src_row, dst, dst_row, sem):
    s0 = pl.multiple_of(src_row * SLAB_ROWS, SLAB_ROWS)
    d0 = pl.multiple_of(dst_row * SLAB_ROWS, SLAB_ROWS)
    return pltpu.make_async_copy(src.at[pl.ds(s0, SLAB_ROWS), :], dst.at[pl.ds(d0, SLAB_ROWS), :], sem)


def _dispatch_kernel(zrow_ref, dest_ref, h_ref, xs_ref, zero_ref, sem, zsem):
    rows = h_ref.shape[0] // SLAB_ROWS
    zrows = zero_ref.shape[0]

    def zero_copy(e):
        start_row = pl.multiple_of(jnp.maximum(zrow_ref[e], 0) * SLAB_ROWS, zrows)
        return pltpu.make_async_copy(zero_ref, xs_ref.at[pl.ds(start_row, zrows), :], zsem)

    @pl.when(pl.program_id(0) == 0)
    def _():
        zero_ref[...] = jnp.zeros_like(zero_ref)

        def start(e, carry):
            @pl.when(zrow_ref[e] >= 0)
            def _():
                zero_copy(e).start()
            return carry

        def wait(e, carry):
            @pl.when(zrow_ref[e] >= 0)
            def _():
                zero_copy(e).wait()
            return carry

        lax.fori_loop(0, zrow_ref.shape[0], start, 0)
        lax.fori_loop(0, zrow_ref.shape[0], wait, 0)

    def start(r, carry):
        for k in range(TOP_K):
            _slab_copy(h_ref, r, xs_ref, dest_ref[0, k, r], sem).start(priority=k)
        return carry

    lax.fori_loop(0, rows, start, 0, unroll=ROW_DMA_UNROLL)
    for k in range(TOP_K):
        pltpu.make_async_copy(h_ref, xs_ref.at[pl.ds(0, rows * SLAB_ROWS), :], sem).wait()


def _dispatch(hs, dest, zrow, n_rows):
    n, _, tm = dest.shape
    grid_spec = pltpu.PrefetchScalarGridSpec(
        num_scalar_prefetch=1,
        grid=(n,),
        in_specs=[
            pl.BlockSpec((1, TOP_K, tm), lambda i, z: (i, 0, 0), memory_space=pltpu.SMEM),
            pl.BlockSpec((tm * SLAB_ROWS, LANES), lambda i, z: (i, 0)),
        ],
        out_specs=pl.BlockSpec(memory_space=pl.ANY),
        scratch_shapes=[pltpu.VMEM((MOE_ROWS * SLAB_ROWS, LANES), jnp.uint32),
                        pltpu.SemaphoreType.DMA, pltpu.SemaphoreType.DMA],
    )
    return pl.pallas_call(
        _dispatch_kernel,
        grid_spec=grid_spec,
        out_shape=jax.ShapeDtypeStruct((n_rows * SLAB_ROWS, LANES), jnp.uint32),
        compiler_params=pltpu.CompilerParams(dimension_semantics=("arbitrary",)),
        name="dispatch",
    )(zrow, dest, hs)


def _expert_kernel(be_ref, nx_ref, nu_ref, x_ref, w1_ref, w3_ref, w2_ref, y_ref,
                   w1s_ref, w3s_ref, w2s_ref, w1b_ref, w3b_ref, w2b_ref, wsem):
    i = pl.program_id(0)
    d = w1_ref.shape[1]
    rows = x_ref.shape[0] // SLAB_ROWS

    def fetch(e):
        return [pltpu.make_async_copy(src.at[e], dst, wsem.at[n])
                for n, (src, dst) in enumerate(((w1_ref, w1s_ref), (w3_ref, w3s_ref), (w2_ref, w2s_ref)))]

    @pl.when(i == 0)
    def _():
        for copy in fetch(be_ref[0]):
            copy.start()

    @pl.when((i == 0) | (be_ref[i] != be_ref[jnp.maximum(i - 1, 0)]))
    def _():
        for copy in fetch(be_ref[i]):
            copy.wait()
        for s in range(SLAB_ROWS):
            for half, src0 in enumerate((s * LANES, d // 2 + s * LANES)):
                dst0 = (2 * s + half) * LANES
                w1b_ref[dst0:dst0 + LANES, :] = w1s_ref[src0:src0 + LANES, :].astype(BF16)
                w3b_ref[dst0:dst0 + LANES, :] = w3s_ref[src0:src0 + LANES, :].astype(BF16)
        w2b_ref[...] = w2s_ref[...].astype(BF16)

        @pl.when(nx_ref[i] >= 0)
        def _():
            for copy in fetch(nx_ref[i]):
                copy.start()

    @pl.when(i < nu_ref[0])
    def _():
        sub = rows // EXPERT_SUBTILES
        for r0 in range(0, rows, sub):
            x = jnp.concatenate(
                [part.astype(BF16) for pair in _load_slabs(x_ref, sub, r0) for part in pair], axis=1)
            a = jnp.dot(x, w1b_ref[...], preferred_element_type=F32)
            b = jnp.dot(x, w3b_ref[...], preferred_element_type=F32)
            mid = (a * _sigmoid(a) * b).astype(BF16)
            _store_slabs(y_ref, r0, jnp.dot(mid, w2b_ref[...], preferred_element_type=F32))

    @pl.when(i >= nu_ref[0])
    def _():
        y_ref[...] = jnp.zeros_like(y_ref)


def _experts(xs, block_expert, next_expert, n_used, w1, w3, w2):
    _, d, de = w1.shape
    nb = xs.shape[0] // (MOE_ROWS * SLAB_ROWS)
    nbytes = MOE_ROWS * d * (2 + 2 + 2 + 4) + 3 * d * de * (2 + 1) + MOE_ROWS * de * 12
    grid_spec = pltpu.PrefetchScalarGridSpec(
        num_scalar_prefetch=3,
        grid=(nb,),
        in_specs=[
            pl.BlockSpec((MOE_ROWS * SLAB_ROWS, LANES), lambda i, be, nx, nu: (jnp.minimum(i, nu[0] - 1), 0)),
            pl.BlockSpec(memory_space=pl.ANY),
            pl.BlockSpec(memory_space=pl.ANY),
            pl.BlockSpec(memory_space=pl.ANY),
        ],
        out_specs=pl.BlockSpec((MOE_ROWS * SLAB_ROWS, LANES), lambda i, be, nx, nu: (i, 0)),
        scratch_shapes=[pltpu.VMEM((d, de), F32), pltpu.VMEM((d, de), F32), pltpu.VMEM((de, d), F32),
                        pltpu.VMEM((d, de), BF16), pltpu.VMEM((d, de), BF16), pltpu.VMEM((de, d), BF16),
                        pltpu.SemaphoreType.DMA((3,))],
    )
    return pl.pallas_call(
        _expert_kernel,
        grid_spec=grid_spec,
        out_shape=jax.ShapeDtypeStruct(xs.shape, jnp.uint32),
        compiler_params=pltpu.CompilerParams(
            dimension_semantics=("arbitrary",), vmem_limit_bytes=_vmem_limit(nbytes)),
        name="experts",
    )(block_expert, next_expert, n_used, xs, w1, w3, w2)


def _combine_kernel(dest_ref, dnext_ref, h_ref, r_ref, y_ref, g_ref, b_ref, o_ref, buf_ref, sem, *, n_steps):
    i = pl.program_id(0)
    rows = h_ref.shape[0]
    chunk = min(rows, COMBINE_CHUNK)

    def gather(dref, slot, r, k):
        return _slab_copy(y_ref, dref[0, k, r], buf_ref.at[slot, k], r, sem.at[slot])

    def wait(slot):
        for k in range(TOP_K):
            pltpu.make_async_copy(y_ref.at[pl.ds(0, rows * SLAB_ROWS), :], buf_ref.at[slot, k],
                                  sem.at[slot]).wait()

    @pl.when(i == 0)
    def _():
        def start(r, carry):
            for k in range(TOP_K):
                gather(dest_ref, 0, r, k).start(priority=k)
            return carry

        lax.fori_loop(0, rows, start, 0, unroll=ROW_DMA_UNROLL)

    def step(slot):
        wait(slot)
        for c0 in range(0, rows, chunk):
            route = r_ref[c0:c0 + chunk, :]
            slabs = [_load_slabs(buf_ref.at[slot, k], chunk, c0) for k in range(TOP_K)]
            lo, hi = [], []
            for s in range(SLAB_ROWS):
                for half, out in enumerate((lo, hi)):
                    out.append(sum(route[:, 2 + k:3 + k] * slabs[k][s][half] for k in range(TOP_K)))
            moe = jnp.concatenate(lo + hi, axis=1)
            o_ref[c0:c0 + chunk, :] = _layer_norm(
                DEEPNORM_ALPHA * h_ref[c0:c0 + chunk, :] + moe, g_ref[...], b_ref[...])
            for r in range(c0, c0 + chunk):
                for k in range(TOP_K):
                    gather(dnext_ref, 1 - slot, r, k).start(priority=k)

    for parity in range(2):
        pl.when(lax.rem(i, 2) == parity)(functools.partial(step, parity))

    @pl.when(i == n_steps - 1)
    def _():
        wait(n_steps % 2)


def _combine(h, route, dest, y, ln_g, ln_b):
    t, d = h.shape
    n, _, tm = dest.shape
    return pl.pallas_call(
        functools.partial(_combine_kernel, n_steps=n),
        grid=(n,),
        in_specs=[
            pl.BlockSpec((1, TOP_K, tm), lambda i: (i, 0, 0), memory_space=pltpu.SMEM),
            pl.BlockSpec((1, TOP_K, tm), lambda i: (jnp.minimum(i + 1, n - 1), 0, 0), memory_space=pltpu.SMEM),
            pl.BlockSpec((tm, d), lambda i: (i, 0)),
            pl.BlockSpec((tm, LANES), lambda i: (i, 0)),
            pl.BlockSpec(memory_space=pl.ANY),
            pl.BlockSpec((1, d), lambda i: (0, 0)),
            pl.BlockSpec((1, d), lambda i: (0, 0)),
        ],
        out_specs=pl.BlockSpec((tm, d), lambda i: (i, 0)),
        out_shape=jax.ShapeDtypeStruct((t, d), F32),
        scratch_shapes=[pltpu.VMEM((2, TOP_K, tm * SLAB_ROWS, LANES), jnp.uint32), pltpu.SemaphoreType.DMA((2,))],
        compiler_params=pltpu.CompilerParams(dimension_semantics=("arbitrary",)),
        name="combine",
    )(dest, dest, h, route, y, ln_g, ln_b)


def kernel(x, w_in, w_gate_a2, b_gate_a, sgu_ln_g, sgu_ln_b, sgu_w_s, sgu_b_s, gla_norm_g, w_branch_a, w_branch_b, w_merge, b_merge, w_out, ln1_g, ln1_b, w_router_group, b_router_group, w_router_expert, b_router_expert, w_exp_gate, w_exp_up, w_exp_down, ln2_g, ln2_b):
    batch, seq, d = x.shape
    t = batch * seq
    assert w_in.shape[0] == 1, "one layer"
    assert seq % SGU_CHUNK == 0 and seq % GLA_CHUNK == 0 and t % MOE_ROWS == 0
    sgu_width = sgu_ln_g.shape[1]
    key_dim = w_gate_a2.shape[2]
    val_dim = gla_norm_g.shape[1]
    rank = w_gate_a2.shape[1]
    xf = x.reshape(t, d)
    n_uv, n_h2 = 2 * sgu_width, 2 * key_dim + 2 * val_dim

    w_a = jnp.pad(w_in[0, :, n_uv + n_h2:].astype(BF16), ((0, 0), (0, LANES - rank)))
    p, a_lr = _proj(xf, w_in[0].astype(BF16), w_merge[0].astype(BF16), b_merge, w_a,
                    n_uv, n_h2 - val_dim, val_dim, 1024, 1024)

    s = _sgu(p, sgu_width, sgu_w_s[0], sgu_ln_g[0], sgu_ln_b[0], sgu_b_s[0], 512)
    w_gate = jnp.pad(w_gate_a2[0].astype(BF16), ((0, LANES - rank), (0, 0)))
    o = _gla(p, n_uv, a_lr, w_gate, b_gate_a, gla_norm_g, batch, 512)
    merged = _merge(s, o, w_branch_a[0].astype(BF16), w_branch_b[0].astype(BF16), p, n_uv + n_h2, 1024, 1024)

    w_r = jnp.concatenate([w_router_group[0], w_router_expert[0]], axis=1)
    n_r = w_r.shape[1]
    w_r = jnp.pad(w_r, ((0, 0), (0, LANES - n_r))).astype(BF16)
    b_r = jnp.pad(jnp.concatenate([b_router_group, b_router_expert], axis=1), ((0, 0), (0, LANES - n_r)))
    h1, h1_slabs, route, counts = _out(merged, w_out[0].astype(BF16), xf, ln1_g, ln1_b, w_r, b_r, 512)

    assert t * TOP_K // MOE_ROWS <= 256, "per-expert block counts must stay exact in bf16"
    dest = _rank(route, counts, MOE_TOKEN_TILE)
    blocks_per_expert = (counts[0, :N_EXPERTS].astype(jnp.int32) + MOE_ROWS - 1) // MOE_ROWS
    block_ends = jnp.cumsum(blocks_per_expert)
    n_blocks = t * TOP_K // MOE_ROWS + N_EXPERTS
    n_used = block_ends[-1:]
    block_ids = jnp.minimum(jnp.arange(n_blocks, dtype=jnp.int32), n_used[0] - 1)
    experts = jnp.arange(N_EXPERTS, dtype=jnp.int32)
    block_expert = jnp.minimum(
        jnp.sum(block_ends[None, :] <= block_ids[:, None], axis=1), N_EXPERTS - 1).astype(jnp.int32)
    later = (experts[None, :] > block_expert[:, None]) & (blocks_per_expert[None, :] > 0)
    next_expert = jnp.min(jnp.where(later, experts[None, :], N_EXPERTS), axis=1)
    next_expert = jnp.where(next_expert < N_EXPERTS, next_expert, -1).astype(jnp.int32)
    tail_ids = n_used[0] + experts
    zrow = jnp.concatenate([
        jnp.where(blocks_per_expert > 0, (block_ends - 1) * MOE_ROWS, -1),
        jnp.where(tail_ids < n_blocks, tail_ids * MOE_ROWS, -1)]).astype(jnp.int32)

    xs = _dispatch(h1_slabs, dest, zrow, n_blocks * MOE_ROWS)
    y = _experts(xs, block_expert, next_expert, n_used.astype(jnp.int32),
                 w_exp_gate[0], w_exp_up[0], w_exp_down[0])
    out = _combine(h1, route, dest, y, ln2_g, ln2_b)
    return out.reshape(batch, seq, d)
```

```python
import functools

import jax
import jax.numpy as jnp
from jax import lax
from jax.experimental import pallas as pl
from jax.experimental.pallas import tpu as pltpu

F32 = jnp.float32
BF16 = jnp.bfloat16

SGU_CHUNK = 128
SGU_GROUPS = 8
GLA_HEADS = 4
GLA_CHUNK = 64
GLA_GATE_NORM = 16.0
GLA_HEADS_PER_STEP = 2
GLA_GROUP_CHUNKS = 4
N_GROUPS = 8
EXPERTS_PER_GROUP = 8
N_EXPERTS = N_GROUPS * EXPERTS_PER_GROUP
TOP_K = 2
LN_EPS = 1e-5
DEEPNORM_ALPHA = 2.0 ** 0.25

LANES = 128
MOE_ROWS = 256
MOE_TOKEN_TILE = 512
ROW_DMA_UNROLL = 8
PROJ_SUBTILES = 2
OUT_SUBTILES = 2
EXPERT_SUBTILES = 1
COMBINE_CHUNK = 32
V7X_VMEM_BYTES = 64 * 2 ** 20


def _vmem_limit(nbytes):
    return int(min(max(2 * nbytes, 16 * 2 ** 20), V7X_VMEM_BYTES - 8 * 2 ** 20))


def _layer_norm(y, g, b):
    mu = jnp.mean(y, axis=-1, keepdims=True)
    var = jnp.mean(jnp.square(y - mu), axis=-1, keepdims=True)
    return (y - mu) * lax.rsqrt(var + LN_EPS) * g + b


def _gelu(x):
    return 0.5 * x * (1.0 + lax.erf(x * (2.0 ** -0.5)))


def _sigmoid(x):
    return 0.5 * (jnp.tanh(0.5 * x) + 1.0)


SLAB_ROWS = 8
HIGH_HALF = 0xFFFF0000


def _bf16_bits(x):
    return lax.bitcast_convert_type(x.astype(BF16).astype(F32), jnp.uint32)


def _store_slabs(ref, row0, x):
    rows, d = x.shape
    assert d == 2 * SLAB_ROWS * LANES
    for s in range(SLAB_ROWS):
        lo = _bf16_bits(x[:, s * LANES:(s + 1) * LANES])
        hi = _bf16_bits(x[:, d // 2 + s * LANES:d // 2 + (s + 1) * LANES])
        word = jnp.right_shift(lo, jnp.uint32(16)) | (hi & jnp.uint32(HIGH_HALF))
        ref[pl.ds(row0 * SLAB_ROWS + s, rows, stride=SLAB_ROWS), :] = word


def _load_slabs(ref, rows, row0=0):
    out = []
    for s in range(SLAB_ROWS):
        word = ref[pl.ds(row0 * SLAB_ROWS + s, rows, stride=SLAB_ROWS), :]
        lo = lax.bitcast_convert_type(jnp.left_shift(word, jnp.uint32(16)), F32)
        hi = lax.bitcast_convert_type(word & jnp.uint32(HIGH_HALF), F32)
        out.append((lo, hi))
    return out


def _proj_kernel(x_ref, w_ref, wm_ref, bm_ref, wa_ref, o_ref, a_ref, xb_ref, *,
                 gelu_blocks, plain_blocks, silu_blocks):
    j = pl.program_id(1)
    silu_start = gelu_blocks + plain_blocks
    sigmoid_start = silu_start + silu_blocks

    @pl.when(j == 0)
    def _():
        xb_ref[...] = x_ref[...].astype(BF16)
        a_ref[...] = jnp.dot(xb_ref[...], wa_ref[...], preferred_element_type=F32).astype(a_ref.dtype)

    def block(weights, act):
        sub = o_ref.shape[1] // PROJ_SUBTILES
        for cols in (slice(si * sub, (si + 1) * sub) for si in range(PROJ_SUBTILES)):
            acc = jnp.dot(xb_ref[...], weights[:, cols], preferred_element_type=F32)
            o_ref[:, cols] = act(acc, cols).astype(o_ref.dtype)

    @pl.when(j < gelu_blocks)
    def _():
        block(w_ref, lambda acc, cols: _gelu(acc))

    @pl.when((j >= gelu_blocks) & (j < silu_start))
    def _():
        block(w_ref, lambda acc, cols: acc)

    @pl.when((j >= silu_start) & (j < sigmoid_start))
    def _():
        block(w_ref, lambda acc, cols: acc * _sigmoid(acc))

    @pl.when(j >= sigmoid_start)
    def _():
        block(wm_ref, lambda acc, cols: _sigmoid(acc + bm_ref[:, cols]))


def _proj(x, w, w_m, b_m, w_a, n_gelu, n_plain, n_silu, tm, tn):
    m, k = x.shape
    n_in, n_m = n_gelu + n_plain + n_silu, w_m.shape[1]
    tm, tn = min(tm, m), min(tn, n_m)
    assert n_gelu % tn == 0 and n_plain % tn == 0 and n_silu % tn == 0 and n_m % tn == 0
    in_blocks = n_in // tn
    nbytes = tm * k * (4 + 1) + 2 * k * tn * 2 + tm * tn * (2 + 2) + k * LANES * 2
    return pl.pallas_call(
        functools.partial(_proj_kernel, gelu_blocks=n_gelu // tn, plain_blocks=n_plain // tn,
                          silu_blocks=n_silu // tn),
        grid=(m // tm, (n_in + n_m) // tn),
        in_specs=[
            pl.BlockSpec((tm, k), lambda i, j: (i, 0)),
            pl.BlockSpec((k, tn), lambda i, j: (0, jnp.minimum(j, in_blocks - 1))),
            pl.BlockSpec((k, tn), lambda i, j: (0, jnp.maximum(j - in_blocks, 0))),
            pl.BlockSpec((1, tn), lambda i, j: (0, jnp.maximum(j - in_blocks, 0))),
            pl.BlockSpec((k, LANES), lambda i, j: (0, 0)),
        ],
        out_specs=[pl.BlockSpec((tm, tn), lambda i, j: (i, j)), pl.BlockSpec((tm, LANES), lambda i, j: (i, 0))],
        out_shape=[jax.ShapeDtypeStruct((m, n_in + n_m), BF16), jax.ShapeDtypeStruct((m, LANES), BF16)],
        scratch_shapes=[pltpu.VMEM((tm, k), BF16)],
        compiler_params=pltpu.CompilerParams(
            dimension_semantics=("parallel", "arbitrary"), vmem_limit_bytes=_vmem_limit(nbytes)),
        name="proj",
    )(x, w, w_m, b_m, w_a)


def _sgu_kernel(u_ref, v_ref, ws_ref, g_ref, b_ref, bs_ref, o_ref):
    c = SGU_CHUNK
    row = lax.broadcasted_iota(jnp.int32, (c, c), 0)
    col = lax.broadcasted_iota(jnp.int32, (c, c), 1)
    causal = row >= col
    for g in range(ws_ref.shape[0]):
        cols = slice(g * c, (g + 1) * c)
        w = jnp.where(causal, ws_ref[g], 0.0).astype(BF16)
        ln_g, ln_b, bias = g_ref[:, cols], b_ref[:, cols], bs_ref[g]
        for ci in range(u_ref.shape[0] // c):
            rows = slice(ci * c, (ci + 1) * c)
            vn = _layer_norm(v_ref[rows, cols].astype(F32), ln_g, ln_b)
            mixed = jnp.dot(w, vn.astype(BF16), preferred_element_type=F32) + bias
            o_ref[rows, cols] = (u_ref[rows, cols].astype(F32) * mixed).astype(o_ref.dtype)


def _sgu(uv, width, w_s, ln_g, ln_b, b_s, tm):
    t = uv.shape[0]
    ng, c, _ = w_s.shape
    tm = min(tm, t)
    return pl.pallas_call(
        _sgu_kernel,
        grid=(t // tm,),
        in_specs=[
            pl.BlockSpec((tm, width), lambda i: (i, 0)),
            pl.BlockSpec((tm, width), lambda i: (i, 1)),
            pl.BlockSpec((ng, c, c), lambda i: (0, 0, 0)),
            pl.BlockSpec((1, width), lambda i: (0, 0)),
            pl.BlockSpec((1, width), lambda i: (0, 0)),
            pl.BlockSpec((ng, c, 1), lambda i: (0, 0, 0)),
        ],
        out_specs=pl.BlockSpec((tm, width), lambda i: (i, 0)),
        out_shape=jax.ShapeDtypeStruct((t, width), BF16),
        compiler_params=pltpu.CompilerParams(dimension_semantics=("parallel",)),
        name="sgu",
    )(uv, uv, w_s, ln_g.reshape(1, width), ln_b.reshape(1, width), b_s.reshape(ng, c, 1))


def _gla_kernel(q_ref, k_ref, v_ref, g_ref, a_ref, wg_ref, bg_ref, ng_ref, o_ref, st_ref):
    c = GLA_CHUNK
    ts = q_ref.shape[0]
    heads, dv, dk = st_ref.shape
    grp = min(ts, GLA_GROUP_CHUNKS * c)
    ncg = grp // c
    contract_last = (((1,), (1,)), ((), ()))
    contract_first = (((0,), (0,)), ((), ()))

    @pl.when(pl.program_id(2) == 0)
    def _():
        st_ref[...] = jnp.zeros_like(st_ref)

    row = lax.broadcasted_iota(jnp.int32, (grp, grp), 0)
    col = lax.broadcasted_iota(jnp.int32, (grp, grp), 1)
    shift = c.bit_length() - 1
    ones_tril = ((row >= col) & (jnp.right_shift(row, shift) == jnp.right_shift(col, shift))).astype(BF16)
    qrow = lax.broadcasted_iota(jnp.int32, (c, grp), 0)
    kcol = lax.broadcasted_iota(jnp.int32, (c, grp), 1)
    visible = [(kcol < j * c) | ((kcol < (j + 1) * c) & (kcol - j * c <= qrow)) for j in range(ncg)]
    zero_keys = jnp.zeros((c, dk), BF16)

    for hd in range(heads):
        kcols = slice(hd * dk, (hd + 1) * dk)
        vcols = slice(hd * dv, (hd + 1) * dv)
        z = jnp.dot(a_ref[...], wg_ref[:, kcols], preferred_element_type=F32) + bg_ref[:, kcols]
        log_a = (jnp.minimum(z, 0.0) - jnp.log1p(jnp.exp(-jnp.abs(z)))) * (1.0 / GLA_GATE_NORM)
        la_hi = log_a.astype(BF16)
        la_lo = (log_a - la_hi.astype(F32)).astype(BF16)
        la_split = jnp.concatenate([la_hi, la_lo], axis=1)
        state_t = st_ref[hd]
        for gi in range(ts // grp):
            rows = slice(gi * grp, (gi + 1) * grp)
            r = jnp.dot(ones_tril, la_split[rows, :], preferred_element_type=F32)
            b = r[:, :dk] + r[:, dk:]
            q = q_ref[rows, kcols].astype(F32)
            k = k_ref[rows, kcols].astype(F32)
            v = v_ref[rows, vcols]
            q_dec = q * jnp.exp(b)
            k_inv = (k * jnp.exp(-b)).astype(BF16)
            chunk = [slice(j * c, (j + 1) * c) for j in range(ncg)]
            b_last = [b[(j + 1) * c - 1:(j + 1) * c, :] for j in range(ncg)]
            before = [jnp.zeros_like(b_last[0])]
            for j in range(ncg):
                before.append(before[j] + b_last[j])
            k_to_end = [k[chunk[j], :] * jnp.exp(b_last[j] - b[chunk[j], :]) for j in range(ncg)]

            q_grp = jnp.concatenate(
                [q_dec[chunk[j], :] * jnp.exp(before[j]) for j in range(ncg)], axis=0).astype(BF16)
            o = lax.dot_general(q_grp, state_t.astype(BF16), contract_last, preferred_element_type=F32)

            q_dec = q_dec.astype(BF16)
            attn = []
            for j in range(ncg):
                keys = [(k_to_end[i] * jnp.exp(before[j] - before[i + 1])).astype(BF16) for i in range(j)]
                keys += [k_inv[chunk[j], :]] + [zero_keys] * (ncg - 1 - j)
                keys = jnp.concatenate(keys, axis=0) if ncg > 1 else keys[0]
                scores = lax.dot_general(q_dec[chunk[j], :], keys, contract_last, preferred_element_type=F32)
                attn.append(jnp.where(visible[j], scores, 0.0).astype(BF16))
            attn = jnp.concatenate(attn, axis=0) if ncg > 1 else attn[0]
            o = o + jnp.dot(attn, v, preferred_element_type=F32)

            k_grp = jnp.concatenate(
                [(k_to_end[j] * jnp.exp(before[ncg] - before[j + 1])).astype(BF16) for j in range(ncg)], axis=0)
            state_t = state_t * jnp.exp(before[ncg]) + lax.dot_general(
                v, k_grp, contract_first, preferred_element_type=F32)

            o = o * lax.rsqrt(jnp.mean(jnp.square(o), axis=-1, keepdims=True) + LN_EPS) * ng_ref[:, vcols]
            o_ref[rows, vcols] = (o * g_ref[rows, vcols].astype(F32)).astype(o_ref.dtype)
        st_ref[hd] = state_t


def _gla(h2, col0, a_lr, w_gate, b_gate, norm_g, batch, ts):
    t = h2.shape[0]
    seq = t // batch
    nh = GLA_HEADS
    key_dim = w_gate.shape[1]
    dk = key_dim // nh
    val_dim = norm_g.shape[1]
    dv = val_dim // nh
    ts = min(ts, seq)
    ns = seq // ts
    hps = GLA_HEADS_PER_STEP
    wk, wv = hps * dk, hps * dv
    assert nh % hps == 0 and col0 % wk == 0 and (col0 + 2 * key_dim) % wv == 0
    qb = col0 // wk
    kb, vb, gb = qb + key_dim // wk, (col0 + 2 * key_dim) // wv, (col0 + 2 * key_dim + val_dim) // wv
    tok = lambda b, h, s: b * ns + s
    return pl.pallas_call(
        _gla_kernel,
        grid=(batch, nh // hps, ns),
        in_specs=[
            pl.BlockSpec((ts, wk), lambda b, h, s: (tok(b, h, s), qb + h)),
            pl.BlockSpec((ts, wk), lambda b, h, s: (tok(b, h, s), kb + h)),
            pl.BlockSpec((ts, wv), lambda b, h, s: (tok(b, h, s), vb + h)),
            pl.BlockSpec((ts, wv), lambda b, h, s: (tok(b, h, s), gb + h)),
            pl.BlockSpec((ts, LANES), lambda b, h, s: (tok(b, h, s), 0)),
            pl.BlockSpec((LANES, wk), lambda b, h, s: (0, h)),
            pl.BlockSpec((1, wk), lambda b, h, s: (0, h)),
            pl.BlockSpec((1, wv), lambda b, h, s: (0, h)),
        ],
        out_specs=pl.BlockSpec((ts, wv), lambda b, h, s: (tok(b, h, s), h)),
        out_shape=jax.ShapeDtypeStruct((t, val_dim), BF16),
        scratch_shapes=[pltpu.VMEM((hps, dv, dk), F32)],
        compiler_params=pltpu.CompilerParams(
            dimension_semantics=("parallel", "parallel", "arbitrary"),
            vmem_limit_bytes=_vmem_limit(hps * ts * (dk * 48 + dv * 16))),
        name="gla",
    )(h2, h2, h2, h2, a_lr, w_gate, b_gate, norm_g)


def _merge_kernel(s_ref, o_ref, wa_ref, wb_ref, ga_ref, gb_ref, out_ref):
    half = out_ref.shape[1] // 2
    for cols in (slice(0, half), slice(half, 2 * half)):
        ya = jnp.dot(s_ref[...], wa_ref[:, cols], preferred_element_type=F32)
        yb = jnp.dot(o_ref[...], wb_ref[:, cols], preferred_element_type=F32)
        out_ref[:, cols] = (ga_ref[:, cols].astype(F32) * ya
                            + gb_ref[:, cols].astype(F32) * yb).astype(out_ref.dtype)


def _merge(s, o, w_a, w_b, gates, gate_col0, tm, tn):
    t, ka = s.shape
    kb = o.shape[1]
    d = w_a.shape[1]
    tm, tn = min(tm, t), min(tn, d)
    gj = gate_col0 // tn
    nbytes = tm * (ka + kb) * 2 + (ka + kb) * tn * 2 + tm * tn * (2 * 2 + 2 + 8)
    return pl.pallas_call(
        _merge_kernel,
        grid=(t // tm, d // tn),
        in_specs=[
            pl.BlockSpec((tm, ka), lambda i, j: (i, 0)),
            pl.BlockSpec((tm, kb), lambda i, j: (i, 0)),
            pl.BlockSpec((ka, tn), lambda i, j: (0, j)),
            pl.BlockSpec((kb, tn), lambda i, j: (0, j)),
            pl.BlockSpec((tm, tn), lambda i, j: (i, gj + j)),
            pl.BlockSpec((tm, tn), lambda i, j: (i, gj + j + d // tn)),
        ],
        out_specs=pl.BlockSpec((tm, tn), lambda i, j: (i, j)),
        out_shape=jax.ShapeDtypeStruct((t, d), BF16),
        compiler_params=pltpu.CompilerParams(
            dimension_semantics=("parallel", "arbitrary"), vmem_limit_bytes=_vmem_limit(nbytes)),
        name="merge",
    )(s, o, w_a, w_b, gates, gates)


def _route(logits):
    lane = lax.broadcasted_iota(jnp.int32, logits.shape, 1).astype(F32)
    neg = float("-inf")
    big = float(LANES)
    gl = jnp.where(lane < N_GROUPS, logits, neg)
    gmax = jnp.max(gl, axis=1, keepdims=True)
    gidx = jnp.min(jnp.where(gl == gmax, lane, big), axis=1, keepdims=True)
    p_group = 1.0 / jnp.sum(jnp.exp(gl - gmax), axis=1, keepdims=True)
    lo = N_GROUPS + EXPERTS_PER_GROUP * gidx
    el = jnp.where((lane >= lo) & (lane < lo + EXPERTS_PER_GROUP), logits, neg)
    v1 = jnp.max(el, axis=1, keepdims=True)
    i1 = jnp.min(jnp.where(el == v1, lane, big), axis=1, keepdims=True)
    el2 = jnp.where(lane == i1, neg, el)
    v2 = jnp.max(el2, axis=1, keepdims=True)
    i2 = jnp.min(jnp.where(el2 == v2, lane, big), axis=1, keepdims=True)
    t = jnp.exp(v2 - v1)
    w1 = p_group / (1.0 + t)
    w2 = p_group * t / (1.0 + t)
    return jnp.where(lane == 0, i1 - N_GROUPS,
                     jnp.where(lane == 1, i2 - N_GROUPS,
                               jnp.where(lane == 2, w1, jnp.where(lane == 3, w2, 0.0))))


def _expert_hits(route):
    lane = lax.broadcasted_iota(jnp.int32, route.shape, 1).astype(F32)
    return [lane == route[:, k:k + 1] for k in range(TOP_K)]


def _out_kernel(m_ref, w_ref, x_ref, g_ref, b_ref, wr_ref, br_ref, h_ref, hs_ref, r_ref, cnt_ref, mix_ref):
    i = pl.program_id(0)

    @pl.when(i == 0)
    def _():
        cnt_ref[...] = jnp.zeros_like(cnt_ref)
        mix_ref[1] = jnp.zeros(mix_ref.shape[1:], F32)

    has_prev = (i > 0).astype(F32)
    sub = m_ref.shape[0] // OUT_SUBTILES

    def step(cur):
        for si in range(OUT_SUBTILES):
            rows = slice(si * sub, (si + 1) * sub)
            mix_ref[cur, rows, :] = jnp.dot(m_ref[rows, :], w_ref[...], preferred_element_type=F32)
            h = _layer_norm(DEEPNORM_ALPHA * x_ref[rows, :] + mix_ref[1 - cur, rows, :], g_ref[...], b_ref[...])
            h_ref[rows, :] = h
            _store_slabs(hs_ref, si * sub, h)
            logits = jnp.dot(h.astype(BF16), wr_ref[...], preferred_element_type=F32) + br_ref[...]
            route = _route(logits)
            r_ref[rows, :] = route
            cnt_ref[...] += has_prev * sum(
                jnp.sum(hit.astype(F32), axis=0, keepdims=True) for hit in _expert_hits(route))

    for parity in range(2):
        pl.when(lax.rem(i, 2) == parity)(functools.partial(step, parity))


def _out(merged, w_out, x, ln_g, ln_b, w_r, b_r, tm):
    t, d = x.shape
    tm = min(tm, t)
    n = t // tm
    nbytes = d * d * 2 + tm * d * (2 + 4 + 4 + 2 + 8 + 8) + d * LANES * 2
    cur = lambda i: (jnp.minimum(i, n - 1), 0)
    prev = lambda i: (jnp.maximum(i - 1, 0), 0)
    return pl.pallas_call(
        _out_kernel,
        grid=(n + 1,),
        in_specs=[
            pl.BlockSpec((tm, d), cur),
            pl.BlockSpec((d, d), lambda i: (0, 0)),
            pl.BlockSpec((tm, d), prev),
            pl.BlockSpec((1, d), lambda i: (0, 0)),
            pl.BlockSpec((1, d), lambda i: (0, 0)),
            pl.BlockSpec((d, LANES), lambda i: (0, 0)),
            pl.BlockSpec((1, LANES), lambda i: (0, 0)),
        ],
        out_specs=[pl.BlockSpec((tm, d), prev),
                   pl.BlockSpec((tm * SLAB_ROWS, LANES), prev),
                   pl.BlockSpec((tm, LANES), prev),
                   pl.BlockSpec((1, LANES), lambda i: (0, 0))],
        out_shape=[jax.ShapeDtypeStruct((t, d), F32),
                   jax.ShapeDtypeStruct((t * SLAB_ROWS, LANES), jnp.uint32),
                   jax.ShapeDtypeStruct((t, LANES), F32),
                   jax.ShapeDtypeStruct((1, LANES), F32)],
        scratch_shapes=[pltpu.VMEM((2, tm, d), F32)],
        compiler_params=pltpu.CompilerParams(
            dimension_semantics=("arbitrary",), vmem_limit_bytes=_vmem_limit(nbytes)),
        name="out_ln_route",
    )(merged, w_out, x, ln_g, ln_b, w_r, b_r)


def _rank_kernel(r_ref, cnt_ref, dest_ref, next_ref):
    rows = r_ref.shape[0]

    @pl.when(pl.program_id(0) == 0)
    def _():
        blocks = jnp.floor((cnt_ref[...] + (MOE_ROWS - 1)) * (1.0 / MOE_ROWS))
        k = lax.broadcasted_iota(jnp.int32, (LANES, LANES), 0)
        e = lax.broadcasted_iota(jnp.int32, (LANES, LANES), 1)
        blocks8 = jnp.broadcast_to(blocks, (8, LANES)).astype(BF16)
        first_block = jnp.dot(blocks8, (k < e).astype(BF16), preferred_element_type=F32)
        next_ref[...] = first_block[0:1, :] * MOE_ROWS

    hits = _expert_hits(r_ref[...])
    cnt = sum(hit.astype(F32) for hit in hits)
    row = lax.broadcasted_iota(jnp.int32, (rows, rows), 0)
    col = lax.broadcasted_iota(jnp.int32, (rows, rows), 1)
    earlier = (row > col).astype(BF16)
    slot = jnp.dot(earlier, cnt.astype(BF16), preferred_element_type=F32) + next_ref[...]
    dest = [jnp.sum(jnp.where(hit, slot, 0.0), axis=1, keepdims=True) for hit in hits]
    lane = lax.broadcasted_iota(jnp.int32, (rows, LANES), 1)
    by_token = jnp.where(lane == 0, dest[0], jnp.where(lane == 1, dest[1], 0.0))
    dest_ref[0] = jnp.transpose(by_token)[:TOP_K, :].astype(jnp.int32)
    next_ref[...] += jnp.sum(cnt, axis=0, keepdims=True)


def _rank(route, counts, tm):
    t = route.shape[0]
    tm = min(tm, t)
    return pl.pallas_call(
        _rank_kernel,
        grid=(t // tm,),
        in_specs=[pl.BlockSpec((tm, LANES), lambda i: (i, 0)), pl.BlockSpec((1, LANES), lambda i: (0, 0))],
        out_specs=pl.BlockSpec((1, TOP_K, tm), lambda i: (i, 0, 0)),
        out_shape=jax.ShapeDtypeStruct((t // tm, TOP_K, tm), jnp.int32),
        scratch_shapes=[pltpu.VMEM((1, LANES), F32)],
        compiler_params=pltpu.CompilerParams(dimension_semantics=("arbitrary",)),
        name="rank",
    )(route, counts)


def _slab_copy(src, src_row, dst, dst_row, sem):
    s0 = pl.multiple_of(src_row * SLAB_ROWS, SLAB_ROWS)
    d0 = pl.multiple_of(dst_row * SLAB_ROWS, SLAB_ROWS)
    return pltpu.make_async_copy(src.at[pl.ds(s0, SLAB_ROWS), :], dst.at[pl.ds(d0, SLAB_ROWS), :], sem)


DISPATCH_SLOTS = 3


def _dispatch_kernel(zrow_ref, dest_ref, hs_ref, xs_ref, buf_ref, zero_ref, lsem, ssem, zsem, *, n_steps):
    i = pl.program_id(0)
    rows = buf_ref.shape[1] // SLAB_ROWS
    zrows = zero_ref.shape[0]

    def zero_copy(e):
        start_row = pl.multiple_of(jnp.maximum(zrow_ref[e], 0) * SLAB_ROWS, zrows)
        return pltpu.make_async_copy(zero_ref, xs_ref.at[pl.ds(start_row, zrows), :], zsem)

    def load(tile, slot):
        src = hs_ref.at[pl.ds(pl.multiple_of(tile * rows * SLAB_ROWS, SLAB_ROWS), rows * SLAB_ROWS), :]
        return pltpu.make_async_copy(src, buf_ref.at[slot], lsem.at[slot])

    def drain(slot):
        for k in range(TOP_K):
            pltpu.make_async_copy(buf_ref.at[slot], xs_ref.at[pl.ds(0, rows * SLAB_ROWS), :], ssem.at[slot]).wait()

    @pl.when(i == 0)
    def _():
        zero_ref[...] = jnp.zeros_like(zero_ref)

        def start(e, carry):
            @pl.when(zrow_ref[e] >= 0)
            def _():
                zero_copy(e).start()
            return carry

        def wait(e, carry):
            @pl.when(zrow_ref[e] >= 0)
            def _():
                zero_copy(e).wait()
            return carry

        lax.fori_loop(0, zrow_ref.shape[0], start, 0)
        for ahead in range(min(DISPATCH_SLOTS - 1, n_steps)):
            load(ahead, ahead).start()
        lax.fori_loop(0, zrow_ref.shape[0], wait, 0)

    def step(slot):
        load(i, slot).wait()

        def start(r, carry):
            for k in range(TOP_K):
                _slab_copy(buf_ref.at[slot], r, xs_ref, dest_ref[0, k, r], ssem.at[slot]).start(priority=k)
            return carry

        lax.fori_loop(0, rows, start, 0, unroll=ROW_DMA_UNROLL)
        free = (slot + DISPATCH_SLOTS - 1) % DISPATCH_SLOTS

        @pl.when(i > 0)
        def _():
            drain(free)

        @pl.when(i + DISPATCH_SLOTS - 1 < n_steps)
        def _():
            load(i + DISPATCH_SLOTS - 1, free).start()

        @pl.when(i == n_steps - 1)
        def _():
            drain(slot)

    for slot in range(DISPATCH_SLOTS):
        pl.when(lax.rem(i, DISPATCH_SLOTS) == slot)(functools.partial(step, slot))


def _dispatch(hs, dest, zrow, n_rows):
    n, _, tm = dest.shape
    grid_spec = pltpu.PrefetchScalarGridSpec(
        num_scalar_prefetch=1,
        grid=(n,),
        in_specs=[
            pl.BlockSpec((1, TOP_K, tm), lambda i, z: (i, 0, 0), memory_space=pltpu.SMEM),
            pl.BlockSpec(memory_space=pl.ANY),
        ],
        out_specs=pl.BlockSpec(memory_space=pl.ANY),
        scratch_shapes=[pltpu.VMEM((DISPATCH_SLOTS, tm * SLAB_ROWS, LANES), jnp.uint32),
                        pltpu.VMEM((MOE_ROWS * SLAB_ROWS, LANES), jnp.uint32),
                        pltpu.SemaphoreType.DMA((DISPATCH_SLOTS,)), pltpu.SemaphoreType.DMA((DISPATCH_SLOTS,)),
                        pltpu.SemaphoreType.DMA],
    )
    return pl.pallas_call(
        functools.partial(_dispatch_kernel, n_steps=n),
        grid_spec=grid_spec,
        out_shape=jax.ShapeDtypeStruct((n_rows * SLAB_ROWS, LANES), jnp.uint32),
        compiler_params=pltpu.CompilerParams(dimension_semantics=("arbitrary",)),
        name="dispatch",
    )(zrow, dest, hs)


def _expert_kernel(be_ref, nx_ref, nu_ref, x_ref, w1_ref, w3_ref, w2_ref, y_ref,
                   w1s_ref, w3s_ref, w2s_ref, w1b_ref, w3b_ref, w2b_ref, wsem):
    i = pl.program_id(0)
    d = w1_ref.shape[1]
    rows = x_ref.shape[0] // SLAB_ROWS

    def fetch(e):
        return [pltpu.make_async_copy(src.at[e], dst, wsem.at[n])
                for n, (src, dst) in enumerate(((w1_ref, w1s_ref), (w3_ref, w3s_ref), (w2_ref, w2s_ref)))]

    @pl.when(i == 0)
    def _():
        for copy in fetch(be_ref[0]):
            copy.start()

    @pl.when((i == 0) | (be_ref[i] != be_ref[jnp.maximum(i - 1, 0)]))
    def _():
        for copy in fetch(be_ref[i]):
            copy.wait()
        for s in range(SLAB_ROWS):
            for half, src0 in enumerate((s * LANES, d // 2 + s * LANES)):
                dst0 = (2 * s + half) * LANES
                w1b_ref[dst0:dst0 + LANES, :] = w1s_ref[src0:src0 + LANES, :].astype(BF16)
                w3b_ref[dst0:dst0 + LANES, :] = w3s_ref[src0:src0 + LANES, :].astype(BF16)
        w2b_ref[...] = w2s_ref[...].astype(BF16)

        @pl.when(nx_ref[i] >= 0)
        def _():
            for copy in fetch(nx_ref[i]):
                copy.start()

    @pl.when(i < nu_ref[0])
    def _():
        sub = rows // EXPERT_SUBTILES
        for r0 in range(0, rows, sub):
            x = jnp.concatenate(
                [part.astype(BF16) for pair in _load_slabs(x_ref, sub, r0) for part in pair], axis=1)
            a = jnp.dot(x, w1b_ref[...], preferred_element_type=F32)
            b = jnp.dot(x, w3b_ref[...], preferred_element_type=F32)
            mid = (a * _sigmoid(a) * b).astype(BF16)
            _store_slabs(y_ref, r0, jnp.dot(mid, w2b_ref[...], preferred_element_type=F32))

    @pl.when(i >= nu_ref[0])
    def _():
        y_ref[...] = jnp.zeros_like(y_ref)


def _experts(xs, block_expert, next_expert, n_used, w1, w3, w2):
    _, d, de = w1.shape
    nb = xs.shape[0] // (MOE_ROWS * SLAB_ROWS)
    nbytes = MOE_ROWS * d * (2 + 2 + 2 + 4) + 3 * d * de * (2 + 1) + MOE_ROWS * de * 12
    grid_spec = pltpu.PrefetchScalarGridSpec(
        num_scalar_prefetch=3,
        grid=(nb,),
        in_specs=[
            pl.BlockSpec((MOE_ROWS * SLAB_ROWS, LANES), lambda i, be, nx, nu: (jnp.minimum(i, nu[0] - 1), 0)),
            pl.BlockSpec(memory_space=pl.ANY),
            pl.BlockSpec(memory_space=pl.ANY),
            pl.BlockSpec(memory_space=pl.ANY),
        ],
        out_specs=pl.BlockSpec((MOE_ROWS * SLAB_ROWS, LANES), lambda i, be, nx, nu: (i, 0)),
        scratch_shapes=[pltpu.VMEM((d, de), F32), pltpu.VMEM((d, de), F32), pltpu.VMEM((de, d), F32),
                        pltpu.VMEM((d, de), BF16), pltpu.VMEM((d, de), BF16), pltpu.VMEM((de, d), BF16),
                        pltpu.SemaphoreType.DMA((3,))],
    )
    return pl.pallas_call(
        _expert_kernel,
        grid_spec=grid_spec,
        out_shape=jax.ShapeDtypeStruct(xs.shape, jnp.uint32),
        compiler_params=pltpu.CompilerParams(
            dimension_semantics=("arbitrary",), vmem_limit_bytes=_vmem_limit(nbytes)),
        name="experts",
    )(block_expert, next_expert, n_used, xs, w1, w3, w2)


def _combine_kernel(dest_ref, dnext_ref, h_ref, r_ref, y_ref, g_ref, b_ref, o_ref, buf_ref, sem, *, n_steps):
    i = pl.program_id(0)
    rows = h_ref.shape[0]
    chunk = min(rows, COMBINE_CHUNK)

    def gather(dref, slot, r, k):
        return _slab_copy(y_ref, dref[0, k, r], buf_ref.at[slot, k], r, sem.at[slot])

    def wait(slot):
        for k in range(TOP_K):
            pltpu.make_async_copy(y_ref.at[pl.ds(0, rows * SLAB_ROWS), :], buf_ref.at[slot, k],
                                  sem.at[slot]).wait()

    @pl.when(i == 0)
    def _():
        def start(r, carry):
            for k in range(TOP_K):
                gather(dest_ref, 0, r, k).start(priority=k)
            return carry

        lax.fori_loop(0, rows, start, 0, unroll=ROW_DMA_UNROLL)

    def step(slot):
        wait(slot)
        for c0 in range(0, rows, chunk):
            route = r_ref[c0:c0 + chunk, :]
            slabs = [_load_slabs(buf_ref.at[slot, k], chunk, c0) for k in range(TOP_K)]
            lo, hi = [], []
            for s in range(SLAB_ROWS):
                for half, out in enumerate((lo, hi)):
                    out.append(sum(route[:, 2 + k:3 + k] * slabs[k][s][half] for k in range(TOP_K)))
            moe = jnp.concatenate(lo + hi, axis=1)
            o_ref[c0:c0 + chunk, :] = _layer_norm(
                DEEPNORM_ALPHA * h_ref[c0:c0 + chunk, :] + moe, g_ref[...], b_ref[...])
            for r in range(c0, c0 + chunk):
                for k in range(TOP_K):
                    gather(dnext_ref, 1 - slot, r, k).start(priority=k)

    for parity in range(2):
        pl.when(lax.rem(i, 2) == parity)(functools.partial(step, parity))

    @pl.when(i == n_steps - 1)
    def _():
        wait(n_steps % 2)


def _combine(h, route, dest, y, ln_g, ln_b):
    t, d = h.shape
    n, _, tm = dest.shape
    return pl.pallas_call(
        functools.partial(_combine_kernel, n_steps=n),
        grid=(n,),
        in_specs=[
            pl.BlockSpec((1, TOP_K, tm), lambda i: (i, 0, 0), memory_space=pltpu.SMEM),
            pl.BlockSpec((1, TOP_K, tm), lambda i: (jnp.minimum(i + 1, n - 1), 0, 0), memory_space=pltpu.SMEM),
            pl.BlockSpec((tm, d), lambda i: (i, 0)),
            pl.BlockSpec((tm, LANES), lambda i: (i, 0)),
            pl.BlockSpec(memory_space=pl.ANY),
            pl.BlockSpec((1, d), lambda i: (0, 0)),
            pl.BlockSpec((1, d), lambda i: (0, 0)),
        ],
        out_specs=pl.BlockSpec((tm, d), lambda i: (i, 0)),
        out_shape=jax.ShapeDtypeStruct((t, d), F32),
        scratch_shapes=[pltpu.VMEM((2, TOP_K, tm * SLAB_ROWS, LANES), jnp.uint32), pltpu.SemaphoreType.DMA((2,))],
        compiler_params=pltpu.CompilerParams(dimension_semantics=("arbitrary",)),
        name="combine",
    )(dest, dest, h, route, y, ln_g, ln_b)


def kernel(x, w_in, w_gate_a2, b_gate_a, sgu_ln_g, sgu_ln_b, sgu_w_s, sgu_b_s, gla_norm_g, w_branch_a, w_branch_b, w_merge, b_merge, w_out, ln1_g, ln1_b, w_router_group, b_router_group, w_router_expert, b_router_expert, w_exp_gate, w_exp_up, w_exp_down, ln2_g, ln2_b):
    batch, seq, d = x.shape
    t = batch * seq
    assert w_in.shape[0] == 1, "one layer"
    assert seq % SGU_CHUNK == 0 and seq % GLA_CHUNK == 0 and t % MOE_ROWS == 0
    sgu_width = sgu_ln_g.shape[1]
    key_dim = w_gate_a2.shape[2]
    val_dim = gla_norm_g.shape[1]
    rank = w_gate_a2.shape[1]
    xf = x.reshape(t, d)
    n_uv, n_h2 = 2 * sgu_width, 2 * key_dim + 2 * val_dim

    w_a = jnp.pad(w_in[0, :, n_uv + n_h2:].astype(BF16), ((0, 0), (0, LANES - rank)))
    head_k = key_dim // GLA_HEADS
    assert head_k & (head_k - 1) == 0 and (head_k.bit_length() - 1) % 2 == 0
    col = jnp.arange(w_in.shape[2])
    q_scale = jnp.where((col >= n_uv) & (col < n_uv + key_dim), head_k ** -0.5, 1.0).astype(F32)
    p, a_lr = _proj(xf, (w_in[0] * q_scale).astype(BF16), w_merge[0].astype(BF16), b_merge, w_a,
                    n_uv, n_h2 - val_dim, val_dim, 1024, 1024)

    s = _sgu(p, sgu_width, sgu_w_s[0], sgu_ln_g[0], sgu_ln_b[0], sgu_b_s[0], 512)
    w_gate = jnp.pad(w_gate_a2[0].astype(BF16), ((0, LANES - rank), (0, 0)))
    o = _gla(p, n_uv, a_lr, w_gate, b_gate_a, gla_norm_g, batch, 1024)
    merged = _merge(s, o, w_branch_a[0].astype(BF16), w_branch_b[0].astype(BF16), p, n_uv + n_h2, 1024, 1024)

    w_r = jnp.concatenate([w_router_group[0], w_router_expert[0]], axis=1)
    n_r = w_r.shape[1]
    w_r = jnp.pad(w_r, ((0, 0), (0, LANES - n_r))).astype(BF16)
    b_r = jnp.pad(jnp.concatenate([b_router_group, b_router_expert], axis=1), ((0, 0), (0, LANES - n_r)))
    h1, h1_slabs, route, counts = _out(merged, w_out[0].astype(BF16), xf, ln1_g, ln1_b, w_r, b_r, 512)

    assert t * TOP_K // MOE_ROWS <= 256, "per-expert block counts must stay exact in bf16"
    dest = _rank(route, counts, MOE_TOKEN_TILE)
    blocks_per_expert = (counts[0, :N_EXPERTS].astype(jnp.int32) + MOE_ROWS - 1) // MOE_ROWS
    block_ends = jnp.cumsum(blocks_per_expert)
    n_blocks = t * TOP_K // MOE_ROWS + N_EXPERTS
    n_used = block_ends[-1:]
    block_ids = jnp.minimum(jnp.arange(n_blocks, dtype=jnp.int32), n_used[0] - 1)
    experts = jnp.arange(N_EXPERTS, dtype=jnp.int32)
    block_expert = jnp.minimum(
        jnp.sum(block_ends[None, :] <= block_ids[:, None], axis=1), N_EXPERTS - 1).astype(jnp.int32)
    later = (experts[None, :] > block_expert[:, None]) & (blocks_per_expert[None, :] > 0)
    next_expert = jnp.min(jnp.where(later, experts[None, :], N_EXPERTS), axis=1)
    next_expert = jnp.where(next_expert < N_EXPERTS, next_expert, -1).astype(jnp.int32)
    tail_ids = n_used[0] + experts
    zrow = jnp.concatenate([
        jnp.where(blocks_per_expert > 0, (block_ends - 1) * MOE_ROWS, -1),
        jnp.where(tail_ids < n_blocks, tail_ids * MOE_ROWS, -1)]).astype(jnp.int32)

    xs = _dispatch(h1_slabs, dest, zrow, n_blocks * MOE_ROWS)
    y = _experts(xs, block_expert, next_expert, n_used.astype(jnp.int32),
                 w_exp_gate[0], w_exp_up[0], w_exp_down[0])
    out = _combine(h1, route, dest, y, ln2_g, ln2_b)
    return out.reshape(batch, seq, d)
```

```python
import functools

import jax
import jax.numpy as jnp
from jax import lax
from jax.experimental import pallas as pl
from jax.experimental.pallas import tpu as pltpu

F32 = jnp.float32
BF16 = jnp.bfloat16

SGU_CHUNK = 128
SGU_GROUPS = 8
GLA_HEADS = 4
GLA_CHUNK = 64
GLA_GATE_NORM = 16.0
GLA_HEADS_PER_STEP = 2
GLA_GROUP_CHUNKS = 4
N_GROUPS = 8
EXPERTS_PER_GROUP = 8
N_EXPERTS = N_GROUPS * EXPERTS_PER_GROUP
TOP_K = 2
LN_EPS = 1e-5
DEEPNORM_ALPHA = 2.0 ** 0.25

LANES = 128
MOE_ROWS = 256
MOE_TOKEN_TILE = 512
ROW_DMA_UNROLL = 8
PROJ_SUBTILES = 2
OUT_SUBTILES = 2
EXPERT_SUBTILES = 1
COMBINE_CHUNK = 32
V7X_VMEM_BYTES = 64 * 2 ** 20


def _vmem_limit(nbytes):
    return int(min(max(2 * nbytes, 16 * 2 ** 20), V7X_VMEM_BYTES - 8 * 2 ** 20))


def _layer_norm(y, g, b):
    mu = jnp.mean(y, axis=-1, keepdims=True)
    var = jnp.mean(jnp.square(y - mu), axis=-1, keepdims=True)
    return (y - mu) * lax.rsqrt(var + LN_EPS) * g + b


def _gelu(x):
    return 0.5 * x * (1.0 + lax.erf(x * (2.0 ** -0.5)))


def _sigmoid(x):
    return 0.5 * (jnp.tanh(0.5 * x) + 1.0)


SLAB_ROWS = 8
HIGH_HALF = 0xFFFF0000


def _bf16_bits(x):
    return lax.bitcast_convert_type(x.astype(BF16).astype(F32), jnp.uint32)


def _store_slabs(ref, row0, x):
    rows, d = x.shape
    assert d == 2 * SLAB_ROWS * LANES
    for s in range(SLAB_ROWS):
        lo = _bf16_bits(x[:, s * LANES:(s + 1) * LANES])
        hi = _bf16_bits(x[:, d // 2 + s * LANES:d // 2 + (s + 1) * LANES])
        word = jnp.right_shift(lo, jnp.uint32(16)) | (hi & jnp.uint32(HIGH_HALF))
        ref[pl.ds(row0 * SLAB_ROWS + s, rows, stride=SLAB_ROWS), :] = word


def _load_slabs(ref, rows, row0=0):
    out = []
    for s in range(SLAB_ROWS):
        word = ref[pl.ds(row0 * SLAB_ROWS + s, rows, stride=SLAB_ROWS), :]
        lo = lax.bitcast_convert_type(jnp.left_shift(word, jnp.uint32(16)), F32)
        hi = lax.bitcast_convert_type(word & jnp.uint32(HIGH_HALF), F32)
        out.append((lo, hi))
    return out


def _proj_kernel(x_ref, w_ref, wm_ref, bm_ref, wa_ref, o_ref, a_ref, xb_ref, *,
                 gelu_blocks, plain_blocks, silu_blocks):
    j = pl.program_id(1)
    silu_start = gelu_blocks + plain_blocks
    sigmoid_start = silu_start + silu_blocks

    contract_last = (((1,), (1,)), ((), ()))

    @pl.when(j == 0)
    def _():
        xb_ref[...] = x_ref[...].astype(BF16)
        a_ref[...] = lax.dot_general(
            xb_ref[...], wa_ref[...], contract_last, preferred_element_type=F32).astype(a_ref.dtype)

    def block(weights, act):
        sub = o_ref.shape[1] // PROJ_SUBTILES
        for cols in (slice(si * sub, (si + 1) * sub) for si in range(PROJ_SUBTILES)):
            if weights is w_ref:
                acc = lax.dot_general(xb_ref[...], w_ref[cols, :], contract_last, preferred_element_type=F32)
            else:
                acc = jnp.dot(xb_ref[...], weights[:, cols], preferred_element_type=F32)
            o_ref[:, cols] = act(acc, cols).astype(o_ref.dtype)

    @pl.when(j < gelu_blocks)
    def _():
        block(w_ref, lambda acc, cols: _gelu(acc))

    @pl.when((j >= gelu_blocks) & (j < silu_start))
    def _():
        block(w_ref, lambda acc, cols: acc)

    @pl.when((j >= silu_start) & (j < sigmoid_start))
    def _():
        block(w_ref, lambda acc, cols: acc * _sigmoid(acc))

    @pl.when(j >= sigmoid_start)
    def _():
        block(wm_ref, lambda acc, cols: _sigmoid(acc + bm_ref[:, cols]))


def _proj(x, w_t, w_m, b_m, w_a_t, n_gelu, n_plain, n_silu, tm, tn):
    m, k = x.shape
    n_in, n_m = n_gelu + n_plain + n_silu, w_m.shape[1]
    tm, tn = min(tm, m), min(tn, n_m)
    assert n_gelu % tn == 0 and n_plain % tn == 0 and n_silu % tn == 0 and n_m % tn == 0
    in_blocks = n_in // tn
    nbytes = tm * k * (4 + 1) + 2 * k * tn * 2 + tm * tn * (2 + 2) + k * LANES * 2
    return pl.pallas_call(
        functools.partial(_proj_kernel, gelu_blocks=n_gelu // tn, plain_blocks=n_plain // tn,
                          silu_blocks=n_silu // tn),
        grid=(m // tm, (n_in + n_m) // tn),
        in_specs=[
            pl.BlockSpec((tm, k), lambda i, j: (i, 0)),
            pl.BlockSpec((tn, k), lambda i, j: (jnp.minimum(j, in_blocks - 1), 0)),
            pl.BlockSpec((k, tn), lambda i, j: (0, jnp.maximum(j - in_blocks, 0))),
            pl.BlockSpec((1, tn), lambda i, j: (0, jnp.maximum(j - in_blocks, 0))),
            pl.BlockSpec((LANES, k), lambda i, j: (0, 0)),
        ],
        out_specs=[pl.BlockSpec((tm, tn), lambda i, j: (i, j)), pl.BlockSpec((tm, LANES), lambda i, j: (i, 0))],
        out_shape=[jax.ShapeDtypeStruct((m, n_in + n_m), BF16), jax.ShapeDtypeStruct((m, LANES), BF16)],
        scratch_shapes=[pltpu.VMEM((tm, k), BF16)],
        compiler_params=pltpu.CompilerParams(
            dimension_semantics=("parallel", "arbitrary"), vmem_limit_bytes=_vmem_limit(nbytes)),
        name="proj",
    )(x, w_t, w_m, b_m, w_a_t)


def _sgu_kernel(u_ref, v_ref, ws_ref, g_ref, b_ref, bs_ref, o_ref):
    c = SGU_CHUNK
    row = lax.broadcasted_iota(jnp.int32, (c, c), 0)
    col = lax.broadcasted_iota(jnp.int32, (c, c), 1)
    causal = row >= col
    for g in range(ws_ref.shape[0]):
        cols = slice(g * c, (g + 1) * c)
        w = jnp.where(causal, ws_ref[g], 0.0).astype(BF16)
        ln_g, ln_b, bias = g_ref[:, cols], b_ref[:, cols], bs_ref[g]
        for ci in range(u_ref.shape[0] // c):
            rows = slice(ci * c, (ci + 1) * c)
            vn = _layer_norm(v_ref[rows, cols].astype(F32), ln_g, ln_b)
            mixed = jnp.dot(w, vn.astype(BF16), preferred_element_type=F32) + bias
            o_ref[rows, cols] = (u_ref[rows, cols].astype(F32) * mixed).astype(o_ref.dtype)


def _sgu(uv, width, w_s, ln_g, ln_b, b_s, tm):
    t = uv.shape[0]
    ng, c, _ = w_s.shape
    tm = min(tm, t)
    return pl.pallas_call(
        _sgu_kernel,
        grid=(t // tm,),
        in_specs=[
            pl.BlockSpec((tm, width), lambda i: (i, 0)),
            pl.BlockSpec((tm, width), lambda i: (i, 1)),
            pl.BlockSpec((ng, c, c), lambda i: (0, 0, 0)),
            pl.BlockSpec((1, width), lambda i: (0, 0)),
            pl.BlockSpec((1, width), lambda i: (0, 0)),
            pl.BlockSpec((ng, c, 1), lambda i: (0, 0, 0)),
        ],
        out_specs=pl.BlockSpec((tm, width), lambda i: (i, 0)),
        out_shape=jax.ShapeDtypeStruct((t, width), BF16),
        compiler_params=pltpu.CompilerParams(dimension_semantics=("parallel",)),
        name="sgu",
    )(uv, uv, w_s, ln_g.reshape(1, width), ln_b.reshape(1, width), b_s.reshape(ng, c, 1))


def _gla_kernel(q_ref, k_ref, v_ref, g_ref, a_ref, wg_ref, bg_ref, ng_ref, o_ref, st_ref):
    c = GLA_CHUNK
    ts = q_ref.shape[0]
    heads, dv, dk = st_ref.shape
    grp = min(ts, GLA_GROUP_CHUNKS * c)
    ncg = grp // c
    contract_last = (((1,), (1,)), ((), ()))
    contract_first = (((0,), (0,)), ((), ()))

    @pl.when(pl.program_id(2) == 0)
    def _():
        st_ref[...] = jnp.zeros_like(st_ref)

    row = lax.broadcasted_iota(jnp.int32, (grp, grp), 0)
    col = lax.broadcasted_iota(jnp.int32, (grp, grp), 1)
    shift = c.bit_length() - 1
    ones_tril = ((row >= col) & (jnp.right_shift(row, shift) == jnp.right_shift(col, shift))).astype(BF16)
    qrow = lax.broadcasted_iota(jnp.int32, (c, grp), 0)
    kcol = lax.broadcasted_iota(jnp.int32, (c, grp), 1)
    visible = [(kcol < j * c) | ((kcol < (j + 1) * c) & (kcol - j * c <= qrow)) for j in range(ncg)]
    zero_keys = jnp.zeros((c, dk), BF16)

    for hd in range(heads):
        kcols = slice(hd * dk, (hd + 1) * dk)
        vcols = slice(hd * dv, (hd + 1) * dv)
        z = jnp.dot(a_ref[...], wg_ref[:, kcols], preferred_element_type=F32) + bg_ref[:, kcols]
        log_a = (jnp.minimum(z, 0.0) - jnp.log1p(jnp.exp(-jnp.abs(z)))) * (1.0 / GLA_GATE_NORM)
        la_hi = log_a.astype(BF16)
        la_lo = (log_a - la_hi.astype(F32)).astype(BF16)
        la_split = jnp.concatenate([la_hi, la_lo], axis=1)
        state_t = st_ref[hd]
        for gi in range(ts // grp):
            rows = slice(gi * grp, (gi + 1) * grp)
            r = jnp.dot(ones_tril, la_split[rows, :], preferred_element_type=F32)
            b = r[:, :dk] + r[:, dk:]
            q = q_ref[rows, kcols].astype(F32) * (dk ** -0.5)
            k = k_ref[rows, kcols].astype(F32)
            v = v_ref[rows, vcols]
            q_dec = q * jnp.exp(b)
            k_inv = (k * jnp.exp(-b)).astype(BF16)
            chunk = [slice(j * c, (j + 1) * c) for j in range(ncg)]
            b_last = [b[(j + 1) * c - 1:(j + 1) * c, :] for j in range(ncg)]
            before = [jnp.zeros_like(b_last[0])]
            for j in range(ncg):
                before.append(before[j] + b_last[j])
            k_to_end = [k[chunk[j], :] * jnp.exp(b_last[j] - b[chunk[j], :]) for j in range(ncg)]

            q_grp = jnp.concatenate(
                [q_dec[chunk[j], :] * jnp.exp(before[j]) for j in range(ncg)], axis=0).astype(BF16)
            o = lax.dot_general(q_grp, state_t.astype(BF16), contract_last, preferred_element_type=F32)

            q_dec = q_dec.astype(BF16)
            attn = []
            for j in range(ncg):
                keys = [(k_to_end[i] * jnp.exp(before[j] - before[i + 1])).astype(BF16) for i in range(j)]
                keys += [k_inv[chunk[j], :]] + [zero_keys] * (ncg - 1 - j)
                keys = jnp.concatenate(keys, axis=0) if ncg > 1 else keys[0]
                scores = lax.dot_general(q_dec[chunk[j], :], keys, contract_last, preferred_element_type=F32)
                attn.append(jnp.where(visible[j], scores, 0.0).astype(BF16))
            attn = jnp.concatenate(attn, axis=0) if ncg > 1 else attn[0]
            o = o + jnp.dot(attn, v, preferred_element_type=F32)

            k_grp = jnp.concatenate(
                [(k_to_end[j] * jnp.exp(before[ncg] - before[j + 1])).astype(BF16) for j in range(ncg)], axis=0)
            state_t = state_t * jnp.exp(before[ncg]) + lax.dot_general(
                v, k_grp, contract_first, preferred_element_type=F32)

            o = o * lax.rsqrt(jnp.mean(jnp.square(o), axis=-1, keepdims=True) + LN_EPS) * ng_ref[:, vcols]
            o_ref[rows, vcols] = (o * g_ref[rows, vcols].astype(F32)).astype(o_ref.dtype)
        st_ref[hd] = state_t


def _gla(h2, col0, a_lr, w_gate, b_gate, norm_g, batch, ts):
    t = h2.shape[0]
    seq = t // batch
    nh = GLA_HEADS
    key_dim = w_gate.shape[1]
    dk = key_dim // nh
    val_dim = norm_g.shape[1]
    dv = val_dim // nh
    ts = min(ts, seq)
    ns = seq // ts
    hps = GLA_HEADS_PER_STEP
    wk, wv = hps * dk, hps * dv
    assert nh % hps == 0 and col0 % wk == 0 and (col0 + 2 * key_dim) % wv == 0
    qb = col0 // wk
    kb, vb, gb = qb + key_dim // wk, (col0 + 2 * key_dim) // wv, (col0 + 2 * key_dim + val_dim) // wv
    tok = lambda b, h, s: b * ns + s
    return pl.pallas_call(
        _gla_kernel,
        grid=(batch, nh // hps, ns),
        in_specs=[
            pl.BlockSpec((ts, wk), lambda b, h, s: (tok(b, h, s), qb + h)),
            pl.BlockSpec((ts, wk), lambda b, h, s: (tok(b, h, s), kb + h)),
            pl.BlockSpec((ts, wv), lambda b, h, s: (tok(b, h, s), vb + h)),
            pl.BlockSpec((ts, wv), lambda b, h, s: (tok(b, h, s), gb + h)),
            pl.BlockSpec((ts, LANES), lambda b, h, s: (tok(b, h, s), 0)),
            pl.BlockSpec((LANES, wk), lambda b, h, s: (0, h)),
            pl.BlockSpec((1, wk), lambda b, h, s: (0, h)),
            pl.BlockSpec((1, wv), lambda b, h, s: (0, h)),
        ],
        out_specs=pl.BlockSpec((ts, wv), lambda b, h, s: (tok(b, h, s), h)),
        out_shape=jax.ShapeDtypeStruct((t, val_dim), BF16),
        scratch_shapes=[pltpu.VMEM((hps, dv, dk), F32)],
        compiler_params=pltpu.CompilerParams(
            dimension_semantics=("parallel", "parallel", "arbitrary"),
            vmem_limit_bytes=_vmem_limit(hps * ts * (dk * 48 + dv * 16))),
        name="gla",
    )(h2, h2, h2, h2, a_lr, w_gate, b_gate, norm_g)


def _merge_kernel(s_ref, o_ref, wa_ref, wb_ref, ga_ref, gb_ref, out_ref):
    half = out_ref.shape[1] // 2
    for cols in (slice(0, half), slice(half, 2 * half)):
        ya = jnp.dot(s_ref[...], wa_ref[:, cols], preferred_element_type=F32)
        yb = jnp.dot(o_ref[...], wb_ref[:, cols], preferred_element_type=F32)
        out_ref[:, cols] = (ga_ref[:, cols].astype(F32) * ya
                            + gb_ref[:, cols].astype(F32) * yb).astype(out_ref.dtype)


def _merge(s, o, w_a, w_b, gates, gate_col0, tm, tn):
    t, ka = s.shape
    kb = o.shape[1]
    d = w_a.shape[1]
    tm, tn = min(tm, t), min(tn, d)
    gj = gate_col0 // tn
    nbytes = tm * (ka + kb) * 2 + (ka + kb) * tn * 2 + tm * tn * (2 * 2 + 2 + 8)
    return pl.pallas_call(
        _merge_kernel,
        grid=(t // tm, d // tn),
        in_specs=[
            pl.BlockSpec((tm, ka), lambda i, j: (i, 0)),
            pl.BlockSpec((tm, kb), lambda i, j: (i, 0)),
            pl.BlockSpec((ka, tn), lambda i, j: (0, j)),
            pl.BlockSpec((kb, tn), lambda i, j: (0, j)),
            pl.BlockSpec((tm, tn), lambda i, j: (i, gj + j)),
            pl.BlockSpec((tm, tn), lambda i, j: (i, gj + j + d // tn)),
        ],
        out_specs=pl.BlockSpec((tm, tn), lambda i, j: (i, j)),
        out_shape=jax.ShapeDtypeStruct((t, d), BF16),
        compiler_params=pltpu.CompilerParams(
            dimension_semantics=("parallel", "arbitrary"), vmem_limit_bytes=_vmem_limit(nbytes)),
        name="merge",
    )(s, o, w_a, w_b, gates, gates)


def _route(logits):
    lane = lax.broadcasted_iota(jnp.int32, logits.shape, 1).astype(F32)
    neg = float("-inf")
    big = float(LANES)
    gl = jnp.where(lane < N_GROUPS, logits, neg)
    gmax = jnp.max(gl, axis=1, keepdims=True)
    gidx = jnp.min(jnp.where(gl == gmax, lane, big), axis=1, keepdims=True)
    p_group = 1.0 / jnp.sum(jnp.exp(gl - gmax), axis=1, keepdims=True)
    lo = N_GROUPS + EXPERTS_PER_GROUP * gidx
    el = jnp.where((lane >= lo) & (lane < lo + EXPERTS_PER_GROUP), logits, neg)
    v1 = jnp.max(el, axis=1, keepdims=True)
    i1 = jnp.min(jnp.where(el == v1, lane, big), axis=1, keepdims=True)
    el2 = jnp.where(lane == i1, neg, el)
    v2 = jnp.max(el2, axis=1, keepdims=True)
    i2 = jnp.min(jnp.where(el2 == v2, lane, big), axis=1, keepdims=True)
    t = jnp.exp(v2 - v1)
    w1 = p_group / (1.0 + t)
    w2 = p_group * t / (1.0 + t)
    return jnp.where(lane == 0, i1 - N_GROUPS,
                     jnp.where(lane == 1, i2 - N_GROUPS,
                               jnp.where(lane == 2, w1, jnp.where(lane == 3, w2, 0.0))))


def _expert_hits(route):
    lane = lax.broadcasted_iota(jnp.int32, route.shape, 1).astype(F32)
    return [lane == route[:, k:k + 1] for k in range(TOP_K)]


def _out_kernel(m_ref, w_ref, x_ref, g_ref, b_ref, wr_ref, br_ref, h_ref, hs_ref, r_ref, cnt_ref, mix_ref):
    i = pl.program_id(0)

    @pl.when(i == 0)
    def _():
        cnt_ref[...] = jnp.zeros_like(cnt_ref)
        mix_ref[1] = jnp.zeros(mix_ref.shape[1:], F32)

    has_prev = (i > 0).astype(F32)
    sub = m_ref.shape[0] // OUT_SUBTILES

    def step(cur):
        for si in range(OUT_SUBTILES):
            rows = slice(si * sub, (si + 1) * sub)
            mix_ref[cur, rows, :] = jnp.dot(m_ref[rows, :], w_ref[...], preferred_element_type=F32)
            h = _layer_norm(DEEPNORM_ALPHA * x_ref[rows, :] + mix_ref[1 - cur, rows, :], g_ref[...], b_ref[...])
            h_ref[rows, :] = h
            _store_slabs(hs_ref, si * sub, h)
            logits = jnp.dot(h.astype(BF16), wr_ref[...], preferred_element_type=F32) + br_ref[...]
            route = _route(logits)
            r_ref[rows, :] = route
            cnt_ref[...] += has_prev * sum(
                jnp.sum(hit.astype(F32), axis=0, keepdims=True) for hit in _expert_hits(route))

    for parity in range(2):
        pl.when(lax.rem(i, 2) == parity)(functools.partial(step, parity))


def _out(merged, w_out, x, ln_g, ln_b, w_r, b_r, tm):
    t, d = x.shape
    tm = min(tm, t)
    n = t // tm
    nbytes = d * d * 2 + tm * d * (2 + 4 + 4 + 2 + 8 + 8) + d * LANES * 2
    cur = lambda i: (jnp.minimum(i, n - 1), 0)
    prev = lambda i: (jnp.maximum(i - 1, 0), 0)
    return pl.pallas_call(
        _out_kernel,
        grid=(n + 1,),
        in_specs=[
            pl.BlockSpec((tm, d), cur),
            pl.BlockSpec((d, d), lambda i: (0, 0)),
            pl.BlockSpec((tm, d), prev),
            pl.BlockSpec((1, d), lambda i: (0, 0)),
            pl.BlockSpec((1, d), lambda i: (0, 0)),
            pl.BlockSpec((d, LANES), lambda i: (0, 0)),
            pl.BlockSpec((1, LANES), lambda i: (0, 0)),
        ],
        out_specs=[pl.BlockSpec((tm, d), prev),
                   pl.BlockSpec((tm * SLAB_ROWS, LANES), prev),
                   pl.BlockSpec((tm, LANES), prev),
                   pl.BlockSpec((1, LANES), lambda i: (0, 0))],
        out_shape=[jax.ShapeDtypeStruct((t, d), F32),
                   jax.ShapeDtypeStruct((t * SLAB_ROWS, LANES), jnp.uint32),
                   jax.ShapeDtypeStruct((t, LANES), F32),
                   jax.ShapeDtypeStruct((1, LANES), F32)],
        scratch_shapes=[pltpu.VMEM((2, tm, d), F32)],
        compiler_params=pltpu.CompilerParams(
            dimension_semantics=("arbitrary",), vmem_limit_bytes=_vmem_limit(nbytes)),
        name="out_ln_route",
    )(merged, w_out, x, ln_g, ln_b, w_r, b_r)


def _rank_kernel(r_ref, cnt_ref, dest_ref, next_ref):
    rows = r_ref.shape[0]

    @pl.when(pl.program_id(0) == 0)
    def _():
        blocks = jnp.floor((cnt_ref[...] + (MOE_ROWS - 1)) * (1.0 / MOE_ROWS))
        k = lax.broadcasted_iota(jnp.int32, (LANES, LANES), 0)
        e = lax.broadcasted_iota(jnp.int32, (LANES, LANES), 1)
        blocks8 = jnp.broadcast_to(blocks, (8, LANES)).astype(BF16)
        first_block = jnp.dot(blocks8, (k < e).astype(BF16), preferred_element_type=F32)
        next_ref[...] = first_block[0:1, :] * MOE_ROWS

    hits = _expert_hits(r_ref[...])
    cnt = sum(hit.astype(F32) for hit in hits)
    row = lax.broadcasted_iota(jnp.int32, (rows, rows), 0)
    col = lax.broadcasted_iota(jnp.int32, (rows, rows), 1)
    earlier = (row > col).astype(BF16)
    slot = jnp.dot(earlier, cnt.astype(BF16), preferred_element_type=F32) + next_ref[...]
    dest = [jnp.sum(jnp.where(hit, slot, 0.0), axis=1, keepdims=True) for hit in hits]
    lane = lax.broadcasted_iota(jnp.int32, (rows, LANES), 1)
    by_token = jnp.where(lane == 0, dest[0], jnp.where(lane == 1, dest[1], 0.0))
    dest_ref[0] = jnp.transpose(by_token)[:TOP_K, :].astype(jnp.int32)
    next_ref[...] += jnp.sum(cnt, axis=0, keepdims=True)


def _rank(route, counts, tm):
    t = route.shape[0]
    tm = min(tm, t)
    return pl.pallas_call(
        _rank_kernel,
        grid=(t // tm,),
        in_specs=[pl.BlockSpec((tm, LANES), lambda i: (i, 0)), pl.BlockSpec((1, LANES), lambda i: (0, 0))],
        out_specs=pl.BlockSpec((1, TOP_K, tm), lambda i: (i, 0, 0)),
        out_shape=jax.ShapeDtypeStruct((t // tm, TOP_K, tm), jnp.int32),
        scratch_shapes=[pltpu.VMEM((1, LANES), F32)],
        compiler_params=pltpu.CompilerParams(dimension_semantics=("arbitrary",)),
        name="rank",
    )(route, counts)


def _slab_copy(src, src_row, dst, dst_row, sem):
    s0 = pl.multiple_of(src_row * SLAB_ROWS, SLAB_ROWS)
    d0 = pl.multiple_of(dst_row * SLAB_ROWS, SLAB_ROWS)
    return pltpu.make_async_copy(src.at[pl.ds(s0, SLAB_ROWS), :], dst.at[pl.ds(d0, SLAB_ROWS), :], sem)


DISPATCH_SLOTS = 3


def _dispatch_kernel(zrow_ref, dest_ref, hs_ref, xs_ref, buf_ref, zero_ref, lsem, ssem, zsem, *, n_steps):
    i = pl.program_id(0)
    rows = buf_ref.shape[1] // SLAB_ROWS
    zrows = zero_ref.shape[0]

    def zero_copy(e):
        start_row = pl.multiple_of(jnp.maximum(zrow_ref[e], 0) * SLAB_ROWS, zrows)
        return pltpu.make_async_copy(zero_ref, xs_ref.at[pl.ds(start_row, zrows), :], zsem)

    def load(tile, slot):
        src = hs_ref.at[pl.ds(pl.multiple_of(tile * rows * SLAB_ROWS, SLAB_ROWS), rows * SLAB_ROWS), :]
        return pltpu.make_async_copy(src, buf_ref.at[slot], lsem.at[slot])

    def drain(slot):
        for k in range(TOP_K):
            pltpu.make_async_copy(buf_ref.at[slot], xs_ref.at[pl.ds(0, rows * SLAB_ROWS), :], ssem.at[slot]).wait()

    @pl.when(i == 0)
    def _():
        zero_ref[...] = jnp.zeros_like(zero_ref)

        def start(e, carry):
            @pl.when(zrow_ref[e] >= 0)
            def _():
                zero_copy(e).start()
            return carry

        def wait(e, carry):
            @pl.when(zrow_ref[e] >= 0)
            def _():
                zero_copy(e).wait()
            return carry

        lax.fori_loop(0, zrow_ref.shape[0], start, 0)
        for ahead in range(min(DISPATCH_SLOTS - 1, n_steps)):
            load(ahead, ahead).start()
        lax.fori_loop(0, zrow_ref.shape[0], wait, 0)

    def step(slot):
        load(i, slot).wait()

        def start(r, carry):
            for k in range(TOP_K):
                _slab_copy(buf_ref.at[slot], r, xs_ref, dest_ref[0, k, r], ssem.at[slot]).start(priority=k)
            return carry

        lax.fori_loop(0, rows, start, 0, unroll=ROW_DMA_UNROLL)
        free = (slot + DISPATCH_SLOTS - 1) % DISPATCH_SLOTS

        @pl.when(i > 0)
        def _():
            drain(free)

        @pl.when(i + DISPATCH_SLOTS - 1 < n_steps)
        def _():
            load(i + DISPATCH_SLOTS - 1, free).start()

        @pl.when(i == n_steps - 1)
        def _():
            drain(slot)

    for slot in range(DISPATCH_SLOTS):
        pl.when(lax.rem(i, DISPATCH_SLOTS) == slot)(functools.partial(step, slot))


def _dispatch(hs, dest, zrow, n_rows):
    n, _, tm = dest.shape
    grid_spec = pltpu.PrefetchScalarGridSpec(
        num_scalar_prefetch=1,
        grid=(n,),
        in_specs=[
            pl.BlockSpec((1, TOP_K, tm), lambda i, z: (i, 0, 0), memory_space=pltpu.SMEM),
            pl.BlockSpec(memory_space=pl.ANY),
        ],
        out_specs=pl.BlockSpec(memory_space=pl.ANY),
        scratch_shapes=[pltpu.VMEM((DISPATCH_SLOTS, tm * SLAB_ROWS, LANES), jnp.uint32),
                        pltpu.VMEM((MOE_ROWS * SLAB_ROWS, LANES), jnp.uint32),
                        pltpu.SemaphoreType.DMA((DISPATCH_SLOTS,)), pltpu.SemaphoreType.DMA((DISPATCH_SLOTS,)),
                        pltpu.SemaphoreType.DMA],
    )
    return pl.pallas_call(
        functools.partial(_dispatch_kernel, n_steps=n),
        grid_spec=grid_spec,
        out_shape=jax.ShapeDtypeStruct((n_rows * SLAB_ROWS, LANES), jnp.uint32),
        compiler_params=pltpu.CompilerParams(dimension_semantics=("arbitrary",)),
        name="dispatch",
    )(zrow, dest, hs)


def _expert_kernel(be_ref, nx_ref, nu_ref, x_ref, w1_ref, w3_ref, w2_ref, y_ref,
                   w1s_ref, w3s_ref, w2s_ref, w1b_ref, w3b_ref, w2b_ref, wsem):
    i = pl.program_id(0)
    d = w1_ref.shape[1]
    rows = x_ref.shape[0] // SLAB_ROWS

    def fetch(e):
        return [pltpu.make_async_copy(src.at[e], dst, wsem.at[n])
                for n, (src, dst) in enumerate(((w1_ref, w1s_ref), (w3_ref, w3s_ref), (w2_ref, w2s_ref)))]

    @pl.when(i == 0)
    def _():
        for copy in fetch(be_ref[0]):
            copy.start()

    @pl.when((i == 0) | (be_ref[i] != be_ref[jnp.maximum(i - 1, 0)]))
    def _():
        for copy in fetch(be_ref[i]):
            copy.wait()
        for s in range(SLAB_ROWS):
            for half, src0 in enumerate((s * LANES, d // 2 + s * LANES)):
                dst0 = (2 * s + half) * LANES
                w1b_ref[dst0:dst0 + LANES, :] = w1s_ref[src0:src0 + LANES, :].astype(BF16)
                w3b_ref[dst0:dst0 + LANES, :] = w3s_ref[src0:src0 + LANES, :].astype(BF16)
        w2b_ref[...] = w2s_ref[...].astype(BF16)

        @pl.when(nx_ref[i] >= 0)
        def _():
            for copy in fetch(nx_ref[i]):
                copy.start()

    @pl.when(i < nu_ref[0])
    def _():
        sub = rows // EXPERT_SUBTILES
        for r0 in range(0, rows, sub):
            x = jnp.concatenate(
                [part.astype(BF16) for pair in _load_slabs(x_ref, sub, r0) for part in pair], axis=1)
            a = jnp.dot(x, w1b_ref[...], preferred_element_type=F32)
            b = jnp.dot(x, w3b_ref[...], preferred_element_type=F32)
            mid = (a * _sigmoid(a) * b).astype(BF16)
            _store_slabs(y_ref, r0, jnp.dot(mid, w2b_ref[...], preferred_element_type=F32))

    @pl.when(i >= nu_ref[0])
    def _():
        y_ref[...] = jnp.zeros_like(y_ref)


def _experts(xs, block_expert, next_expert, n_used, w1, w3, w2):
    _, d, de = w1.shape
    nb = xs.shape[0] // (MOE_ROWS * SLAB_ROWS)
    nbytes = MOE_ROWS * d * (2 + 2 + 2 + 4) + 3 * d * de * (2 + 1) + MOE_ROWS * de * 12
    grid_spec = pltpu.PrefetchScalarGridSpec(
        num_scalar_prefetch=3,
        grid=(nb,),
        in_specs=[
            pl.BlockSpec((MOE_ROWS * SLAB_ROWS, LANES), lambda i, be, nx, nu: (jnp.minimum(i, nu[0] - 1), 0)),
            pl.BlockSpec(memory_space=pl.ANY),
            pl.BlockSpec(memory_space=pl.ANY),
            pl.BlockSpec(memory_space=pl.ANY),
        ],
        out_specs=pl.BlockSpec((MOE_ROWS * SLAB_ROWS, LANES), lambda i, be, nx, nu: (i, 0)),
        scratch_shapes=[pltpu.VMEM((d, de), F32), pltpu.VMEM((d, de), F32), pltpu.VMEM((de, d), F32),
                        pltpu.VMEM((d, de), BF16), pltpu.VMEM((d, de), BF16), pltpu.VMEM((de, d), BF16),
                        pltpu.SemaphoreType.DMA((3,))],
    )
    return pl.pallas_call(
        _expert_kernel,
        grid_spec=grid_spec,
        out_shape=jax.ShapeDtypeStruct(xs.shape, jnp.uint32),
        compiler_params=pltpu.CompilerParams(
            dimension_semantics=("arbitrary",), vmem_limit_bytes=_vmem_limit(nbytes)),
        name="experts",
    )(block_expert, next_expert, n_used, xs, w1, w3, w2)


def _combine_kernel(dest_ref, dnext_ref, h_ref, r_ref, y_ref, g_ref, b_ref, o_ref, buf_ref, sem, *, n_steps):
    i = pl.program_id(0)
    rows = h_ref.shape[0]
    chunk = min(rows, COMBINE_CHUNK)

    def gather(dref, slot, r, k):
        return _slab_copy(y_ref, dref[0, k, r], buf_ref.at[slot, k], r, sem.at[slot])

    def wait(slot):
        for k in range(TOP_K):
            pltpu.make_async_copy(y_ref.at[pl.ds(0, rows * SLAB_ROWS), :], buf_ref.at[slot, k],
                                  sem.at[slot]).wait()

    @pl.when(i == 0)
    def _():
        def start(r, carry):
            for k in range(TOP_K):
                gather(dest_ref, 0, r, k).start(priority=k)
            return carry

        lax.fori_loop(0, rows, start, 0, unroll=ROW_DMA_UNROLL)

    def step(slot):
        wait(slot)
        for c0 in range(0, rows, chunk):
            route = r_ref[c0:c0 + chunk, :]
            slabs = [_load_slabs(buf_ref.at[slot, k], chunk, c0) for k in range(TOP_K)]
            lo, hi = [], []
            for s in range(SLAB_ROWS):
                for half, out in enumerate((lo, hi)):
                    out.append(sum(route[:, 2 + k:3 + k] * slabs[k][s][half] for k in range(TOP_K)))
            moe = jnp.concatenate(lo + hi, axis=1)
            o_ref[c0:c0 + chunk, :] = _layer_norm(
                DEEPNORM_ALPHA * h_ref[c0:c0 + chunk, :] + moe, g_ref[...], b_ref[...])
            for r in range(c0, c0 + chunk):
                for k in range(TOP_K):
                    gather(dnext_ref, 1 - slot, r, k).start(priority=k)

    for parity in range(2):
        pl.when(lax.rem(i, 2) == parity)(functools.partial(step, parity))

    @pl.when(i == n_steps - 1)
    def _():
        wait(n_steps % 2)


def _combine(h, route, dest, y, ln_g, ln_b):
    t, d = h.shape
    n, _, tm = dest.shape
    return pl.pallas_call(
        functools.partial(_combine_kernel, n_steps=n),
        grid=(n,),
        in_specs=[
            pl.BlockSpec((1, TOP_K, tm), lambda i: (i, 0, 0), memory_space=pltpu.SMEM),
            pl.BlockSpec((1, TOP_K, tm), lambda i: (jnp.minimum(i + 1, n - 1), 0, 0), memory_space=pltpu.SMEM),
            pl.BlockSpec((tm, d), lambda i: (i, 0)),
            pl.BlockSpec((tm, LANES), lambda i: (i, 0)),
            pl.BlockSpec(memory_space=pl.ANY),
            pl.BlockSpec((1, d), lambda i: (0, 0)),
            pl.BlockSpec((1, d), lambda i: (0, 0)),
        ],
        out_specs=pl.BlockSpec((tm, d), lambda i: (i, 0)),
        out_shape=jax.ShapeDtypeStruct((t, d), F32),
        scratch_shapes=[pltpu.VMEM((2, TOP_K, tm * SLAB_ROWS, LANES), jnp.uint32), pltpu.SemaphoreType.DMA((2,))],
        compiler_params=pltpu.CompilerParams(dimension_semantics=("arbitrary",)),
        name="combine",
    )(dest, dest, h, route, y, ln_g, ln_b)


def kernel(x, w_in, w_gate_a2, b_gate_a, sgu_ln_g, sgu_ln_b, sgu_w_s, sgu_b_s, gla_norm_g, w_branch_a, w_branch_b, w_merge, b_merge, w_out, ln1_g, ln1_b, w_router_group, b_router_group, w_router_expert, b_router_expert, w_exp_gate, w_exp_up, w_exp_down, ln2_g, ln2_b):
    batch, seq, d = x.shape
    t = batch * seq
    assert w_in.shape[0] == 1, "one layer"
    assert seq % SGU_CHUNK == 0 and seq % GLA_CHUNK == 0 and t % MOE_ROWS == 0
    sgu_width = sgu_ln_g.shape[1]
    key_dim = w_gate_a2.shape[2]
    val_dim = gla_norm_g.shape[1]
    rank = w_gate_a2.shape[1]
    xf = x.reshape(t, d)
    n_uv, n_h2 = 2 * sgu_width, 2 * key_dim + 2 * val_dim

    w_in_t = jnp.swapaxes(w_in.reshape(w_in.shape[1:]), 0, 1).astype(BF16)
    w_a_t = jnp.pad(w_in_t[n_uv + n_h2:, :], ((0, LANES - rank), (0, 0)))
    p, a_lr = _proj(xf, w_in_t, w_merge[0].astype(BF16), b_merge, w_a_t,
                    n_uv, n_h2 - val_dim, val_dim, 1024, 1024)

    s = _sgu(p, sgu_width, sgu_w_s[0], sgu_ln_g[0], sgu_ln_b[0], sgu_b_s[0], 512)
    w_gate = jnp.pad(w_gate_a2[0].astype(BF16), ((0, LANES - rank), (0, 0)))
    o = _gla(p, n_uv, a_lr, w_gate, b_gate_a, gla_norm_g, batch, 1024)
    merged = _merge(s, o, w_branch_a[0].astype(BF16), w_branch_b[0].astype(BF16), p, n_uv + n_h2, 1024, 1024)

    w_r = jnp.concatenate([w_router_group[0], w_router_expert[0]], axis=1)
    n_r = w_r.shape[1]
    w_r = jnp.pad(w_r, ((0, 0), (0, LANES - n_r))).astype(BF16)
    b_r = jnp.pad(jnp.concatenate([b_router_group, b_router_expert], axis=1), ((0, 0), (0, LANES - n_r)))
    h1, h1_slabs, route, counts = _out(merged, w_out[0].astype(BF16), xf, ln1_g, ln1_b, w_r, b_r, 512)

    assert t * TOP_K // MOE_ROWS <= 256, "per-expert block counts must stay exact in bf16"
    dest = _rank(route, counts, MOE_TOKEN_TILE)
    blocks_per_expert = (counts[0, :N_EXPERTS].astype(jnp.int32) + MOE_ROWS - 1) // MOE_ROWS
    block_ends = jnp.cumsum(blocks_per_expert)
    n_blocks = t * TOP_K // MOE_ROWS + N_EXPERTS
    n_used = block_ends[-1:]
    block_ids = jnp.minimum(jnp.arange(n_blocks, dtype=jnp.int32), n_used[0] - 1)
    experts = jnp.arange(N_EXPERTS, dtype=jnp.int32)
    block_expert = jnp.minimum(
        jnp.sum(block_ends[None, :] <= block_ids[:, None], axis=1), N_EXPERTS - 1).astype(jnp.int32)
    later = (experts[None, :] > block_expert[:, None]) & (blocks_per_expert[None, :] > 0)
    next_expert = jnp.min(jnp.where(later, experts[None, :], N_EXPERTS), axis=1)
    next_expert = jnp.where(next_expert < N_EXPERTS, next_expert, -1).astype(jnp.int32)
    tail_ids = n_used[0] + experts
    zrow = jnp.concatenate([
        jnp.where(blocks_per_expert > 0, (block_ends - 1) * MOE_ROWS, -1),
        jnp.where(tail_ids < n_blocks, tail_ids * MOE_ROWS, -1)]).astype(jnp.int32)

    xs = _dispatch(h1_slabs, dest, zrow, n_blocks * MOE_ROWS)
    y = _experts(xs, block_expert, next_expert, n_used.astype(jnp.int32),
                 w_exp_gate[0], w_exp_up[0], w_exp_down[0])
    out = _combine(h1, route, dest, y, ln2_g, ln2_b)
    return out.reshape(batch, seq, d)
```

```python
import functools

import jax
import jax.numpy as jnp
from jax import lax
from jax.experimental import pallas as pl
from jax.experimental.pallas import tpu as pltpu

F32 = jnp.float32
BF16 = jnp.bfloat16

SGU_CHUNK = 128
SGU_GROUPS = 8
GLA_HEADS = 4
GLA_CHUNK = 64
GLA_GATE_NORM = 16.0
GLA_HEADS_PER_STEP = 2
GLA_GROUP_CHUNKS = 4
N_GROUPS = 8
EXPERTS_PER_GROUP = 8
N_EXPERTS = N_GROUPS * EXPERTS_PER_GROUP
TOP_K = 2
LN_EPS = 1e-5
DEEPNORM_ALPHA = 2.0 ** 0.25

LANES = 128
MOE_ROWS = 256
MOE_TOKEN_TILE = 512
ROW_DMA_UNROLL = 8
PROJ_SUBTILES = 2
OUT_SUBTILES = 2
EXPERT_SUBTILES = 1
COMBINE_CHUNK = 32
V7X_VMEM_BYTES = 64 * 2 ** 20


def _vmem_limit(nbytes):
    return int(min(max(2 * nbytes, 16 * 2 ** 20), V7X_VMEM_BYTES - 8 * 2 ** 20))


def _layer_norm(y, g, b):
    mu = jnp.mean(y, axis=-1, keepdims=True)
    var = jnp.mean(jnp.square(y - mu), axis=-1, keepdims=True)
    return (y - mu) * lax.rsqrt(var + LN_EPS) * g + b


def _gelu(x):
    return 0.5 * x * (1.0 + lax.erf(x * (2.0 ** -0.5)))


def _sigmoid(x):
    return 0.5 * (jnp.tanh(0.5 * x) + 1.0)


SLAB_ROWS = 8
HIGH_HALF = 0xFFFF0000


def _bf16_bits(x):
    return lax.bitcast_convert_type(x.astype(BF16).astype(F32), jnp.uint32)


def _store_slabs(ref, row0, x):
    rows, d = x.shape
    assert d == 2 * SLAB_ROWS * LANES
    for s in range(SLAB_ROWS):
        lo = _bf16_bits(x[:, s * LANES:(s + 1) * LANES])
        hi = _bf16_bits(x[:, d // 2 + s * LANES:d // 2 + (s + 1) * LANES])
        word = jnp.right_shift(lo, jnp.uint32(16)) | (hi & jnp.uint32(HIGH_HALF))
        ref[pl.ds(row0 * SLAB_ROWS + s, rows, stride=SLAB_ROWS), :] = word


def _load_slabs(ref, rows, row0=0):
    out = []
    for s in range(SLAB_ROWS):
        word = ref[pl.ds(row0 * SLAB_ROWS + s, rows, stride=SLAB_ROWS), :]
        lo = lax.bitcast_convert_type(jnp.left_shift(word, jnp.uint32(16)), F32)
        hi = lax.bitcast_convert_type(word & jnp.uint32(HIGH_HALF), F32)
        out.append((lo, hi))
    return out


def _proj_kernel(x_ref, w_ref, wm_ref, bm_ref, wa_ref, ws_ref, lg_ref, lb_ref, bs_ref,
                 o_ref, a_ref, s_ref, xb_ref, u_ref, v_ref, *, plain_blocks, silu_blocks, n_blocks):
    j = pl.program_id(1)
    sgu_blocks = 2
    silu_start = sgu_blocks + plain_blocks
    sigmoid_start = silu_start + silu_blocks
    contract_last = (((1,), (1,)), ((), ()))
    tm, tn = o_ref.shape
    sub = tn // PROJ_SUBTILES
    c = SGU_CHUNK
    groups = tn // c
    assert sgu_blocks + groups <= n_blocks

    def x_dot_w(cols):
        return lax.dot_general(xb_ref[...], w_ref[cols, :], contract_last, preferred_element_type=F32)

    def x_dot_wm(cols):
        return jnp.dot(xb_ref[...], wm_ref[:, cols], preferred_element_type=F32)

    def sgu_group(g):
        row = lax.broadcasted_iota(jnp.int32, (c, c), 0)
        col = lax.broadcasted_iota(jnp.int32, (c, c), 1)
        cols = slice(g * c, (g + 1) * c)
        w = jnp.where(row >= col, ws_ref[g], 0.0).astype(BF16)
        ln_g, ln_b, bias = lg_ref[:, cols], lb_ref[:, cols], bs_ref[g]
        for ci in range(tm // c):
            rows = slice(ci * c, (ci + 1) * c)
            vn = _layer_norm(v_ref[rows, cols].astype(F32), ln_g, ln_b)
            mixed = jnp.dot(w, vn.astype(BF16), preferred_element_type=F32) + bias
            s_ref[rows, cols] = (u_ref[rows, cols].astype(F32) * mixed).astype(s_ref.dtype)

    def block(jj):
        for cols in (slice(si * sub, (si + 1) * sub) for si in range(PROJ_SUBTILES)):
            if jj == 0:
                u_ref[:, cols] = _gelu(x_dot_w(cols)).astype(u_ref.dtype)
            elif jj == 1:
                v_ref[:, cols] = _gelu(x_dot_w(cols)).astype(v_ref.dtype)
            elif jj < silu_start:
                o_ref[:, cols] = x_dot_w(cols).astype(o_ref.dtype)
            elif jj < sigmoid_start:
                acc = x_dot_w(cols)
                o_ref[:, cols] = (acc * _sigmoid(acc)).astype(o_ref.dtype)
            else:
                o_ref[:, cols] = _sigmoid(x_dot_wm(cols) + bm_ref[:, cols]).astype(o_ref.dtype)
        if sgu_blocks <= jj < sgu_blocks + groups:
            sgu_group(jj - sgu_blocks)

    @pl.when(j == 0)
    def _():
        xb_ref[...] = x_ref[...].astype(BF16)
        a_ref[...] = lax.dot_general(
            xb_ref[...], wa_ref[...], contract_last, preferred_element_type=F32).astype(a_ref.dtype)

    for jj in range(n_blocks):
        pl.when(j == jj)(functools.partial(block, jj))


def _proj(x, w_t, w_m, b_m, w_a_t, w_s, ln_g, ln_b, b_s, n_plain, n_silu, tm):
    m, k = x.shape
    ng, c, _ = w_s.shape
    width = tn = ng * c
    n_m = w_m.shape[1]
    tm = min(tm, m)
    assert n_plain % tn == 0 and n_silu % tn == 0 and n_m % tn == 0 and tm % c == 0
    in_blocks = 2 + (n_plain + n_silu) // tn
    p_block = lambda i, j: (i, jnp.maximum(j - 2, 0))
    nbytes = tm * k * (4 + 1) + 2 * k * tn * 2 + tm * tn * (2 + 2 + 2 + 1) + k * LANES * 2
    return pl.pallas_call(
        functools.partial(_proj_kernel, plain_blocks=n_plain // tn, silu_blocks=n_silu // tn,
                          n_blocks=in_blocks + n_m // tn),
        grid=(m // tm, in_blocks + n_m // tn),
        in_specs=[
            pl.BlockSpec((tm, k), lambda i, j: (i, 0)),
            pl.BlockSpec((tn, k), lambda i, j: (jnp.minimum(j, in_blocks - 1), 0)),
            pl.BlockSpec((k, tn), lambda i, j: (0, jnp.maximum(j - in_blocks, 0))),
            pl.BlockSpec((1, tn), lambda i, j: (0, jnp.maximum(j - in_blocks, 0))),
            pl.BlockSpec((LANES, k), lambda i, j: (0, 0)),
            pl.BlockSpec((ng, c, c), lambda i, j: (0, 0, 0)),
            pl.BlockSpec((1, width), lambda i, j: (0, 0)),
            pl.BlockSpec((1, width), lambda i, j: (0, 0)),
            pl.BlockSpec((ng, c, 1), lambda i, j: (0, 0, 0)),
        ],
        out_specs=[pl.BlockSpec((tm, tn), p_block), pl.BlockSpec((tm, LANES), lambda i, j: (i, 0)),
                   pl.BlockSpec((tm, width), lambda i, j: (i, 0))],
        out_shape=[jax.ShapeDtypeStruct((m, n_plain + n_silu + n_m), BF16),
                   jax.ShapeDtypeStruct((m, LANES), BF16), jax.ShapeDtypeStruct((m, width), BF16)],
        scratch_shapes=[pltpu.VMEM((tm, k), BF16), pltpu.VMEM((tm, width), BF16), pltpu.VMEM((tm, width), BF16)],
        compiler_params=pltpu.CompilerParams(
            dimension_semantics=("parallel", "arbitrary"), vmem_limit_bytes=_vmem_limit(nbytes)),
        name="proj",
    )(x, w_t, w_m, b_m, w_a_t, w_s, ln_g.reshape(1, width), ln_b.reshape(1, width), b_s.reshape(ng, c, 1))


def _gla_kernel(q_ref, k_ref, v_ref, g_ref, a_ref, wg_ref, bg_ref, ng_ref, o_ref, st_ref):
    c = GLA_CHUNK
    ts = q_ref.shape[0]
    heads, dv, dk = st_ref.shape
    grp = min(ts, GLA_GROUP_CHUNKS * c)
    ncg = grp // c
    contract_last = (((1,), (1,)), ((), ()))
    contract_first = (((0,), (0,)), ((), ()))

    @pl.when(pl.program_id(2) == 0)
    def _():
        st_ref[...] = jnp.zeros_like(st_ref)

    row = lax.broadcasted_iota(jnp.int32, (grp, grp), 0)
    col = lax.broadcasted_iota(jnp.int32, (grp, grp), 1)
    shift = c.bit_length() - 1
    ones_tril = ((row >= col) & (jnp.right_shift(row, shift) == jnp.right_shift(col, shift))).astype(BF16)
    qrow = lax.broadcasted_iota(jnp.int32, (c, grp), 0)
    kcol = lax.broadcasted_iota(jnp.int32, (c, grp), 1)
    visible = [(kcol < j * c) | ((kcol < (j + 1) * c) & (kcol - j * c <= qrow)) for j in range(ncg)]
    zero_keys = jnp.zeros((c, dk), BF16)

    for hd in range(heads):
        kcols = slice(hd * dk, (hd + 1) * dk)
        vcols = slice(hd * dv, (hd + 1) * dv)
        z = jnp.dot(a_ref[...], wg_ref[:, kcols], preferred_element_type=F32) + bg_ref[:, kcols]
        log_a = (jnp.minimum(z, 0.0) - jnp.log1p(jnp.exp(-jnp.abs(z)))) * (1.0 / GLA_GATE_NORM)
        la_hi = log_a.astype(BF16)
        la_lo = (log_a - la_hi.astype(F32)).astype(BF16)
        la_split = jnp.concatenate([la_hi, la_lo], axis=1)
        state_t = st_ref[hd]
        for gi in range(ts // grp):
            rows = slice(gi * grp, (gi + 1) * grp)
            r = jnp.dot(ones_tril, la_split[rows, :], preferred_element_type=F32)
            b = r[:, :dk] + r[:, dk:]
            q = q_ref[rows, kcols].astype(F32) * (dk ** -0.5)
            k = k_ref[rows, kcols].astype(F32)
            v = v_ref[rows, vcols]
            q_dec = q * jnp.exp(b)
            k_inv = (k * jnp.exp(-b)).astype(BF16)
            chunk = [slice(j * c, (j + 1) * c) for j in range(ncg)]
            b_last = [b[(j + 1) * c - 1:(j + 1) * c, :] for j in range(ncg)]
            before = [jnp.zeros_like(b_last[0])]
            for j in range(ncg):
                before.append(before[j] + b_last[j])
            k_to_end = [k[chunk[j], :] * jnp.exp(b_last[j] - b[chunk[j], :]) for j in range(ncg)]

            q_grp = jnp.concatenate(
                [q_dec[chunk[j], :] * jnp.exp(before[j]) for j in range(ncg)], axis=0).astype(BF16)
            o = lax.dot_general(q_grp, state_t.astype(BF16), contract_last, preferred_element_type=F32)

            q_dec = q_dec.astype(BF16)
            attn = []
            for j in range(ncg):
                keys = [(k_to_end[i] * jnp.exp(before[j] - before[i + 1])).astype(BF16) for i in range(j)]
                keys += [k_inv[chunk[j], :]] + [zero_keys] * (ncg - 1 - j)
                keys = jnp.concatenate(keys, axis=0) if ncg > 1 else keys[0]
                scores = lax.dot_general(q_dec[chunk[j], :], keys, contract_last, preferred_element_type=F32)
                attn.append(jnp.where(visible[j], scores, 0.0).astype(BF16))
            attn = jnp.concatenate(attn, axis=0) if ncg > 1 else attn[0]
            o = o + jnp.dot(attn, v, preferred_element_type=F32)

            k_grp = jnp.concatenate(
                [(k_to_end[j] * jnp.exp(before[ncg] - before[j + 1])).astype(BF16) for j in range(ncg)], axis=0)
            state_t = state_t * jnp.exp(before[ncg]) + lax.dot_general(
                v, k_grp, contract_first, preferred_element_type=F32)

            o = o * lax.rsqrt(jnp.mean(jnp.square(o), axis=-1, keepdims=True) + LN_EPS) * ng_ref[:, vcols]
            o_ref[rows, vcols] = (o * g_ref[rows, vcols].astype(F32)).astype(o_ref.dtype)
        st_ref[hd] = state_t


def _gla(h2, col0, a_lr, w_gate, b_gate, norm_g, batch, ts):
    t = h2.shape[0]
    seq = t // batch
    nh = GLA_HEADS
    key_dim = w_gate.shape[1]
    dk = key_dim // nh
    val_dim = norm_g.shape[1]
    dv = val_dim // nh
    ts = min(ts, seq)
    ns = seq // ts
    hps = GLA_HEADS_PER_STEP
    wk, wv = hps * dk, hps * dv
    assert nh % hps == 0 and col0 % wk == 0 and (col0 + 2 * key_dim) % wv == 0
    qb = col0 // wk
    kb, vb, gb = qb + key_dim // wk, (col0 + 2 * key_dim) // wv, (col0 + 2 * key_dim + val_dim) // wv
    tok = lambda b, h, s: b * ns + s
    return pl.pallas_call(
        _gla_kernel,
        grid=(batch, nh // hps, ns),
        in_specs=[
            pl.BlockSpec((ts, wk), lambda b, h, s: (tok(b, h, s), qb + h)),
            pl.BlockSpec((ts, wk), lambda b, h, s: (tok(b, h, s), kb + h)),
            pl.BlockSpec((ts, wv), lambda b, h, s: (tok(b, h, s), vb + h)),
            pl.BlockSpec((ts, wv), lambda b, h, s: (tok(b, h, s), gb + h)),
            pl.BlockSpec((ts, LANES), lambda b, h, s: (tok(b, h, s), 0)),
            pl.BlockSpec((LANES, wk), lambda b, h, s: (0, h)),
            pl.BlockSpec((1, wk), lambda b, h, s: (0, h)),
            pl.BlockSpec((1, wv), lambda b, h, s: (0, h)),
        ],
        out_specs=pl.BlockSpec((ts, wv), lambda b, h, s: (tok(b, h, s), h)),
        out_shape=jax.ShapeDtypeStruct((t, val_dim), BF16),
        scratch_shapes=[pltpu.VMEM((hps, dv, dk), F32)],
        compiler_params=pltpu.CompilerParams(
            dimension_semantics=("parallel", "parallel", "arbitrary"),
            vmem_limit_bytes=_vmem_limit(hps * ts * (dk * 48 + dv * 16))),
        name="gla",
    )(h2, h2, h2, h2, a_lr, w_gate, b_gate, norm_g)


def _merge_kernel(s_ref, o_ref, wa_ref, wb_ref, ga_ref, gb_ref, out_ref):
    half = out_ref.shape[1] // 2
    for cols in (slice(0, half), slice(half, 2 * half)):
        ya = jnp.dot(s_ref[...], wa_ref[:, cols], preferred_element_type=F32)
        yb = jnp.dot(o_ref[...], wb_ref[:, cols], preferred_element_type=F32)
        out_ref[:, cols] = (ga_ref[:, cols].astype(F32) * ya
                            + gb_ref[:, cols].astype(F32) * yb).astype(out_ref.dtype)


def _merge(s, o, w_a, w_b, gates, gate_col0, tm, tn):
    t, ka = s.shape
    kb = o.shape[1]
    d = w_a.shape[1]
    tm, tn = min(tm, t), min(tn, d)
    gj = gate_col0 // tn
    nbytes = tm * (ka + kb) * 2 + (ka + kb) * tn * 2 + tm * tn * (2 * 2 + 2 + 8)
    return pl.pallas_call(
        _merge_kernel,
        grid=(t // tm, d // tn),
        in_specs=[
            pl.BlockSpec((tm, ka), lambda i, j: (i, 0)),
            pl.BlockSpec((tm, kb), lambda i, j: (i, 0)),
            pl.BlockSpec((ka, tn), lambda i, j: (0, j)),
            pl.BlockSpec((kb, tn), lambda i, j: (0, j)),
            pl.BlockSpec((tm, tn), lambda i, j: (i, gj + j)),
            pl.BlockSpec((tm, tn), lambda i, j: (i, gj + j + d // tn)),
        ],
        out_specs=pl.BlockSpec((tm, tn), lambda i, j: (i, j)),
        out_shape=jax.ShapeDtypeStruct((t, d), BF16),
        compiler_params=pltpu.CompilerParams(
            dimension_semantics=("parallel", "arbitrary"), vmem_limit_bytes=_vmem_limit(nbytes)),
        name="merge",
    )(s, o, w_a, w_b, gates, gates)


def _route(logits):
    lane = lax.broadcasted_iota(jnp.int32, logits.shape, 1).astype(F32)
    neg = float("-inf")
    big = float(LANES)
    gl = jnp.where(lane < N_GROUPS, logits, neg)
    gmax = jnp.max(gl, axis=1, keepdims=True)
    gidx = jnp.min(jnp.where(gl == gmax, lane, big), axis=1, keepdims=True)
    p_group = 1.0 / jnp.sum(jnp.exp(gl - gmax), axis=1, keepdims=True)
    lo = N_GROUPS + EXPERTS_PER_GROUP * gidx
    el = jnp.where((lane >= lo) & (lane < lo + EXPERTS_PER_GROUP), logits, neg)
    v1 = jnp.max(el, axis=1, keepdims=True)
    i1 = jnp.min(jnp.where(el == v1, lane, big), axis=1, keepdims=True)
    el2 = jnp.where(lane == i1, neg, el)
    v2 = jnp.max(el2, axis=1, keepdims=True)
    i2 = jnp.min(jnp.where(el2 == v2, lane, big), axis=1, keepdims=True)
    t = jnp.exp(v2 - v1)
    w1 = p_group / (1.0 + t)
    w2 = p_group * t / (1.0 + t)
    return jnp.where(lane == 0, i1 - N_GROUPS,
                     jnp.where(lane == 1, i2 - N_GROUPS,
                               jnp.where(lane == 2, w1, jnp.where(lane == 3, w2, 0.0))))


def _expert_hits(route):
    lane = lax.broadcasted_iota(jnp.int32, route.shape, 1).astype(F32)
    return [lane == route[:, k:k + 1] for k in range(TOP_K)]


def _out_kernel(m_ref, w_ref, x_ref, g_ref, b_ref, wr_ref, br_ref, h_ref, hs_ref, r_ref, cnt_ref, mix_ref):
    i = pl.program_id(0)

    @pl.when(i == 0)
    def _():
        cnt_ref[...] = jnp.zeros_like(cnt_ref)
        mix_ref[1] = jnp.zeros(mix_ref.shape[1:], F32)

    has_prev = (i > 0).astype(F32)
    sub = m_ref.shape[0] // OUT_SUBTILES

    def step(cur):
        for si in range(OUT_SUBTILES):
            rows = slice(si * sub, (si + 1) * sub)
            mix_ref[cur, rows, :] = jnp.dot(m_ref[rows, :], w_ref[...], preferred_element_type=F32)
            h = _layer_norm(DEEPNORM_ALPHA * x_ref[rows, :] + mix_ref[1 - cur, rows, :], g_ref[...], b_ref[...])
            h_ref[rows, :] = h
            _store_slabs(hs_ref, si * sub, h)
            logits = jnp.dot(h.astype(BF16), wr_ref[...], preferred_element_type=F32) + br_ref[...]
            route = _route(logits)
            r_ref[rows, :] = route
            cnt_ref[...] += has_prev * sum(
                jnp.sum(hit.astype(F32), axis=0, keepdims=True) for hit in _expert_hits(route))

    for parity in range(2):
        pl.when(lax.rem(i, 2) == parity)(functools.partial(step, parity))


def _out(merged, w_out, x, ln_g, ln_b, w_r, b_r, tm):
    t, d = x.shape
    tm = min(tm, t)
    n = t // tm
    nbytes = d * d * 2 + tm * d * (2 + 4 + 4 + 2 + 8 + 8) + d * LANES * 2
    cur = lambda i: (jnp.minimum(i, n - 1), 0)
    prev = lambda i: (jnp.maximum(i - 1, 0), 0)
    return pl.pallas_call(
        _out_kernel,
        grid=(n + 1,),
        in_specs=[
            pl.BlockSpec((tm, d), cur),
            pl.BlockSpec((d, d), lambda i: (0, 0)),
            pl.BlockSpec((tm, d), prev),
            pl.BlockSpec((1, d), lambda i: (0, 0)),
            pl.BlockSpec((1, d), lambda i: (0, 0)),
            pl.BlockSpec((d, LANES), lambda i: (0, 0)),
            pl.BlockSpec((1, LANES), lambda i: (0, 0)),
        ],
        out_specs=[pl.BlockSpec((tm, d), prev),
                   pl.BlockSpec((tm * SLAB_ROWS, LANES), prev),
                   pl.BlockSpec((tm, LANES), prev),
                   pl.BlockSpec((1, LANES), lambda i: (0, 0))],
        out_shape=[jax.ShapeDtypeStruct((t, d), F32),
                   jax.ShapeDtypeStruct((t * SLAB_ROWS, LANES), jnp.uint32),
                   jax.ShapeDtypeStruct((t, LANES), F32),
                   jax.ShapeDtypeStruct((1, LANES), F32)],
        scratch_shapes=[pltpu.VMEM((2, tm, d), F32)],
        compiler_params=pltpu.CompilerParams(
            dimension_semantics=("arbitrary",), vmem_limit_bytes=_vmem_limit(nbytes)),
        name="out_ln_route",
    )(merged, w_out, x, ln_g, ln_b, w_r, b_r)


def _rank_kernel(r_ref, cnt_ref, dest_ref, next_ref):
    rows = r_ref.shape[0]

    @pl.when(pl.program_id(0) == 0)
    def _():
        blocks = jnp.floor((cnt_ref[...] + (MOE_ROWS - 1)) * (1.0 / MOE_ROWS))
        k = lax.broadcasted_iota(jnp.int32, (LANES, LANES), 0)
        e = lax.broadcasted_iota(jnp.int32, (LANES, LANES), 1)
        blocks8 = jnp.broadcast_to(blocks, (8, LANES)).astype(BF16)
        first_block = jnp.dot(blocks8, (k < e).astype(BF16), preferred_element_type=F32)
        next_ref[...] = first_block[0:1, :] * MOE_ROWS

    hits = _expert_hits(r_ref[...])
    cnt = sum(hit.astype(F32) for hit in hits)
    row = lax.broadcasted_iota(jnp.int32, (rows, rows), 0)
    col = lax.broadcasted_iota(jnp.int32, (rows, rows), 1)
    earlier = (row > col).astype(BF16)
    slot = jnp.dot(earlier, cnt.astype(BF16), preferred_element_type=F32) + next_ref[...]
    dest = [jnp.sum(jnp.where(hit, slot, 0.0), axis=1, keepdims=True) for hit in hits]
    lane = lax.broadcasted_iota(jnp.int32, (rows, LANES), 1)
    by_token = jnp.where(lane == 0, dest[0], jnp.where(lane == 1, dest[1], 0.0))
    dest_ref[0] = jnp.transpose(by_token)[:TOP_K, :].astype(jnp.int32)
    next_ref[...] += jnp.sum(cnt, axis=0, keepdims=True)


def _rank(route, counts, tm):
    t = route.shape[0]
    tm = min(tm, t)
    return pl.pallas_call(
        _rank_kernel,
        grid=(t // tm,),
        in_specs=[pl.BlockSpec((tm, LANES), lambda i: (i, 0)), pl.BlockSpec((1, LANES), lambda i: (0, 0))],
        out_specs=pl.BlockSpec((1, TOP_K, tm), lambda i: (i, 0, 0)),
        out_shape=jax.ShapeDtypeStruct((t // tm, TOP_K, tm), jnp.int32),
        scratch_shapes=[pltpu.VMEM((1, LANES), F32)],
        compiler_params=pltpu.CompilerParams(dimension_semantics=("arbitrary",)),
        name="rank",
    )(route, counts)


def _slab_copy(src, src_row, dst, dst_row, sem):
    s0 = pl.multiple_of(src_row * SLAB_ROWS, SLAB_ROWS)
    d0 = pl.multiple_of(dst_row * SLAB_ROWS, SLAB_ROWS)
    return pltpu.make_async_copy(src.at[pl.ds(s0, SLAB_ROWS), :], dst.at[pl.ds(d0, SLAB_ROWS), :], sem)


DISPATCH_SLOTS = 3


def _dispatch_kernel(zrow_ref, dest_ref, hs_ref, xs_ref, buf_ref, zero_ref, lsem, ssem, zsem, *, n_steps):
    i = pl.program_id(0)
    rows = buf_ref.shape[1] // SLAB_ROWS
    zrows = zero_ref.shape[0]

    def zero_copy(e):
        start_row = pl.multiple_of(jnp.maximum(zrow_ref[e], 0) * SLAB_ROWS, zrows)
        return pltpu.make_async_copy(zero_ref, xs_ref.at[pl.ds(start_row, zrows), :], zsem)

    def load(tile, slot):
        src = hs_ref.at[pl.ds(pl.multiple_of(tile * rows * SLAB_ROWS, SLAB_ROWS), rows * SLAB_ROWS), :]
        return pltpu.make_async_copy(src, buf_ref.at[slot], lsem.at[slot])

    def drain(slot):
        for k in range(TOP_K):
            pltpu.make_async_copy(buf_ref.at[slot], xs_ref.at[pl.ds(0, rows * SLAB_ROWS), :], ssem.at[slot]).wait()

    @pl.when(i == 0)
    def _():
        zero_ref[...] = jnp.zeros_like(zero_ref)

        def start(e, carry):
            @pl.when(zrow_ref[e] >= 0)
            def _():
                zero_copy(e).start()
            return carry

        def wait(e, carry):
            @pl.when(zrow_ref[e] >= 0)
            def _():
                zero_copy(e).wait()
            return carry

        lax.fori_loop(0, zrow_ref.shape[0], start, 0)
        for ahead in range(min(DISPATCH_SLOTS - 1, n_steps)):
            load(ahead, ahead).start()
        lax.fori_loop(0, zrow_ref.shape[0], wait, 0)

    def step(slot):
        load(i, slot).wait()

        def start(r, carry):
            for k in range(TOP_K):
                _slab_copy(buf_ref.at[slot], r, xs_ref, dest_ref[0, k, r], ssem.at[slot]).start(priority=k)
            return carry

        lax.fori_loop(0, rows, start, 0, unroll=ROW_DMA_UNROLL)
        free = (slot + DISPATCH_SLOTS - 1) % DISPATCH_SLOTS

        @pl.when(i > 0)
        def _():
            drain(free)

        @pl.when(i + DISPATCH_SLOTS - 1 < n_steps)
        def _():
            load(i + DISPATCH_SLOTS - 1, free).start()

        @pl.when(i == n_steps - 1)
        def _():
            drain(slot)

    for slot in range(DISPATCH_SLOTS):
        pl.when(lax.rem(i, DISPATCH_SLOTS) == slot)(functools.partial(step, slot))


def _dispatch(hs, dest, zrow, n_rows):
    n, _, tm = dest.shape
    grid_spec = pltpu.PrefetchScalarGridSpec(
        num_scalar_prefetch=1,
        grid=(n,),
        in_specs=[
            pl.BlockSpec((1, TOP_K, tm), lambda i, z: (i, 0, 0), memory_space=pltpu.SMEM),
            pl.BlockSpec(memory_space=pl.ANY),
        ],
        out_specs=pl.BlockSpec(memory_space=pl.ANY),
        scratch_shapes=[pltpu.VMEM((DISPATCH_SLOTS, tm * SLAB_ROWS, LANES), jnp.uint32),
                        pltpu.VMEM((MOE_ROWS * SLAB_ROWS, LANES), jnp.uint32),
                        pltpu.SemaphoreType.DMA((DISPATCH_SLOTS,)), pltpu.SemaphoreType.DMA((DISPATCH_SLOTS,)),
                        pltpu.SemaphoreType.DMA],
    )
    return pl.pallas_call(
        functools.partial(_dispatch_kernel, n_steps=n),
        grid_spec=grid_spec,
        out_shape=jax.ShapeDtypeStruct((n_rows * SLAB_ROWS, LANES), jnp.uint32),
        compiler_params=pltpu.CompilerParams(dimension_semantics=("arbitrary",)),
        name="dispatch",
    )(zrow, dest, hs)


def _expert_kernel(be_ref, nx_ref, nu_ref, x_ref, w1_ref, w3_ref, w2_ref, y_ref,
                   w1s_ref, w3s_ref, w2s_ref, w1b_ref, w3b_ref, w2b_ref, wsem):
    i = pl.program_id(0)
    d = w1_ref.shape[1]
    rows = x_ref.shape[0] // SLAB_ROWS

    def fetch(e):
        return [pltpu.make_async_copy(src.at[e], dst, wsem.at[n])
                for n, (src, dst) in enumerate(((w1_ref, w1s_ref), (w3_ref, w3s_ref), (w2_ref, w2s_ref)))]

    @pl.when(i == 0)
    def _():
        for copy in fetch(be_ref[0]):
            copy.start()

    @pl.when((i == 0) | (be_ref[i] != be_ref[jnp.maximum(i - 1, 0)]))
    def _():
        for copy in fetch(be_ref[i]):
            copy.wait()
        for s in range(SLAB_ROWS):
            for half, src0 in enumerate((s * LANES, d // 2 + s * LANES)):
                dst0 = (2 * s + half) * LANES
                w1b_ref[dst0:dst0 + LANES, :] = w1s_ref[src0:src0 + LANES, :].astype(BF16)
                w3b_ref[dst0:dst0 + LANES, :] = w3s_ref[src0:src0 + LANES, :].astype(BF16)
        w2b_ref[...] = w2s_ref[...].astype(BF16)

        @pl.when(nx_ref[i] >= 0)
        def _():
            for copy in fetch(nx_ref[i]):
                copy.start()

    @pl.when(i < nu_ref[0])
    def _():
        sub = rows // EXPERT_SUBTILES
        for r0 in range(0, rows, sub):
            x = jnp.concatenate(
                [part.astype(BF16) for pair in _load_slabs(x_ref, sub, r0) for part in pair], axis=1)
            a = jnp.dot(x, w1b_ref[...], preferred_element_type=F32)
            b = jnp.dot(x, w3b_ref[...], preferred_element_type=F32)
            mid = (a * _sigmoid(a) * b).astype(BF16)
            _store_slabs(y_ref, r0, jnp.dot(mid, w2b_ref[...], preferred_element_type=F32))

    @pl.when(i >= nu_ref[0])
    def _():
        y_ref[...] = jnp.zeros_like(y_ref)


def _experts(xs, block_expert, next_expert, n_used, w1, w3, w2):
    _, d, de = w1.shape
    nb = xs.shape[0] // (MOE_ROWS * SLAB_ROWS)
    nbytes = MOE_ROWS * d * (2 + 2 + 2 + 4) + 3 * d * de * (2 + 1) + MOE_ROWS * de * 12
    grid_spec = pltpu.PrefetchScalarGridSpec(
        num_scalar_prefetch=3,
        grid=(nb,),
        in_specs=[
            pl.BlockSpec((MOE_ROWS * SLAB_ROWS, LANES), lambda i, be, nx, nu: (jnp.minimum(i, nu[0] - 1), 0)),
            pl.BlockSpec(memory_space=pl.ANY),
            pl.BlockSpec(memory_space=pl.ANY),
            pl.BlockSpec(memory_space=pl.ANY),
        ],
        out_specs=pl.BlockSpec((MOE_ROWS * SLAB_ROWS, LANES), lambda i, be, nx, nu: (i, 0)),
        scratch_shapes=[pltpu.VMEM((d, de), F32), pltpu.VMEM((d, de), F32), pltpu.VMEM((de, d), F32),
                        pltpu.VMEM((d, de), BF16), pltpu.VMEM((d, de), BF16), pltpu.VMEM((de, d), BF16),
                        pltpu.SemaphoreType.DMA((3,))],
    )
    return pl.pallas_call(
        _expert_kernel,
        grid_spec=grid_spec,
        out_shape=jax.ShapeDtypeStruct(xs.shape, jnp.uint32),
        compiler_params=pltpu.CompilerParams(
            dimension_semantics=("arbitrary",), vmem_limit_bytes=_vmem_limit(nbytes)),
        name="experts",
    )(block_expert, next_expert, n_used, xs, w1, w3, w2)


def _combine_kernel(dest_ref, dnext_ref, h_ref, r_ref, y_ref, g_ref, b_ref, o_ref, buf_ref, sem, *, n_steps):
    i = pl.program_id(0)
    rows = h_ref.shape[0]
    chunk = min(rows, COMBINE_CHUNK)

    def gather(dref, slot, r, k):
        return _slab_copy(y_ref, dref[0, k, r], buf_ref.at[slot, k], r, sem.at[slot])

    def wait(slot):
        for k in range(TOP_K):
            pltpu.make_async_copy(y_ref.at[pl.ds(0, rows * SLAB_ROWS), :], buf_ref.at[slot, k],
                                  sem.at[slot]).wait()

    @pl.when(i == 0)
    def _():
        def start(r, carry):
            for k in range(TOP_K):
                gather(dest_ref, 0, r, k).start(priority=k)
            return carry

        lax.fori_loop(0, rows, start, 0, unroll=ROW_DMA_UNROLL)

    def step(slot):
        wait(slot)
        for c0 in range(0, rows, chunk):
            route = r_ref[c0:c0 + chunk, :]
            slabs = [_load_slabs(buf_ref.at[slot, k], chunk, c0) for k in range(TOP_K)]
            lo, hi = [], []
            for s in range(SLAB_ROWS):
                for half, out in enumerate((lo, hi)):
                    out.append(sum(route[:, 2 + k:3 + k] * slabs[k][s][half] for k in range(TOP_K)))
            moe = jnp.concatenate(lo + hi, axis=1)
            o_ref[c0:c0 + chunk, :] = _layer_norm(
                DEEPNORM_ALPHA * h_ref[c0:c0 + chunk, :] + moe, g_ref[...], b_ref[...])
            for r in range(c0, c0 + chunk):
                for k in range(TOP_K):
                    gather(dnext_ref, 1 - slot, r, k).start(priority=k)

    for parity in range(2):
        pl.when(lax.rem(i, 2) == parity)(functools.partial(step, parity))

    @pl.when(i == n_steps - 1)
    def _():
        wait(n_steps % 2)


def _combine(h, route, dest, y, ln_g, ln_b):
    t, d = h.shape
    n, _, tm = dest.shape
    return pl.pallas_call(
        functools.partial(_combine_kernel, n_steps=n),
        grid=(n,),
        in_specs=[
            pl.BlockSpec((1, TOP_K, tm), lambda i: (i, 0, 0), memory_space=pltpu.SMEM),
            pl.BlockSpec((1, TOP_K, tm), lambda i: (jnp.minimum(i + 1, n - 1), 0, 0), memory_space=pltpu.SMEM),
            pl.BlockSpec((tm, d), lambda i: (i, 0)),
            pl.BlockSpec((tm, LANES), lambda i: (i, 0)),
            pl.BlockSpec(memory_space=pl.ANY),
            pl.BlockSpec((1, d), lambda i: (0, 0)),
            pl.BlockSpec((1, d), lambda i: (0, 0)),
        ],
        out_specs=pl.BlockSpec((tm, d), lambda i: (i, 0)),
        out_shape=jax.ShapeDtypeStruct((t, d), F32),
        scratch_shapes=[pltpu.VMEM((2, TOP_K, tm * SLAB_ROWS, LANES), jnp.uint32), pltpu.SemaphoreType.DMA((2,))],
        compiler_params=pltpu.CompilerParams(dimension_semantics=("arbitrary",)),
        name="combine",
    )(dest, dest, h, route, y, ln_g, ln_b)


def kernel(x, w_in, w_gate_a2, b_gate_a, sgu_ln_g, sgu_ln_b, sgu_w_s, sgu_b_s, gla_norm_g, w_branch_a, w_branch_b, w_merge, b_merge, w_out, ln1_g, ln1_b, w_router_group, b_router_group, w_router_expert, b_router_expert, w_exp_gate, w_exp_up, w_exp_down, ln2_g, ln2_b):
    batch, seq, d = x.shape
    t = batch * seq
    assert w_in.shape[0] == 1, "one layer"
    assert seq % SGU_CHUNK == 0 and seq % GLA_CHUNK == 0 and t % MOE_ROWS == 0
    sgu_width = sgu_ln_g.shape[1]
    key_dim = w_gate_a2.shape[2]
    val_dim = gla_norm_g.shape[1]
    rank = w_gate_a2.shape[1]
    xf = x.reshape(t, d)
    n_uv, n_h2 = 2 * sgu_width, 2 * key_dim + 2 * val_dim

    w_in_t = jnp.swapaxes(w_in.reshape(w_in.shape[1:]), 0, 1).astype(BF16)
    w_a_t = jnp.pad(w_in_t[n_uv + n_h2:, :], ((0, LANES - rank), (0, 0)))
    assert n_uv == 2 * sgu_w_s.shape[1] * SGU_CHUNK
    p, a_lr, s = _proj(xf, w_in_t, w_merge[0].astype(BF16), b_merge, w_a_t,
                       sgu_w_s[0], sgu_ln_g[0], sgu_ln_b[0], sgu_b_s[0], n_h2 - val_dim, val_dim, 1024)

    w_gate = jnp.pad(w_gate_a2[0].astype(BF16), ((0, LANES - rank), (0, 0)))
    o = _gla(p, 0, a_lr, w_gate, b_gate_a, gla_norm_g, batch, 1024)
    merged = _merge(s, o, w_branch_a[0].astype(BF16), w_branch_b[0].astype(BF16), p, n_h2, 1024, 1024)

    w_r = jnp.concatenate([w_router_group[0], w_router_expert[0]], axis=1)
    n_r = w_r.shape[1]
    w_r = jnp.pad(w_r, ((0, 0), (0, LANES - n_r))).astype(BF16)
    b_r = jnp.pad(jnp.concatenate([b_router_group, b_router_expert], axis=1), ((0, 0), (0, LANES - n_r)))
    h1, h1_slabs, route, counts = _out(merged, w_out[0].astype(BF16), xf, ln1_g, ln1_b, w_r, b_r, 512)

    assert t * TOP_K // MOE_ROWS <= 256, "per-expert block counts must stay exact in bf16"
    dest = _rank(route, counts, MOE_TOKEN_TILE)
    blocks_per_expert = (counts[0, :N_EXPERTS].astype(jnp.int32) + MOE_ROWS - 1) // MOE_ROWS
    block_ends = jnp.cumsum(blocks_per_expert)
    n_blocks = t * TOP_K // MOE_ROWS + N_EXPERTS
    n_used = block_ends[-1:]
    block_ids = jnp.minimum(jnp.arange(n_blocks, dtype=jnp.int32), n_used[0] - 1)
    experts = jnp.arange(N_EXPERTS, dtype=jnp.int32)
    block_expert = jnp.minimum(
        jnp.sum(block_ends[None, :] <= block_ids[:, None], axis=1), N_EXPERTS - 1).astype(jnp.int32)
    later = (experts[None, :] > block_expert[:, None]) & (blocks_per_expert[None, :] > 0)
    next_expert = jnp.min(jnp.where(later, experts[None, :], N_EXPERTS), axis=1)
    next_expert = jnp.where(next_expert < N_EXPERTS, next_expert, -1).astype(jnp.int32)
    tail_ids = n_used[0] + experts
    zrow = jnp.concatenate([
        jnp.where(blocks_per_expert > 0, (block_ends - 1) * MOE_ROWS, -1),
        jnp.where(tail_ids < n_blocks, tail_ids * MOE_ROWS, -1)]).astype(jnp.int32)

    xs = _dispatch(h1_slabs, dest, zrow, n_blocks * MOE_ROWS)
    y = _experts(xs, block_expert, next_expert, n_used.astype(jnp.int32),
                 w_exp_gate[0], w_exp_up[0], w_exp_down[0])
    out = _combine(h1, route, dest, y, ln2_g, ln2_b)
    return out.reshape(batch, seq, d)
```

```python
import functools

import jax
import jax.numpy as jnp
from jax import lax
from jax.experimental import pallas as pl
from jax.experimental.pallas import tpu as pltpu

F32 = jnp.float32
BF16 = jnp.bfloat16

SGU_CHUNK = 128
SGU_GROUPS = 8
GLA_HEADS = 4
GLA_CHUNK = 64
GLA_GATE_NORM = 16.0
GLA_HEADS_PER_STEP = 2
GLA_GROUP_CHUNKS = 4
N_GROUPS = 8
EXPERTS_PER_GROUP = 8
N_EXPERTS = N_GROUPS * EXPERTS_PER_GROUP
TOP_K = 2
LN_EPS = 1e-5
DEEPNORM_ALPHA = 2.0 ** 0.25

LANES = 128
MOE_ROWS = 256
MOE_TOKEN_TILE = 512
ROW_DMA_UNROLL = 8
PROJ_SUBTILES = 2
OUT_SUBTILES = 2
EXPERT_SUBTILES = 1
COMBINE_CHUNK = 32
V7X_VMEM_BYTES = 64 * 2 ** 20


def _vmem_limit(nbytes):
    return int(min(max(2 * nbytes, 16 * 2 ** 20), V7X_VMEM_BYTES - 8 * 2 ** 20))


def _layer_norm(y, g, b):
    mu = jnp.mean(y, axis=-1, keepdims=True)
    var = jnp.mean(jnp.square(y - mu), axis=-1, keepdims=True)
    return (y - mu) * lax.rsqrt(var + LN_EPS) * g + b


def _gelu(x):
    return 0.5 * x * (1.0 + lax.erf(x * (2.0 ** -0.5)))


def _sigmoid(x):
    return 0.5 * (jnp.tanh(0.5 * x) + 1.0)


SLAB_ROWS = 8
HIGH_HALF = 0xFFFF0000


def _bf16_bits(x):
    return lax.bitcast_convert_type(x.astype(BF16).astype(F32), jnp.uint32)


def _store_slabs(ref, row0, x):
    rows, d = x.shape
    assert d == 2 * SLAB_ROWS * LANES
    for s in range(SLAB_ROWS):
        lo = _bf16_bits(x[:, s * LANES:(s + 1) * LANES])
        hi = _bf16_bits(x[:, d // 2 + s * LANES:d // 2 + (s + 1) * LANES])
        word = jnp.right_shift(lo, jnp.uint32(16)) | (hi & jnp.uint32(HIGH_HALF))
        ref[pl.ds(row0 * SLAB_ROWS + s, rows, stride=SLAB_ROWS), :] = word


def _load_slabs(ref, rows, row0=0):
    out = []
    for s in range(SLAB_ROWS):
        word = ref[pl.ds(row0 * SLAB_ROWS + s, rows, stride=SLAB_ROWS), :]
        lo = lax.bitcast_convert_type(jnp.left_shift(word, jnp.uint32(16)), F32)
        hi = lax.bitcast_convert_type(word & jnp.uint32(HIGH_HALF), F32)
        out.append((lo, hi))
    return out


def _proj_kernel(x_ref, w_ref, wm_ref, bm_ref, wa_ref, ws_ref, lg_ref, lb_ref, bs_ref,
                 o_ref, a_ref, s_ref, xb_ref, u_ref, v_ref, *, plain_blocks, silu_blocks, n_blocks):
    j = pl.program_id(1)
    sgu_blocks = 2
    silu_start = sgu_blocks + plain_blocks
    sigmoid_start = silu_start + silu_blocks
    contract_last = (((1,), (1,)), ((), ()))
    tm, tn = o_ref.shape
    sub = tn // PROJ_SUBTILES
    c = SGU_CHUNK
    groups = tn // c
    assert sgu_blocks + groups <= n_blocks

    def x_dot_w(cols):
        return lax.dot_general(xb_ref[...], w_ref[cols, :], contract_last, preferred_element_type=F32)

    def x_dot_wm(cols):
        return jnp.dot(xb_ref[...], wm_ref[:, cols], preferred_element_type=F32)

    def sgu_group():
        g = jnp.clip(j - sgu_blocks, 0, groups - 1)
        cols = pl.ds(pl.multiple_of(g * c, c), c)
        row = lax.broadcasted_iota(jnp.int32, (c, c), 0)
        col = lax.broadcasted_iota(jnp.int32, (c, c), 1)
        w = jnp.where(row >= col, ws_ref[g], 0.0).astype(BF16)
        ln_g, ln_b, bias = lg_ref[g], lb_ref[g], bs_ref[g]
        for ci in range(tm // c):
            rows = slice(ci * c, (ci + 1) * c)
            vn = _layer_norm(v_ref[rows, cols].astype(F32), ln_g, ln_b)
            mixed = jnp.dot(w, vn.astype(BF16), preferred_element_type=F32) + bias
            s_ref[rows, cols] = (u_ref[rows, cols].astype(F32) * mixed).astype(s_ref.dtype)

    def block(store):
        for cols in (slice(si * sub, (si + 1) * sub) for si in range(PROJ_SUBTILES)):
            store(cols)

    def store_u(cols):
        u_ref[:, cols] = _gelu(x_dot_w(cols)).astype(u_ref.dtype)

    def store_v(cols):
        v_ref[:, cols] = _gelu(x_dot_w(cols)).astype(v_ref.dtype)

    def store_plain(cols):
        o_ref[:, cols] = x_dot_w(cols).astype(o_ref.dtype)

    def store_silu(cols):
        acc = x_dot_w(cols)
        o_ref[:, cols] = (acc * _sigmoid(acc)).astype(o_ref.dtype)

    def store_sigmoid(cols):
        o_ref[:, cols] = _sigmoid(x_dot_wm(cols) + bm_ref[:, cols]).astype(o_ref.dtype)

    @pl.when(j == 0)
    def _():
        xb_ref[...] = x_ref[...].astype(BF16)
        a_ref[...] = lax.dot_general(
            xb_ref[...], wa_ref[...], contract_last, preferred_element_type=F32).astype(a_ref.dtype)
        block(store_u)

    @pl.when(j == 1)
    def _():
        block(store_v)

    @pl.when((j >= sgu_blocks) & (j < silu_start))
    def _():
        block(store_plain)
        sgu_group()

    @pl.when((j >= silu_start) & (j < sigmoid_start))
    def _():
        block(store_silu)
        sgu_group()

    @pl.when(j >= sigmoid_start)
    def _():
        block(store_sigmoid)
        sgu_group()


def _proj(x, w_t, w_m, b_m, w_a_t, w_s, ln_g, ln_b, b_s, n_plain, n_silu, tm):
    m, k = x.shape
    ng, c, _ = w_s.shape
    width = tn = ng * c
    n_m = w_m.shape[1]
    tm = min(tm, m)
    assert n_plain % tn == 0 and n_silu % tn == 0 and n_m % tn == 0 and tm % c == 0
    in_blocks = 2 + (n_plain + n_silu) // tn
    p_block = lambda i, j: (i, jnp.maximum(j - 2, 0))
    nbytes = tm * k * (4 + 1) + 2 * k * tn * 2 + tm * tn * (2 + 2 + 2 + 1) + k * LANES * 2
    return pl.pallas_call(
        functools.partial(_proj_kernel, plain_blocks=n_plain // tn, silu_blocks=n_silu // tn,
                          n_blocks=in_blocks + n_m // tn),
        grid=(m // tm, in_blocks + n_m // tn),
        in_specs=[
            pl.BlockSpec((tm, k), lambda i, j: (i, 0)),
            pl.BlockSpec((tn, k), lambda i, j: (jnp.minimum(j, in_blocks - 1), 0)),
            pl.BlockSpec((k, tn), lambda i, j: (0, jnp.maximum(j - in_blocks, 0))),
            pl.BlockSpec((1, tn), lambda i, j: (0, jnp.maximum(j - in_blocks, 0))),
            pl.BlockSpec((LANES, k), lambda i, j: (0, 0)),
            pl.BlockSpec((ng, c, c), lambda i, j: (0, 0, 0)),
            pl.BlockSpec((ng, 1, c), lambda i, j: (0, 0, 0)),
            pl.BlockSpec((ng, 1, c), lambda i, j: (0, 0, 0)),
            pl.BlockSpec((ng, c, 1), lambda i, j: (0, 0, 0)),
        ],
        out_specs=[pl.BlockSpec((tm, tn), p_block), pl.BlockSpec((tm, LANES), lambda i, j: (i, 0)),
                   pl.BlockSpec((tm, width), lambda i, j: (i, 0))],
        out_shape=[jax.ShapeDtypeStruct((m, n_plain + n_silu + n_m), BF16),
                   jax.ShapeDtypeStruct((m, LANES), BF16), jax.ShapeDtypeStruct((m, width), BF16)],
        scratch_shapes=[pltpu.VMEM((tm, k), BF16), pltpu.VMEM((tm, width), BF16), pltpu.VMEM((tm, width), BF16)],
        compiler_params=pltpu.CompilerParams(
            dimension_semantics=("parallel", "arbitrary"), vmem_limit_bytes=_vmem_limit(nbytes)),
        name="proj",
    )(x, w_t, w_m, b_m, w_a_t, w_s, ln_g.reshape(ng, 1, c), ln_b.reshape(ng, 1, c), b_s.reshape(ng, c, 1))


def _gla_kernel(q_ref, k_ref, v_ref, g_ref, a_ref, wg_ref, bg_ref, ng_ref, o_ref, st_ref):
    c = GLA_CHUNK
    ts = q_ref.shape[0]
    heads, dv, dk = st_ref.shape
    grp = min(ts, GLA_GROUP_CHUNKS * c)
    ncg = grp // c
    contract_last = (((1,), (1,)), ((), ()))
    contract_first = (((0,), (0,)), ((), ()))

    @pl.when(pl.program_id(2) == 0)
    def _():
        st_ref[...] = jnp.zeros_like(st_ref)

    row = lax.broadcasted_iota(jnp.int32, (grp, grp), 0)
    col = lax.broadcasted_iota(jnp.int32, (grp, grp), 1)
    shift = c.bit_length() - 1
    ones_tril = ((row >= col) & (jnp.right_shift(row, shift) == jnp.right_shift(col, shift))).astype(BF16)
    qrow = lax.broadcasted_iota(jnp.int32, (c, grp), 0)
    kcol = lax.broadcasted_iota(jnp.int32, (c, grp), 1)
    visible = [(kcol < j * c) | ((kcol < (j + 1) * c) & (kcol - j * c <= qrow)) for j in range(ncg)]
    zero_keys = jnp.zeros((c, dk), BF16)

    for hd in range(heads):
        kcols = slice(hd * dk, (hd + 1) * dk)
        vcols = slice(hd * dv, (hd + 1) * dv)
        z = jnp.dot(a_ref[...], wg_ref[:, kcols], preferred_element_type=F32) + bg_ref[:, kcols]
        log_a = (jnp.minimum(z, 0.0) - jnp.log1p(jnp.exp(-jnp.abs(z)))) * (1.0 / GLA_GATE_NORM)
        la_hi = log_a.astype(BF16)
        la_lo = (log_a - la_hi.astype(F32)).astype(BF16)
        la_split = jnp.concatenate([la_hi, la_lo], axis=1)
        state_t = st_ref[hd]
        for gi in range(ts // grp):
            rows = slice(gi * grp, (gi + 1) * grp)
            r = jnp.dot(ones_tril, la_split[rows, :], preferred_element_type=F32)
            b = r[:, :dk] + r[:, dk:]
            q = q_ref[rows, kcols].astype(F32) * (dk ** -0.5)
            k = k_ref[rows, kcols].astype(F32)
            v = v_ref[rows, vcols]
            q_dec = q * jnp.exp(b)
            k_inv = (k * jnp.exp(-b)).astype(BF16)
            chunk = [slice(j * c, (j + 1) * c) for j in range(ncg)]
            b_last = [b[(j + 1) * c - 1:(j + 1) * c, :] for j in range(ncg)]
            before = [jnp.zeros_like(b_last[0])]
            for j in range(ncg):
                before.append(before[j] + b_last[j])
            k_to_end = [k[chunk[j], :] * jnp.exp(b_last[j] - b[chunk[j], :]) for j in range(ncg)]

            q_grp = jnp.concatenate(
                [q_dec[chunk[j], :] * jnp.exp(before[j]) for j in range(ncg)], axis=0).astype(BF16)
            o = lax.dot_general(q_grp, state_t.astype(BF16), contract_last, preferred_element_type=F32)

            q_dec = q_dec.astype(BF16)
            attn = []
            for j in range(ncg):
                keys = [(k_to_end[i] * jnp.exp(before[j] - before[i + 1])).astype(BF16) for i in range(j)]
                keys += [k_inv[chunk[j], :]] + [zero_keys] * (ncg - 1 - j)
                keys = jnp.concatenate(keys, axis=0) if ncg > 1 else keys[0]
                scores = lax.dot_general(q_dec[chunk[j], :], keys, contract_last, preferred_element_type=F32)
                attn.append(jnp.where(visible[j], scores, 0.0).astype(BF16))
            attn = jnp.concatenate(attn, axis=0) if ncg > 1 else attn[0]
            o = o + jnp.dot(attn, v, preferred_element_type=F32)

            k_grp = jnp.concatenate(
                [(k_to_end[j] * jnp.exp(before[ncg] - before[j + 1])).astype(BF16) for j in range(ncg)], axis=0)
            state_t = state_t * jnp.exp(before[ncg]) + lax.dot_general(
                v, k_grp, contract_first, preferred_element_type=F32)

            o = o * lax.rsqrt(jnp.mean(jnp.square(o), axis=-1, keepdims=True) + LN_EPS) * ng_ref[:, vcols]
            o_ref[rows, vcols] = (o * g_ref[rows, vcols].astype(F32)).astype(o_ref.dtype)
        st_ref[hd] = state_t


def _gla(h2, col0, a_lr, w_gate, b_gate, norm_g, batch, ts):
    t = h2.shape[0]
    seq = t // batch
    nh = GLA_HEADS
    key_dim = w_gate.shape[1]
    dk = key_dim // nh
    val_dim = norm_g.shape[1]
    dv = val_dim // nh
    ts = min(ts, seq)
    ns = seq // ts
    hps = GLA_HEADS_PER_STEP
    wk, wv = hps * dk, hps * dv
    assert nh % hps == 0 and col0 % wk == 0 and (col0 + 2 * key_dim) % wv == 0
    qb = col0 // wk
    kb, vb, gb = qb + key_dim // wk, (col0 + 2 * key_dim) // wv, (col0 + 2 * key_dim + val_dim) // wv
    tok = lambda b, h, s: b * ns + s
    return pl.pallas_call(
        _gla_kernel,
        grid=(batch, nh // hps, ns),
        in_specs=[
            pl.BlockSpec((ts, wk), lambda b, h, s: (tok(b, h, s), qb + h)),
            pl.BlockSpec((ts, wk), lambda b, h, s: (tok(b, h, s), kb + h)),
            pl.BlockSpec((ts, wv), lambda b, h, s: (tok(b, h, s), vb + h)),
            pl.BlockSpec((ts, wv), lambda b, h, s: (tok(b, h, s), gb + h)),
            pl.BlockSpec((ts, LANES), lambda b, h, s: (tok(b, h, s), 0)),
            pl.BlockSpec((LANES, wk), lambda b, h, s: (0, h)),
            pl.BlockSpec((1, wk), lambda b, h, s: (0, h)),
            pl.BlockSpec((1, wv), lambda b, h, s: (0, h)),
        ],
        out_specs=pl.BlockSpec((ts, wv), lambda b, h, s: (tok(b, h, s), h)),
        out_shape=jax.ShapeDtypeStruct((t, val_dim), BF16),
        scratch_shapes=[pltpu.VMEM((hps, dv, dk), F32)],
        compiler_params=pltpu.CompilerParams(
            dimension_semantics=("parallel", "parallel", "arbitrary"),
            vmem_limit_bytes=_vmem_limit(hps * ts * (dk * 48 + dv * 16))),
        name="gla",
    )(h2, h2, h2, h2, a_lr, w_gate, b_gate, norm_g)


def _merge_kernel(s_ref, o_ref, wa_ref, wb_ref, ga_ref, gb_ref, out_ref):
    half = out_ref.shape[1] // 2
    for cols in (slice(0, half), slice(half, 2 * half)):
        ya = jnp.dot(s_ref[...], wa_ref[:, cols], preferred_element_type=F32)
        yb = jnp.dot(o_ref[...], wb_ref[:, cols], preferred_element_type=F32)
        out_ref[:, cols] = (ga_ref[:, cols].astype(F32) * ya
                            + gb_ref[:, cols].astype(F32) * yb).astype(out_ref.dtype)


def _merge(s, o, w_a, w_b, gates, gate_col0, tm, tn):
    t, ka = s.shape
    kb = o.shape[1]
    d = w_a.shape[1]
    tm, tn = min(tm, t), min(tn, d)
    gj = gate_col0 // tn
    nbytes = tm * (ka + kb) * 2 + (ka + kb) * tn * 2 + tm * tn * (2 * 2 + 2 + 8)
    return pl.pallas_call(
        _merge_kernel,
        grid=(t // tm, d // tn),
        in_specs=[
            pl.BlockSpec((tm, ka), lambda i, j: (i, 0)),
            pl.BlockSpec((tm, kb), lambda i, j: (i, 0)),
            pl.BlockSpec((ka, tn), lambda i, j: (0, j)),
            pl.BlockSpec((kb, tn), lambda i, j: (0, j)),
            pl.BlockSpec((tm, tn), lambda i, j: (i, gj + j)),
            pl.BlockSpec((tm, tn), lambda i, j: (i, gj + j + d // tn)),
        ],
        out_specs=pl.BlockSpec((tm, tn), lambda i, j: (i, j)),
        out_shape=jax.ShapeDtypeStruct((t, d), BF16),
        compiler_params=pltpu.CompilerParams(
            dimension_semantics=("parallel", "arbitrary"), vmem_limit_bytes=_vmem_limit(nbytes)),
        name="merge",
    )(s, o, w_a, w_b, gates, gates)


def _route(logits):
    lane = lax.broadcasted_iota(jnp.int32, logits.shape, 1).astype(F32)
    neg = float("-inf")
    big = float(LANES)
    gl = jnp.where(lane < N_GROUPS, logits, neg)
    gmax = jnp.max(gl, axis=1, keepdims=True)
    gidx = jnp.min(jnp.where(gl == gmax, lane, big), axis=1, keepdims=True)
    p_group = 1.0 / jnp.sum(jnp.exp(gl - gmax), axis=1, keepdims=True)
    lo = N_GROUPS + EXPERTS_PER_GROUP * gidx
    el = jnp.where((lane >= lo) & (lane < lo + EXPERTS_PER_GROUP), logits, neg)
    v1 = jnp.max(el, axis=1, keepdims=True)
    i1 = jnp.min(jnp.where(el == v1, lane, big), axis=1, keepdims=True)
    el2 = jnp.where(lane == i1, neg, el)
    v2 = jnp.max(el2, axis=1, keepdims=True)
    i2 = jnp.min(jnp.where(el2 == v2, lane, big), axis=1, keepdims=True)
    t = jnp.exp(v2 - v1)
    w1 = p_group / (1.0 + t)
    w2 = p_group * t / (1.0 + t)
    return jnp.where(lane == 0, i1 - N_GROUPS,
                     jnp.where(lane == 1, i2 - N_GROUPS,
                               jnp.where(lane == 2, w1, jnp.where(lane == 3, w2, 0.0))))


def _expert_hits(route):
    lane = lax.broadcasted_iota(jnp.int32, route.shape, 1).astype(F32)
    return [lane == route[:, k:k + 1] for k in range(TOP_K)]


def _out_kernel(m_ref, w_ref, x_ref, g_ref, b_ref, wr_ref, br_ref, h_ref, hs_ref, r_ref, cnt_ref, mix_ref):
    i = pl.program_id(0)

    @pl.when(i == 0)
    def _():
        cnt_ref[...] = jnp.zeros_like(cnt_ref)
        mix_ref[1] = jnp.zeros(mix_ref.shape[1:], F32)

    has_prev = (i > 0).astype(F32)
    sub = m_ref.shape[0] // OUT_SUBTILES

    def step(cur):
        for si in range(OUT_SUBTILES):
            rows = slice(si * sub, (si + 1) * sub)
            mix_ref[cur, rows, :] = jnp.dot(m_ref[rows, :], w_ref[...], preferred_element_type=F32)
            h = _layer_norm(DEEPNORM_ALPHA * x_ref[rows, :] + mix_ref[1 - cur, rows, :], g_ref[...], b_ref[...])
            h_ref[rows, :] = h
            _store_slabs(hs_ref, si * sub, h)
            logits = jnp.dot(h.astype(BF16), wr_ref[...], preferred_element_type=F32) + br_ref[...]
            route = _route(logits)
            r_ref[rows, :] = route
            cnt_ref[...] += has_prev * sum(
                jnp.sum(hit.astype(F32), axis=0, keepdims=True) for hit in _expert_hits(route))

    for parity in range(2):
        pl.when(lax.rem(i, 2) == parity)(functools.partial(step, parity))


def _out(merged, w_out, x, ln_g, ln_b, w_r, b_r, tm):
    t, d = x.shape
    tm = min(tm, t)
    n = t // tm
    nbytes = d * d * 2 + tm * d * (2 + 4 + 4 + 2 + 8 + 8) + d * LANES * 2
    cur = lambda i: (jnp.minimum(i, n - 1), 0)
    prev = lambda i: (jnp.maximum(i - 1, 0), 0)
    return pl.pallas_call(
        _out_kernel,
        grid=(n + 1,),
        in_specs=[
            pl.BlockSpec((tm, d), cur),
            pl.BlockSpec((d, d), lambda i: (0, 0)),
            pl.BlockSpec((tm, d), prev),
            pl.BlockSpec((1, d), lambda i: (0, 0)),
            pl.BlockSpec((1, d), lambda i: (0, 0)),
            pl.BlockSpec((d, LANES), lambda i: (0, 0)),
            pl.BlockSpec((1, LANES), lambda i: (0, 0)),
        ],
        out_specs=[pl.BlockSpec((tm, d), prev),
                   pl.BlockSpec((tm * SLAB_ROWS, LANES), prev),
                   pl.BlockSpec((tm, LANES), prev),
                   pl.BlockSpec((1, LANES), lambda i: (0, 0))],
        out_shape=[jax.ShapeDtypeStruct((t, d), F32),
                   jax.ShapeDtypeStruct((t * SLAB_ROWS, LANES), jnp.uint32),
                   jax.ShapeDtypeStruct((t, LANES), F32),
                   jax.ShapeDtypeStruct((1, LANES), F32)],
        scratch_shapes=[pltpu.VMEM((2, tm, d), F32)],
        compiler_params=pltpu.CompilerParams(
            dimension_semantics=("arbitrary",), vmem_limit_bytes=_vmem_limit(nbytes)),
        name="out_ln_route",
    )(merged, w_out, x, ln_g, ln_b, w_r, b_r)


def _rank_kernel(r_ref, cnt_ref, dest_ref, next_ref):
    rows = r_ref.shape[0]

    @pl.when(pl.program_id(0) == 0)
    def _():
        blocks = jnp.floor((cnt_ref[...] + (MOE_ROWS - 1)) * (1.0 / MOE_ROWS))
        k = lax.broadcasted_iota(jnp.int32, (LANES, LANES), 0)
        e = lax.broadcasted_iota(jnp.int32, (LANES, LANES), 1)
        blocks8 = jnp.broadcast_to(blocks, (8, LANES)).astype(BF16)
        first_block = jnp.dot(blocks8, (k < e).astype(BF16), preferred_element_type=F32)
        next_ref[...] = first_block[0:1, :] * MOE_ROWS

    hits = _expert_hits(r_ref[...])
    cnt = sum(hit.astype(F32) for hit in hits)
    row = lax.broadcasted_iota(jnp.int32, (rows, rows), 0)
    col = lax.broadcasted_iota(jnp.int32, (rows, rows), 1)
    earlier = (row > col).astype(BF16)
    slot = jnp.dot(earlier, cnt.astype(BF16), preferred_element_type=F32) + next_ref[...]
    dest = [jnp.sum(jnp.where(hit, slot, 0.0), axis=1, keepdims=True) for hit in hits]
    lane = lax.broadcasted_iota(jnp.int32, (rows, LANES), 1)
    by_token = jnp.where(lane == 0, dest[0], jnp.where(lane == 1, dest[1], 0.0))
    dest_ref[0] = jnp.transpose(by_token)[:TOP_K, :].astype(jnp.int32)
    next_ref[...] += jnp.sum(cnt, axis=0, keepdims=True)


def _rank(route, counts, tm):
    t = route.shape[0]
    tm = min(tm, t)
    return pl.pallas_call(
        _rank_kernel,
        grid=(t // tm,),
        in_specs=[pl.BlockSpec((tm, LANES), lambda i: (i, 0)), pl.BlockSpec((1, LANES), lambda i: (0, 0))],
        out_specs=pl.BlockSpec((1, TOP_K, tm), lambda i: (i, 0, 0)),
        out_shape=jax.ShapeDtypeStruct((t // tm, TOP_K, tm), jnp.int32),
        scratch_shapes=[pltpu.VMEM((1, LANES), F32)],
        compiler_params=pltpu.CompilerParams(dimension_semantics=("arbitrary",)),
        name="rank",
    )(route, counts)


def _slab_copy(src, src_row, dst, dst_row, sem):
    s0 = pl.multiple_of(src_row * SLAB_ROWS, SLAB_ROWS)
    d0 = pl.multiple_of(dst_row * SLAB_ROWS, SLAB_ROWS)
    return pltpu.make_async_copy(src.at[pl.ds(s0, SLAB_ROWS), :], dst.at[pl.ds(d0, SLAB_ROWS), :], sem)


DISPATCH_SLOTS = 3


def _dispatch_kernel(zrow_ref, dest_ref, hs_ref, xs_ref, buf_ref, zero_ref, lsem, ssem, zsem, *, n_steps):
    i = pl.program_id(0)
    rows = buf_ref.shape[1] // SLAB_ROWS
    zrows = zero_ref.shape[0]

    def zero_copy(e):
        start_row = pl.multiple_of(jnp.maximum(zrow_ref[e], 0) * SLAB_ROWS, zrows)
        return pltpu.make_async_copy(zero_ref, xs_ref.at[pl.ds(start_row, zrows), :], zsem)

    def load(tile, slot):
        src = hs_ref.at[pl.ds(pl.multiple_of(tile * rows * SLAB_ROWS, SLAB_ROWS), rows * SLAB_ROWS), :]
        return pltpu.make_async_copy(src, buf_ref.at[slot], lsem.at[slot])

    def drain(slot):
        for k in range(TOP_K):
            pltpu.make_async_copy(buf_ref.at[slot], xs_ref.at[pl.ds(0, rows * SLAB_ROWS), :], ssem.at[slot]).wait()

    @pl.when(i == 0)
    def _():
        zero_ref[...] = jnp.zeros_like(zero_ref)

        def start(e, carry):
            @pl.when(zrow_ref[e] >= 0)
            def _():
                zero_copy(e).start()
            return carry

        def wait(e, carry):
            @pl.when(zrow_ref[e] >= 0)
            def _():
                zero_copy(e).wait()
            return carry

        lax.fori_loop(0, zrow_ref.shape[0], start, 0)
        for ahead in range(min(DISPATCH_SLOTS - 1, n_steps)):
            load(ahead, ahead).start()
        lax.fori_loop(0, zrow_ref.shape[0], wait, 0)

    def step(slot):
        load(i, slot).wait()

        def start(r, carry):
            for k in range(TOP_K):
                _slab_copy(buf_ref.at[slot], r, xs_ref, dest_ref[0, k, r], ssem.at[slot]).start(priority=k)
            return carry

        lax.fori_loop(0, rows, start, 0, unroll=ROW_DMA_UNROLL)
        free = (slot + DISPATCH_SLOTS - 1) % DISPATCH_SLOTS

        @pl.when(i > 0)
        def _():
            drain(free)

        @pl.when(i + DISPATCH_SLOTS - 1 < n_steps)
        def _():
            load(i + DISPATCH_SLOTS - 1, free).start()

        @pl.when(i == n_steps - 1)
        def _():
            drain(slot)

    for slot in range(DISPATCH_SLOTS):
        pl.when(lax.rem(i, DISPATCH_SLOTS) == slot)(functools.partial(step, slot))


def _dispatch(hs, dest, zrow, n_rows):
    n, _, tm = dest.shape
    grid_spec = pltpu.PrefetchScalarGridSpec(
        num_scalar_prefetch=1,
        grid=(n,),
        in_specs=[
            pl.BlockSpec((1, TOP_K, tm), lambda i, z: (i, 0, 0), memory_space=pltpu.SMEM),
            pl.BlockSpec(memory_space=pl.ANY),
        ],
        out_specs=pl.BlockSpec(memory_space=pl.ANY),
        scratch_shapes=[pltpu.VMEM((DISPATCH_SLOTS, tm * SLAB_ROWS, LANES), jnp.uint32),
                        pltpu.VMEM((MOE_ROWS * SLAB_ROWS, LANES), jnp.uint32),
                        pltpu.SemaphoreType.DMA((DISPATCH_SLOTS,)), pltpu.SemaphoreType.DMA((DISPATCH_SLOTS,)),
                        pltpu.SemaphoreType.DMA],
    )
    return pl.pallas_call(
        functools.partial(_dispatch_kernel, n_steps=n),
        grid_spec=grid_spec,
        out_shape=jax.ShapeDtypeStruct((n_rows * SLAB_ROWS, LANES), jnp.uint32),
        compiler_params=pltpu.CompilerParams(dimension_semantics=("arbitrary",)),
        name="dispatch",
    )(zrow, dest, hs)


def _expert_kernel(be_ref, nx_ref, nu_ref, x_ref, w1_ref, w3_ref, w2_ref, y_ref,
                   w1s_ref, w3s_ref, w2s_ref, w1b_ref, w3b_ref, w2b_ref, wsem):
    i = pl.program_id(0)
    d = w1_ref.shape[1]
    rows = x_ref.shape[0] // SLAB_ROWS

    def fetch(e):
        return [pltpu.make_async_copy(src.at[e], dst, wsem.at[n])
                for n, (src, dst) in enumerate(((w1_ref, w1s_ref), (w3_ref, w3s_ref), (w2_ref, w2s_ref)))]

    @pl.when(i == 0)
    def _():
        for copy in fetch(be_ref[0]):
            copy.start()

    @pl.when((i == 0) | (be_ref[i] != be_ref[jnp.maximum(i - 1, 0)]))
    def _():
        for copy in fetch(be_ref[i]):
            copy.wait()
        for s in range(SLAB_ROWS):
            for half, src0 in enumerate((s * LANES, d // 2 + s * LANES)):
                dst0 = (2 * s + half) * LANES
                w1b_ref[dst0:dst0 + LANES, :] = w1s_ref[src0:src0 + LANES, :].astype(BF16)
                w3b_ref[dst0:dst0 + LANES, :] = w3s_ref[src0:src0 + LANES, :].astype(BF16)
        w2b_ref[...] = w2s_ref[...].astype(BF16)

        @pl.when(nx_ref[i] >= 0)
        def _():
            for copy in fetch(nx_ref[i]):
                copy.start()

    @pl.when(i < nu_ref[0])
    def _():
        sub = rows // EXPERT_SUBTILES
        for r0 in range(0, rows, sub):
            x = jnp.concatenate(
                [part.astype(BF16) for pair in _load_slabs(x_ref, sub, r0) for part in pair], axis=1)
            a = jnp.dot(x, w1b_ref[...], preferred_element_type=F32)
            b = jnp.dot(x, w3b_ref[...], preferred_element_type=F32)
            mid = (a * _sigmoid(a) * b).astype(BF16)
            _store_slabs(y_ref, r0, jnp.dot(mid, w2b_ref[...], preferred_element_type=F32))

    @pl.when(i >= nu_ref[0])
    def _():
        y_ref[...] = jnp.zeros_like(y_ref)


def _experts(xs, block_expert, next_expert, n_used, w1, w3, w2):
    _, d, de = w1.shape
    nb = xs.shape[0] // (MOE_ROWS * SLAB_ROWS)
    nbytes = MOE_ROWS * d * (2 + 2 + 2 + 4) + 3 * d * de * (2 + 1) + MOE_ROWS * de * 12
    grid_spec = pltpu.PrefetchScalarGridSpec(
        num_scalar_prefetch=3,
        grid=(nb,),
        in_specs=[
            pl.BlockSpec((MOE_ROWS * SLAB_ROWS, LANES), lambda i, be, nx, nu: (jnp.minimum(i, nu[0] - 1), 0)),
            pl.BlockSpec(memory_space=pl.ANY),
            pl.BlockSpec(memory_space=pl.ANY),
            pl.BlockSpec(memory_space=pl.ANY),
        ],
        out_specs=pl.BlockSpec((MOE_ROWS * SLAB_ROWS, LANES), lambda i, be, nx, nu: (i, 0)),
        scratch_shapes=[pltpu.VMEM((d, de), F32), pltpu.VMEM((d, de), F32), pltpu.VMEM((de, d), F32),
                        pltpu.VMEM((d, de), BF16), pltpu.VMEM((d, de), BF16), pltpu.VMEM((de, d), BF16),
                        pltpu.SemaphoreType.DMA((3,))],
    )
    return pl.pallas_call(
        _expert_kernel,
        grid_spec=grid_spec,
        out_shape=jax.ShapeDtypeStruct(xs.shape, jnp.uint32),
        compiler_params=pltpu.CompilerParams(
            dimension_semantics=("arbitrary",), vmem_limit_bytes=_vmem_limit(nbytes)),
        name="experts",
    )(block_expert, next_expert, n_used, xs, w1, w3, w2)


def _combine_kernel(dest_ref, dnext_ref, h_ref, r_ref, y_ref, g_ref, b_ref, o_ref, buf_ref, sem, *, n_steps):
    i = pl.program_id(0)
    rows = h_ref.shape[0]
    chunk = min(rows, COMBINE_CHUNK)

    def gather(dref, slot, r, k):
        return _slab_copy(y_ref, dref[0, k, r], buf_ref.at[slot, k], r, sem.at[slot])

    def wait(slot):
        for k in range(TOP_K):
            pltpu.make_async_copy(y_ref.at[pl.ds(0, rows * SLAB_ROWS), :], buf_ref.at[slot, k],
                                  sem.at[slot]).wait()

    @pl.when(i == 0)
    def _():
        def start(r, carry):
            for k in range(TOP_K):
                gather(dest_ref, 0, r, k).start(priority=k)
            return carry

        lax.fori_loop(0, rows, start, 0, unroll=ROW_DMA_UNROLL)

    def step(slot):
        wait(slot)
        for c0 in range(0, rows, chunk):
            route = r_ref[c0:c0 + chunk, :]
            slabs = [_load_slabs(buf_ref.at[slot, k], chunk, c0) for k in range(TOP_K)]
            lo, hi = [], []
            for s in range(SLAB_ROWS):
                for half, out in enumerate((lo, hi)):
                    out.append(sum(route[:, 2 + k:3 + k] * slabs[k][s][half] for k in range(TOP_K)))
            moe = jnp.concatenate(lo + hi, axis=1)
            o_ref[c0:c0 + chunk, :] = _layer_norm(
                DEEPNORM_ALPHA * h_ref[c0:c0 + chunk, :] + moe, g_ref[...], b_ref[...])
            for r in range(c0, c0 + chunk):
                for k in range(TOP_K):
                    gather(dnext_ref, 1 - slot, r, k).start(priority=k)

    for parity in range(2):
        pl.when(lax.rem(i, 2) == parity)(functools.partial(step, parity))

    @pl.when(i == n_steps - 1)
    def _():
        wait(n_steps % 2)


def _combine(h, route, dest, y, ln_g, ln_b):
    t, d = h.shape
    n, _, tm = dest.shape
    return pl.pallas_call(
        functools.partial(_combine_kernel, n_steps=n),
        grid=(n,),
        in_specs=[
            pl.BlockSpec((1, TOP_K, tm), lambda i: (i, 0, 0), memory_space=pltpu.SMEM),
            pl.BlockSpec((1, TOP_K, tm), lambda i: (jnp.minimum(i + 1, n - 1), 0, 0), memory_space=pltpu.SMEM),
            pl.BlockSpec((tm, d), lambda i: (i, 0)),
            pl.BlockSpec((tm, LANES), lambda i: (i, 0)),
            pl.BlockSpec(memory_space=pl.ANY),
            pl.BlockSpec((1, d), lambda i: (0, 0)),
            pl.BlockSpec((1, d), lambda i: (0, 0)),
        ],
        out_specs=pl.BlockSpec((tm, d), lambda i: (i, 0)),
        out_shape=jax.ShapeDtypeStruct((t, d), F32),
        scratch_shapes=[pltpu.VMEM((2, TOP_K, tm * SLAB_ROWS, LANES), jnp.uint32), pltpu.SemaphoreType.DMA((2,))],
        compiler_params=pltpu.CompilerParams(dimension_semantics=("arbitrary",)),
        name="combine",
    )(dest, dest, h, route, y, ln_g, ln_b)


def kernel(x, w_in, w_gate_a2, b_gate_a, sgu_ln_g, sgu_ln_b, sgu_w_s, sgu_b_s, gla_norm_g, w_branch_a, w_branch_b, w_merge, b_merge, w_out, ln1_g, ln1_b, w_router_group, b_router_group, w_router_expert, b_router_expert, w_exp_gate, w_exp_up, w_exp_down, ln2_g, ln2_b):
    batch, seq, d = x.shape
    t = batch * seq
    assert w_in.shape[0] == 1, "one layer"
    assert seq % SGU_CHUNK == 0 and seq % GLA_CHUNK == 0 and t % MOE_ROWS == 0
    sgu_width = sgu_ln_g.shape[1]
    key_dim = w_gate_a2.shape[2]
    val_dim = gla_norm_g.shape[1]
    rank = w_gate_a2.shape[1]
    xf = x.reshape(t, d)
    n_uv, n_h2 = 2 * sgu_width, 2 * key_dim + 2 * val_dim

    w_in_t = jnp.swapaxes(w_in.reshape(w_in.shape[1:]), 0, 1).astype(BF16)
    w_a_t = jnp.pad(w_in_t[n_uv + n_h2:, :], ((0, LANES - rank), (0, 0)))
    assert n_uv == 2 * sgu_w_s.shape[1] * SGU_CHUNK
    p, a_lr, s = _proj(xf, w_in_t, w_merge[0].astype(BF16), b_merge, w_a_t,
                       sgu_w_s[0], sgu_ln_g[0], sgu_ln_b[0], sgu_b_s[0], n_h2 - val_dim, val_dim, 1024)

    w_gate = jnp.pad(w_gate_a2[0].astype(BF16), ((0, LANES - rank), (0, 0)))
    o = _gla(p, 0, a_lr, w_gate, b_gate_a, gla_norm_g, batch, 1024)
    merged = _merge(s, o, w_branch_a[0].astype(BF16), w_branch_b[0].astype(BF16), p, n_h2, 1024, 1024)

    w_r = jnp.concatenate([w_router_group[0], w_router_expert[0]], axis=1)
    n_r = w_r.shape[1]
    w_r = jnp.pad(w_r, ((0, 0), (0, LANES - n_r))).astype(BF16)
    b_r = jnp.pad(jnp.concatenate([b_router_group, b_router_expert], axis=1), ((0, 0), (0, LANES - n_r)))
    h1, h1_slabs, route, counts = _out(merged, w_out[0].astype(BF16), xf, ln1_g, ln1_b, w_r, b_r, 512)

    assert t * TOP_K // MOE_ROWS <= 256, "per-expert block counts must stay exact in bf16"
    dest = _rank(route, counts, MOE_TOKEN_TILE)
    blocks_per_expert = (counts[0, :N_EXPERTS].astype(jnp.int32) + MOE_ROWS - 1) // MOE_ROWS
    block_ends = jnp.cumsum(blocks_per_expert)
    n_blocks = t * TOP_K // MOE_ROWS + N_EXPERTS
    n_used = block_ends[-1:]
    block_ids = jnp.minimum(jnp.arange(n_blocks, dtype=jnp.int32), n_used[0] - 1)
    experts = jnp.arange(N_EXPERTS, dtype=jnp.int32)
    block_expert = jnp.minimum(
        jnp.sum(block_ends[None, :] <= block_ids[:, None], axis=1), N_EXPERTS - 1).astype(jnp.int32)
    later = (experts[None, :] > block_expert[:, None]) & (blocks_per_expert[None, :] > 0)
    next_expert = jnp.min(jnp.where(later, experts[None, :], N_EXPERTS), axis=1)
    next_expert = jnp.where(next_expert < N_EXPERTS, next_expert, -1).astype(jnp.int32)
    tail_ids = n_used[0] + experts
    zrow = jnp.concatenate([
        jnp.where(blocks_per_expert > 0, (block_ends - 1) * MOE_ROWS, -1),
        jnp.where(tail_ids < n_blocks, tail_ids * MOE_ROWS, -1)]).astype(jnp.int32)

    xs = _dispatch(h1_slabs, dest, zrow, n_blocks * MOE_ROWS)
    y = _experts(xs, block_expert, next_expert, n_used.astype(jnp.int32),
                 w_exp_gate[0], w_exp_up[0], w_exp_down[0])
    out = _combine(h1, route, dest, y, ln2_g, ln2_b)
    return out.reshape(batch, seq, d)
```

```python
import functools

import jax
import jax.numpy as jnp
from jax import lax
from jax.experimental import pallas as pl
from jax.experimental.pallas import tpu as pltpu

F32 = jnp.float32
BF16 = jnp.bfloat16

SGU_CHUNK = 128
GLA_HEADS = 4
GLA_CHUNK = 64
GLA_GATE_NORM = 16.0
GLA_HEADS_PER_STEP = 2
GLA_GROUP_CHUNKS = 4
N_GROUPS = 8
EXPERTS_PER_GROUP = 8
N_EXPERTS = N_GROUPS * EXPERTS_PER_GROUP
TOP_K = 2
LN_EPS = 1e-5
DEEPNORM_ALPHA = 2.0 ** 0.25

LANES = 128
MOE_ROWS = 256
MOE_TOKEN_TILE = 512
ROW_DMA_UNROLL = 8
DISPATCH_SLOTS = 3
PROJ_SUBTILES = 2
OUT_SUBTILES = 2
COMBINE_CHUNK = 32
V7X_VMEM_BYTES = 64 * 2 ** 20


def _vmem_limit(nbytes):
    return int(min(max(2 * nbytes, 16 * 2 ** 20), V7X_VMEM_BYTES - 8 * 2 ** 20))


def _layer_norm(y, g, b):
    mu = jnp.mean(y, axis=-1, keepdims=True)
    var = jnp.mean(jnp.square(y - mu), axis=-1, keepdims=True)
    return (y - mu) * lax.rsqrt(var + LN_EPS) * g + b


def _gelu(x):
    return 0.5 * x * (1.0 + lax.erf(x * (2.0 ** -0.5)))


def _sigmoid(x):
    return 0.5 * (jnp.tanh(0.5 * x) + 1.0)


SLAB_ROWS = 8
HIGH_HALF = 0xFFFF0000


def _bf16_bits(x):
    return lax.bitcast_convert_type(x.astype(BF16).astype(F32), jnp.uint32)


def _store_slabs(ref, row0, x):
    rows, d = x.shape
    assert d == 2 * SLAB_ROWS * LANES
    for s in range(SLAB_ROWS):
        lo = _bf16_bits(x[:, s * LANES:(s + 1) * LANES])
        hi = _bf16_bits(x[:, d // 2 + s * LANES:d // 2 + (s + 1) * LANES])
        word = jnp.right_shift(lo, jnp.uint32(16)) | (hi & jnp.uint32(HIGH_HALF))
        ref[pl.ds(row0 * SLAB_ROWS + s, rows, stride=SLAB_ROWS), :] = word


def _load_slabs(ref, rows, row0=0):
    out = []
    for s in range(SLAB_ROWS):
        word = ref[pl.ds(row0 * SLAB_ROWS + s, rows, stride=SLAB_ROWS), :]
        lo = lax.bitcast_convert_type(jnp.left_shift(word, jnp.uint32(16)), F32)
        hi = lax.bitcast_convert_type(word & jnp.uint32(HIGH_HALF), F32)
        out.append((lo, hi))
    return out


def _proj_kernel(x_ref, w_ref, wm_ref, bm_ref, wa_ref, ws_ref, lg_ref, lb_ref, bs_ref,
                 o_ref, a_ref, s_ref, xb_ref, u_ref, v_ref, *, plain_blocks, silu_blocks, n_blocks):
    j = pl.program_id(1)
    sgu_blocks = 2
    silu_start = sgu_blocks + plain_blocks
    sigmoid_start = silu_start + silu_blocks
    contract_last = (((1,), (1,)), ((), ()))
    tm, tn = o_ref.shape
    sub = tn // PROJ_SUBTILES
    c = SGU_CHUNK
    groups = tn // c
    assert sgu_blocks + groups <= n_blocks

    def x_dot_w(cols):
        return lax.dot_general(xb_ref[...], w_ref[cols, :], contract_last, preferred_element_type=F32)

    def x_dot_wm(cols):
        return jnp.dot(xb_ref[...], wm_ref[:, cols], preferred_element_type=F32)

    def sgu_group():
        g = jnp.clip(j - sgu_blocks, 0, groups - 1)
        cols = pl.ds(pl.multiple_of(g * c, c), c)
        row = lax.broadcasted_iota(jnp.int32, (c, c), 0)
        col = lax.broadcasted_iota(jnp.int32, (c, c), 1)
        w = jnp.where(row >= col, ws_ref[g], 0.0).astype(BF16)
        ln_g, ln_b, bias = lg_ref[g], lb_ref[g], bs_ref[g]
        for ci in range(tm // c):
            rows = slice(ci * c, (ci + 1) * c)
            vn = _layer_norm(v_ref[rows, cols].astype(F32), ln_g, ln_b)
            mixed = jnp.dot(w, vn.astype(BF16), preferred_element_type=F32) + bias
            s_ref[rows, cols] = (u_ref[rows, cols].astype(F32) * mixed).astype(s_ref.dtype)

    def block(store):
        for cols in (slice(si * sub, (si + 1) * sub) for si in range(PROJ_SUBTILES)):
            store(cols)

    def store_u(cols):
        u_ref[:, cols] = _gelu(x_dot_w(cols)).astype(u_ref.dtype)

    def store_v(cols):
        v_ref[:, cols] = _gelu(x_dot_w(cols)).astype(v_ref.dtype)

    def store_plain(cols):
        o_ref[:, cols] = x_dot_w(cols).astype(o_ref.dtype)

    def store_silu(cols):
        acc = x_dot_w(cols)
        o_ref[:, cols] = (acc * _sigmoid(acc)).astype(o_ref.dtype)

    def store_sigmoid(cols):
        o_ref[:, cols] = _sigmoid(x_dot_wm(cols) + bm_ref[:, cols]).astype(o_ref.dtype)

    @pl.when(j == 0)
    def _():
        xb_ref[...] = x_ref[...].astype(BF16)
        a_ref[...] = lax.dot_general(
            xb_ref[...], wa_ref[...], contract_last, preferred_element_type=F32).astype(a_ref.dtype)
        block(store_u)

    @pl.when(j == 1)
    def _():
        block(store_v)

    @pl.when((j >= sgu_blocks) & (j < silu_start))
    def _():
        block(store_plain)
        sgu_group()

    @pl.when((j >= silu_start) & (j < sigmoid_start))
    def _():
        block(store_silu)
        sgu_group()

    @pl.when(j >= sigmoid_start)
    def _():
        block(store_sigmoid)
        sgu_group()


def _proj(x, w_t, w_m, b_m, w_a_t, w_s, ln_g, ln_b, b_s, n_plain, n_silu, tm):
    m, k = x.shape
    ng, c, _ = w_s.shape
    width = tn = ng * c
    n_m = w_m.shape[1]
    tm = min(tm, m)
    assert n_plain % tn == 0 and n_silu % tn == 0 and n_m % tn == 0 and tm % c == 0
    in_blocks = 2 + (n_plain + n_silu) // tn
    p_block = lambda i, j: (i, jnp.maximum(j - 2, 0))
    nbytes = tm * k * (4 + 1) + 2 * k * tn * 2 + tm * tn * (2 + 2 + 2 + 1) + k * LANES * 2
    return pl.pallas_call(
        functools.partial(_proj_kernel, plain_blocks=n_plain // tn, silu_blocks=n_silu // tn,
                          n_blocks=in_blocks + n_m // tn),
        grid=(m // tm, in_blocks + n_m // tn),
        in_specs=[
            pl.BlockSpec((tm, k), lambda i, j: (i, 0)),
            pl.BlockSpec((tn, k), lambda i, j: (jnp.minimum(j, in_blocks - 1), 0)),
            pl.BlockSpec((k, tn), lambda i, j: (0, jnp.maximum(j - in_blocks, 0))),
            pl.BlockSpec((1, tn), lambda i, j: (0, jnp.maximum(j - in_blocks, 0))),
            pl.BlockSpec((LANES, k), lambda i, j: (0, 0)),
            pl.BlockSpec((ng, c, c), lambda i, j: (0, 0, 0)),
            pl.BlockSpec((ng, 1, c), lambda i, j: (0, 0, 0)),
            pl.BlockSpec((ng, 1, c), lambda i, j: (0, 0, 0)),
            pl.BlockSpec((ng, c, 1), lambda i, j: (0, 0, 0)),
        ],
        out_specs=[pl.BlockSpec((tm, tn), p_block), pl.BlockSpec((tm, LANES), lambda i, j: (i, 0)),
                   pl.BlockSpec((tm, width), lambda i, j: (i, 0))],
        out_shape=[jax.ShapeDtypeStruct((m, n_plain + n_silu + n_m), BF16),
                   jax.ShapeDtypeStruct((m, LANES), BF16), jax.ShapeDtypeStruct((m, width), BF16)],
        scratch_shapes=[pltpu.VMEM((tm, k), BF16), pltpu.VMEM((tm, width), BF16), pltpu.VMEM((tm, width), BF16)],
        compiler_params=pltpu.CompilerParams(
            dimension_semantics=("parallel", "arbitrary"), vmem_limit_bytes=_vmem_limit(nbytes)),
        name="proj",
    )(x, w_t, w_m, b_m, w_a_t, w_s, ln_g.reshape(ng, 1, c), ln_b.reshape(ng, 1, c), b_s.reshape(ng, c, 1))


def _lane_block(index, width):
    return slice(index * width, (index + 1) * width)


def _gla_heads(heads, q_ref, k_ref, v_ref, g_ref, a_ref, wg_ref, bg_ref, ng_ref, st_ref, store):
    c = GLA_CHUNK
    ts = q_ref.shape[0]
    _, dv, dk = st_ref.shape
    grp = min(ts, GLA_GROUP_CHUNKS * c)
    ncg = grp // c
    contract_last = (((1,), (1,)), ((), ()))
    contract_first = (((0,), (0,)), ((), ()))

    row = lax.broadcasted_iota(jnp.int32, (grp, grp), 0)
    col = lax.broadcasted_iota(jnp.int32, (grp, grp), 1)
    shift = c.bit_length() - 1
    ones_tril = ((row >= col) & (jnp.right_shift(row, shift) == jnp.right_shift(col, shift))).astype(BF16)
    qrow = lax.broadcasted_iota(jnp.int32, (c, grp), 0)
    kcol = lax.broadcasted_iota(jnp.int32, (c, grp), 1)
    visible = [(kcol < j * c) | ((kcol < (j + 1) * c) & (kcol - j * c <= qrow)) for j in range(ncg)]
    zero_keys = jnp.zeros((c, dk), BF16)

    for hd in heads:
        kcols = _lane_block(hd, dk)
        vcols = _lane_block(hd, dv)
        z = jnp.dot(a_ref[...], wg_ref[:, kcols], preferred_element_type=F32) + bg_ref[:, kcols]
        log_a = (jnp.minimum(z, 0.0) - jnp.log1p(jnp.exp(-jnp.abs(z)))) * (1.0 / GLA_GATE_NORM)
        la_hi = log_a.astype(BF16)
        la_lo = (log_a - la_hi.astype(F32)).astype(BF16)
        la_split = jnp.concatenate([la_hi, la_lo], axis=1)
        state_t = st_ref[hd]
        for gi in range(ts // grp):
            rows = slice(gi * grp, (gi + 1) * grp)
            r = jnp.dot(ones_tril, la_split[rows, :], preferred_element_type=F32)
            b = r[:, :dk] + r[:, dk:]
            q = q_ref[rows, kcols].astype(F32) * (dk ** -0.5)
            k = k_ref[rows, kcols].astype(F32)
            v = v_ref[rows, vcols]
            q_dec = q * jnp.exp(b)
            k_inv = (k * jnp.exp(-b)).astype(BF16)
            chunk = [slice(j * c, (j + 1) * c) for j in range(ncg)]
            b_last = [b[(j + 1) * c - 1:(j + 1) * c, :] for j in range(ncg)]
            before = [jnp.zeros_like(b_last[0])]
            for j in range(ncg):
                before.append(before[j] + b_last[j])
            k_to_end = [k[chunk[j], :] * jnp.exp(b_last[j] - b[chunk[j], :]) for j in range(ncg)]

            q_grp = jnp.concatenate(
                [q_dec[chunk[j], :] * jnp.exp(before[j]) for j in range(ncg)], axis=0).astype(BF16)
            o = lax.dot_general(q_grp, state_t.astype(BF16), contract_last, preferred_element_type=F32)

            q_dec = q_dec.astype(BF16)
            attn = []
            for j in range(ncg):
                keys = [(k_to_end[i] * jnp.exp(before[j] - before[i + 1])).astype(BF16) for i in range(j)]
                keys += [k_inv[chunk[j], :]] + [zero_keys] * (ncg - 1 - j)
                keys = jnp.concatenate(keys, axis=0) if ncg > 1 else keys[0]
                scores = lax.dot_general(q_dec[chunk[j], :], keys, contract_last, preferred_element_type=F32)
                attn.append(jnp.where(visible[j], scores, 0.0).astype(BF16))
            attn = jnp.concatenate(attn, axis=0) if ncg > 1 else attn[0]
            o = o + jnp.dot(attn, v, preferred_element_type=F32)

            k_grp = jnp.concatenate(
                [(k_to_end[j] * jnp.exp(before[ncg] - before[j + 1])).astype(BF16) for j in range(ncg)], axis=0)
            state_t = state_t * jnp.exp(before[ncg]) + lax.dot_general(
                v, k_grp, contract_first, preferred_element_type=F32)

            o = o * lax.rsqrt(jnp.mean(jnp.square(o), axis=-1, keepdims=True) + LN_EPS) * ng_ref[:, vcols]
            store(rows, hd, o * g_ref[rows, vcols].astype(F32))
        st_ref[hd] = state_t


def _gla_kernel(q_ref, k_ref, v_ref, g_ref, a_ref, wg_ref, bg_ref, ng_ref, o_ref, st_ref):
    heads, dv, _ = st_ref.shape

    @pl.when(pl.program_id(2) == 0)
    def _():
        st_ref[...] = jnp.zeros_like(st_ref)

    def store(rows, head, o):
        o_ref[rows, _lane_block(head, dv)] = o.astype(o_ref.dtype)

    _gla_heads(range(heads), q_ref, k_ref, v_ref, g_ref, a_ref, wg_ref, bg_ref, ng_ref, st_ref, store)


def _gla(h2, col0, a_lr, w_gate, b_gate, norm_g, batch, ts):
    t = h2.shape[0]
    seq = t // batch
    nh = GLA_HEADS
    key_dim = w_gate.shape[1]
    dk = key_dim // nh
    val_dim = norm_g.shape[1]
    dv = val_dim // nh
    ts = min(ts, seq)
    ns = seq // ts
    hps = GLA_HEADS_PER_STEP
    wk, wv = hps * dk, hps * dv
    assert nh % hps == 0 and col0 % wk == 0 and (col0 + 2 * key_dim) % wv == 0
    qb = col0 // wk
    kb, vb, gb = qb + key_dim // wk, (col0 + 2 * key_dim) // wv, (col0 + 2 * key_dim + val_dim) // wv
    tok = lambda b, h, s: b * ns + s
    return pl.pallas_call(
        _gla_kernel,
        grid=(batch, nh // hps, ns),
        in_specs=[
            pl.BlockSpec((ts, wk), lambda b, h, s: (tok(b, h, s), qb + h)),
            pl.BlockSpec((ts, wk), lambda b, h, s: (tok(b, h, s), kb + h)),
            pl.BlockSpec((ts, wv), lambda b, h, s: (tok(b, h, s), vb + h)),
            pl.BlockSpec((ts, wv), lambda b, h, s: (tok(b, h, s), gb + h)),
            pl.BlockSpec((ts, LANES), lambda b, h, s: (tok(b, h, s), 0)),
            pl.BlockSpec((LANES, wk), lambda b, h, s: (0, h)),
            pl.BlockSpec((1, wk), lambda b, h, s: (0, h)),
            pl.BlockSpec((1, wv), lambda b, h, s: (0, h)),
        ],
        out_specs=pl.BlockSpec((ts, wv), lambda b, h, s: (tok(b, h, s), h)),
        out_shape=jax.ShapeDtypeStruct((t, val_dim), BF16),
        scratch_shapes=[pltpu.VMEM((hps, dv, dk), F32)],
        compiler_params=pltpu.CompilerParams(
            dimension_semantics=("parallel", "parallel", "arbitrary"),
            vmem_limit_bytes=_vmem_limit(hps * ts * (dk * 48 + dv * 16))),
        name="gla",
    )(h2, h2, h2, h2, a_lr, w_gate, b_gate, norm_g)


def _merge_kernel(s_ref, o_ref, wa_ref, wb_ref, ga_ref, gb_ref, out_ref):
    half = out_ref.shape[1] // 2
    for cols in (slice(0, half), slice(half, 2 * half)):
        ya = jnp.dot(s_ref[...], wa_ref[:, cols], preferred_element_type=F32)
        yb = jnp.dot(o_ref[...], wb_ref[:, cols], preferred_element_type=F32)
        out_ref[:, cols] = (ga_ref[:, cols].astype(F32) * ya
                            + gb_ref[:, cols].astype(F32) * yb).astype(out_ref.dtype)


def _merge(s, o, w_a, w_b, gates, gate_col0, tm, tn):
    t, ka = s.shape
    kb = o.shape[1]
    d = w_a.shape[1]
    tm, tn = min(tm, t), min(tn, d)
    gj = gate_col0 // tn
    nbytes = tm * (ka + kb) * 2 + (ka + kb) * tn * 2 + tm * tn * (2 * 2 + 2 + 8)
    return pl.pallas_call(
        _merge_kernel,
        grid=(t // tm, d // tn),
        in_specs=[
            pl.BlockSpec((tm, ka), lambda i, j: (i, 0)),
            pl.BlockSpec((tm, kb), lambda i, j: (i, 0)),
            pl.BlockSpec((ka, tn), lambda i, j: (0, j)),
            pl.BlockSpec((kb, tn), lambda i, j: (0, j)),
            pl.BlockSpec((tm, tn), lambda i, j: (i, gj + j)),
            pl.BlockSpec((tm, tn), lambda i, j: (i, gj + j + d // tn)),
        ],
        out_specs=pl.BlockSpec((tm, tn), lambda i, j: (i, j)),
        out_shape=jax.ShapeDtypeStruct((t, d), BF16),
        compiler_params=pltpu.CompilerParams(
            dimension_semantics=("parallel", "arbitrary"), vmem_limit_bytes=_vmem_limit(nbytes)),
        name="merge",
    )(s, o, w_a, w_b, gates, gates)


def _route(logits):
    lane = lax.broadcasted_iota(jnp.int32, logits.shape, 1).astype(F32)
    neg = float("-inf")
    big = float(LANES)
    gl = jnp.where(lane < N_GROUPS, logits, neg)
    gmax = jnp.max(gl, axis=1, keepdims=True)
    gidx = jnp.min(jnp.where(gl == gmax, lane, big), axis=1, keepdims=True)
    p_group = 1.0 / jnp.sum(jnp.exp(gl - gmax), axis=1, keepdims=True)
    lo = N_GROUPS + EXPERTS_PER_GROUP * gidx
    el = jnp.where((lane >= lo) & (lane < lo + EXPERTS_PER_GROUP), logits, neg)
    v1 = jnp.max(el, axis=1, keepdims=True)
    i1 = jnp.min(jnp.where(el == v1, lane, big), axis=1, keepdims=True)
    el2 = jnp.where(lane == i1, neg, el)
    v2 = jnp.max(el2, axis=1, keepdims=True)
    i2 = jnp.min(jnp.where(el2 == v2, lane, big), axis=1, keepdims=True)
    t = jnp.exp(v2 - v1)
    w1 = p_group / (1.0 + t)
    w2 = p_group * t / (1.0 + t)
    return jnp.where(lane == 0, i1 - N_GROUPS,
                     jnp.where(lane == 1, i2 - N_GROUPS,
                               jnp.where(lane == 2, w1, jnp.where(lane == 3, w2, 0.0))))


def _expert_hits(route):
    lane = lax.broadcasted_iota(jnp.int32, route.shape, 1).astype(F32)
    return [lane == route[:, k:k + 1] for k in range(TOP_K)]


def _out_kernel(m_ref, w_ref, x_ref, g_ref, b_ref, wr_ref, br_ref, h_ref, hs_ref, r_ref, cnt_ref, mix_ref):
    i = pl.program_id(0)

    @pl.when(i == 0)
    def _():
        cnt_ref[...] = jnp.zeros_like(cnt_ref)
        mix_ref[1] = jnp.zeros(mix_ref.shape[1:], F32)

    has_prev = (i > 0).astype(F32)
    sub = m_ref.shape[0] // OUT_SUBTILES

    def step(cur):
        for si in range(OUT_SUBTILES):
            rows = slice(si * sub, (si + 1) * sub)
            mix_ref[cur, rows, :] = jnp.dot(m_ref[rows, :], w_ref[...], preferred_element_type=F32)
            h = _layer_norm(DEEPNORM_ALPHA * x_ref[rows, :] + mix_ref[1 - cur, rows, :], g_ref[...], b_ref[...])
            h_ref[rows, :] = h
            _store_slabs(hs_ref, si * sub, h)
            logits = jnp.dot(h.astype(BF16), wr_ref[...], preferred_element_type=F32) + br_ref[...]
            route = _route(logits)
            r_ref[rows, :] = route
            cnt_ref[...] += has_prev * sum(
                jnp.sum(hit.astype(F32), axis=0, keepdims=True) for hit in _expert_hits(route))

    for parity in range(2):
        pl.when(lax.rem(i, 2) == parity)(functools.partial(step, parity))


def _out(merged, w_out, x, ln_g, ln_b, w_r, b_r, tm):
    t, d = x.shape
    tm = min(tm, t)
    n = t // tm
    nbytes = d * d * 2 + tm * d * (2 + 4 + 4 + 2 + 8 + 8) + d * LANES * 2
    cur = lambda i: (jnp.minimum(i, n - 1), 0)
    prev = lambda i: (jnp.maximum(i - 1, 0), 0)
    return pl.pallas_call(
        _out_kernel,
        grid=(n + 1,),
        in_specs=[
            pl.BlockSpec((tm, d), cur),
            pl.BlockSpec((d, d), lambda i: (0, 0)),
            pl.BlockSpec((tm, d), prev),
            pl.BlockSpec((1, d), lambda i: (0, 0)),
            pl.BlockSpec((1, d), lambda i: (0, 0)),
            pl.BlockSpec((d, LANES), lambda i: (0, 0)),
            pl.BlockSpec((1, LANES), lambda i: (0, 0)),
        ],
        out_specs=[pl.BlockSpec((tm, d), prev),
                   pl.BlockSpec((tm * SLAB_ROWS, LANES), prev),
                   pl.BlockSpec((tm, LANES), prev),
                   pl.BlockSpec((1, LANES), lambda i: (0, 0))],
        out_shape=[jax.ShapeDtypeStruct((t, d), F32),
                   jax.ShapeDtypeStruct((t * SLAB_ROWS, LANES), jnp.uint32),
                   jax.ShapeDtypeStruct((t, LANES), F32),
                   jax.ShapeDtypeStruct((1, LANES), F32)],
        scratch_shapes=[pltpu.VMEM((2, tm, d), F32)],
        compiler_params=pltpu.CompilerParams(
            dimension_semantics=("arbitrary",), vmem_limit_bytes=_vmem_limit(nbytes)),
        name="out_ln_route",
    )(merged, w_out, x, ln_g, ln_b, w_r, b_r)


def _rank_kernel(r_ref, cnt_ref, dest_ref, next_ref):
    rows = r_ref.shape[0]

    @pl.when(pl.program_id(0) == 0)
    def _():
        blocks = jnp.floor((cnt_ref[...] + (MOE_ROWS - 1)) * (1.0 / MOE_ROWS))
        k = lax.broadcasted_iota(jnp.int32, (LANES, LANES), 0)
        e = lax.broadcasted_iota(jnp.int32, (LANES, LANES), 1)
        blocks8 = jnp.broadcast_to(blocks, (8, LANES)).astype(BF16)
        first_block = jnp.dot(blocks8, (k < e).astype(BF16), preferred_element_type=F32)
        next_ref[...] = first_block[0:1, :] * MOE_ROWS

    hits = _expert_hits(r_ref[...])
    cnt = sum(hit.astype(F32) for hit in hits)
    row = lax.broadcasted_iota(jnp.int32, (rows, rows), 0)
    col = lax.broadcasted_iota(jnp.int32, (rows, rows), 1)
    earlier = (row > col).astype(BF16)
    slot = jnp.dot(earlier, cnt.astype(BF16), preferred_element_type=F32) + next_ref[...]
    dest = [jnp.sum(jnp.where(hit, slot, 0.0), axis=1, keepdims=True) for hit in hits]
    lane = lax.broadcasted_iota(jnp.int32, (rows, LANES), 1)
    by_token = jnp.where(lane == 0, dest[0], jnp.where(lane == 1, dest[1], 0.0))
    dest_ref[0] = jnp.transpose(by_token)[:TOP_K, :].astype(jnp.int32)
    next_ref[...] += jnp.sum(cnt, axis=0, keepdims=True)


def _rank(route, counts, tm):
    t = route.shape[0]
    tm = min(tm, t)
    return pl.pallas_call(
        _rank_kernel,
        grid=(t // tm,),
        in_specs=[pl.BlockSpec((tm, LANES), lambda i: (i, 0)), pl.BlockSpec((1, LANES), lambda i: (0, 0))],
        out_specs=pl.BlockSpec((1, TOP_K, tm), lambda i: (i, 0, 0)),
        out_shape=jax.ShapeDtypeStruct((t // tm, TOP_K, tm), jnp.int32),
        scratch_shapes=[pltpu.VMEM((1, LANES), F32)],
        compiler_params=pltpu.CompilerParams(dimension_semantics=("arbitrary",)),
        name="rank",
    )(route, counts)


def _slab_copy(src, src_row, dst, dst_row, sem):
    s0 = pl.multiple_of(src_row * SLAB_ROWS, SLAB_ROWS)
    d0 = pl.multiple_of(dst_row * SLAB_ROWS, SLAB_ROWS)
    return pltpu.make_async_copy(src.at[pl.ds(s0, SLAB_ROWS), :], dst.at[pl.ds(d0, SLAB_ROWS), :], sem)


def _dispatch_kernel(zrow_ref, dest_ref, hs_ref, xs_ref, buf_ref, zero_ref, lsem, ssem, zsem, *, n_steps):
    i = pl.program_id(0)
    rows = buf_ref.shape[1] // SLAB_ROWS
    zrows = zero_ref.shape[0]

    def zero_copy(e):
        start_row = pl.multiple_of(jnp.maximum(zrow_ref[e], 0) * SLAB_ROWS, zrows)
        return pltpu.make_async_copy(zero_ref, xs_ref.at[pl.ds(start_row, zrows), :], zsem)

    def load(tile, slot):
        src = hs_ref.at[pl.ds(pl.multiple_of(tile * rows * SLAB_ROWS, SLAB_ROWS), rows * SLAB_ROWS), :]
        return pltpu.make_async_copy(src, buf_ref.at[slot], lsem.at[slot])

    def drain(slot):
        for k in range(TOP_K):
            pltpu.make_async_copy(buf_ref.at[slot], xs_ref.at[pl.ds(0, rows * SLAB_ROWS), :], ssem.at[slot]).wait()

    @pl.when(i == 0)
    def _():
        zero_ref[...] = jnp.zeros_like(zero_ref)

        def start(e, carry):
            @pl.when(zrow_ref[e] >= 0)
            def _():
                zero_copy(e).start()
            return carry

        def wait(e, carry):
            @pl.when(zrow_ref[e] >= 0)
            def _():
                zero_copy(e).wait()
            return carry

        lax.fori_loop(0, zrow_ref.shape[0], start, 0)
        for ahead in range(min(DISPATCH_SLOTS - 1, n_steps)):
            load(ahead, ahead).start()
        lax.fori_loop(0, zrow_ref.shape[0], wait, 0)

    def step(slot):
        load(i, slot).wait()

        def start(r, carry):
            for k in range(TOP_K):
                _slab_copy(buf_ref.at[slot], r, xs_ref, dest_ref[0, k, r], ssem.at[slot]).start(priority=k)
            return carry

        lax.fori_loop(0, rows, start, 0, unroll=ROW_DMA_UNROLL)
        free = (slot + DISPATCH_SLOTS - 1) % DISPATCH_SLOTS

        @pl.when(i > 0)
        def _():
            drain(free)

        @pl.when(i + DISPATCH_SLOTS - 1 < n_steps)
        def _():
            load(i + DISPATCH_SLOTS - 1, free).start()

        @pl.when(i == n_steps - 1)
        def _():
            drain(slot)

    for slot in range(DISPATCH_SLOTS):
        pl.when(lax.rem(i, DISPATCH_SLOTS) == slot)(functools.partial(step, slot))


def _dispatch(hs, dest, zrow, n_rows):
    n, _, tm = dest.shape
    grid_spec = pltpu.PrefetchScalarGridSpec(
        num_scalar_prefetch=1,
        grid=(n,),
        in_specs=[
            pl.BlockSpec((1, TOP_K, tm), lambda i, z: (i, 0, 0), memory_space=pltpu.SMEM),
            pl.BlockSpec(memory_space=pl.ANY),
        ],
        out_specs=pl.BlockSpec(memory_space=pl.ANY),
        scratch_shapes=[pltpu.VMEM((DISPATCH_SLOTS, tm * SLAB_ROWS, LANES), jnp.uint32),
                        pltpu.VMEM((MOE_ROWS * SLAB_ROWS, LANES), jnp.uint32),
                        pltpu.SemaphoreType.DMA((DISPATCH_SLOTS,)), pltpu.SemaphoreType.DMA((DISPATCH_SLOTS,)),
                        pltpu.SemaphoreType.DMA],
    )
    return pl.pallas_call(
        functools.partial(_dispatch_kernel, n_steps=n),
        grid_spec=grid_spec,
        out_shape=jax.ShapeDtypeStruct((n_rows * SLAB_ROWS, LANES), jnp.uint32),
        compiler_params=pltpu.CompilerParams(dimension_semantics=("arbitrary",)),
        name="dispatch",
    )(zrow, dest, hs)


def _expert_kernel(be_ref, nx_ref, nu_ref, x_ref, w1_ref, w3_ref, w2_ref, y_ref,
                   w1s_ref, w3s_ref, w2s_ref, w1b_ref, w3b_ref, w2b_ref, wsem):
    i = pl.program_id(0)
    d = w1_ref.shape[1]
    rows = x_ref.shape[0] // SLAB_ROWS

    def fetch(e):
        return [pltpu.make_async_copy(src.at[e], dst, wsem.at[n])
                for n, (src, dst) in enumerate(((w1_ref, w1s_ref), (w3_ref, w3s_ref), (w2_ref, w2s_ref)))]

    @pl.when(i == 0)
    def _():
        for copy in fetch(be_ref[0]):
            copy.start()

    @pl.when((i == 0) | (be_ref[i] != be_ref[jnp.maximum(i - 1, 0)]))
    def _():
        for copy in fetch(be_ref[i]):
            copy.wait()
        for s in range(SLAB_ROWS):
            for half, src0 in enumerate((s * LANES, d // 2 + s * LANES)):
                dst0 = (2 * s + half) * LANES
                w1b_ref[dst0:dst0 + LANES, :] = w1s_ref[src0:src0 + LANES, :].astype(BF16)
                w3b_ref[dst0:dst0 + LANES, :] = w3s_ref[src0:src0 + LANES, :].astype(BF16)
        w2b_ref[...] = w2s_ref[...].astype(BF16)

        @pl.when(nx_ref[i] >= 0)
        def _():
            for copy in fetch(nx_ref[i]):
                copy.start()

    @pl.when(i < nu_ref[0])
    def _():
        x = jnp.concatenate([part.astype(BF16) for pair in _load_slabs(x_ref, rows) for part in pair], axis=1)
        a = jnp.dot(x, w1b_ref[...], preferred_element_type=F32)
        b = jnp.dot(x, w3b_ref[...], preferred_element_type=F32)
        mid = (a * _sigmoid(a) * b).astype(BF16)
        _store_slabs(y_ref, 0, jnp.dot(mid, w2b_ref[...], preferred_element_type=F32))

    @pl.when(i >= nu_ref[0])
    def _():
        y_ref[...] = jnp.zeros_like(y_ref)


def _experts(xs, block_expert, next_expert, n_used, w1, w3, w2):
    _, d, de = w1.shape
    nb = xs.shape[0] // (MOE_ROWS * SLAB_ROWS)
    nbytes = MOE_ROWS * d * (2 + 2 + 2 + 4) + 3 * d * de * (2 + 1) + MOE_ROWS * de * 12
    grid_spec = pltpu.PrefetchScalarGridSpec(
        num_scalar_prefetch=3,
        grid=(nb,),
        in_specs=[
            pl.BlockSpec((MOE_ROWS * SLAB_ROWS, LANES), lambda i, be, nx, nu: (jnp.minimum(i, nu[0] - 1), 0)),
            pl.BlockSpec(memory_space=pl.ANY),
            pl.BlockSpec(memory_space=pl.ANY),
            pl.BlockSpec(memory_space=pl.ANY),
        ],
        out_specs=pl.BlockSpec((MOE_ROWS * SLAB_ROWS, LANES), lambda i, be, nx, nu: (i, 0)),
        scratch_shapes=[pltpu.VMEM((d, de), F32), pltpu.VMEM((d, de), F32), pltpu.VMEM((de, d), F32),
                        pltpu.VMEM((d, de), BF16), pltpu.VMEM((d, de), BF16), pltpu.VMEM((de, d), BF16),
                        pltpu.SemaphoreType.DMA((3,))],
    )
    return pl.pallas_call(
        _expert_kernel,
        grid_spec=grid_spec,
        out_shape=jax.ShapeDtypeStruct(xs.shape, jnp.uint32),
        compiler_params=pltpu.CompilerParams(
            dimension_semantics=("arbitrary",), vmem_limit_bytes=_vmem_limit(nbytes)),
        name="experts",
    )(block_expert, next_expert, n_used, xs, w1, w3, w2)


def _combine_kernel(dest_ref, dnext_ref, h_ref, r_ref, y_ref, g_ref, b_ref, o_ref, buf_ref, sem, *, n_steps):
    i = pl.program_id(0)
    rows = h_ref.shape[0]
    chunk = min(rows, COMBINE_CHUNK)

    def gather(dref, slot, r, k):
        return _slab_copy(y_ref, dref[0, k, r], buf_ref.at[slot, k], r, sem.at[slot])

    def wait(slot):
        for k in range(TOP_K):
            pltpu.make_async_copy(y_ref.at[pl.ds(0, rows * SLAB_ROWS), :], buf_ref.at[slot, k],
                                  sem.at[slot]).wait()

    @pl.when(i == 0)
    def _():
        def start(r, carry):
            for k in range(TOP_K):
                gather(dest_ref, 0, r, k).start(priority=k)
            return carry

        lax.fori_loop(0, rows, start, 0, unroll=ROW_DMA_UNROLL)

    def step(slot):
        wait(slot)
        for c0 in range(0, rows, chunk):
            route = r_ref[c0:c0 + chunk, :]
            slabs = [_load_slabs(buf_ref.at[slot, k], chunk, c0) for k in range(TOP_K)]
            lo, hi = [], []
            for s in range(SLAB_ROWS):
                for half, out in enumerate((lo, hi)):
                    out.append(sum(route[:, 2 + k:3 + k] * slabs[k][s][half] for k in range(TOP_K)))
            moe = jnp.concatenate(lo + hi, axis=1)
            o_ref[c0:c0 + chunk, :] = _layer_norm(
                DEEPNORM_ALPHA * h_ref[c0:c0 + chunk, :] + moe, g_ref[...], b_ref[...])
            for r in range(c0, c0 + chunk):
                for k in range(TOP_K):
                    gather(dnext_ref, 1 - slot, r, k).start(priority=k)

    for parity in range(2):
        pl.when(lax.rem(i, 2) == parity)(functools.partial(step, parity))

    @pl.when(i == n_steps - 1)
    def _():
        wait(n_steps % 2)


def _combine(h, route, dest, y, ln_g, ln_b):
    t, d = h.shape
    n, _, tm = dest.shape
    return pl.pallas_call(
        functools.partial(_combine_kernel, n_steps=n),
        grid=(n,),
        in_specs=[
            pl.BlockSpec((1, TOP_K, tm), lambda i: (i, 0, 0), memory_space=pltpu.SMEM),
            pl.BlockSpec((1, TOP_K, tm), lambda i: (jnp.minimum(i + 1, n - 1), 0, 0), memory_space=pltpu.SMEM),
            pl.BlockSpec((tm, d), lambda i: (i, 0)),
            pl.BlockSpec((tm, LANES), lambda i: (i, 0)),
            pl.BlockSpec(memory_space=pl.ANY),
            pl.BlockSpec((1, d), lambda i: (0, 0)),
            pl.BlockSpec((1, d), lambda i: (0, 0)),
        ],
        out_specs=pl.BlockSpec((tm, d), lambda i: (i, 0)),
        out_shape=jax.ShapeDtypeStruct((t, d), F32),
        scratch_shapes=[pltpu.VMEM((2, TOP_K, tm * SLAB_ROWS, LANES), jnp.uint32), pltpu.SemaphoreType.DMA((2,))],
        compiler_params=pltpu.CompilerParams(dimension_semantics=("arbitrary",)),
        name="combine",
    )(dest, dest, h, route, y, ln_g, ln_b)


def kernel(x, w_in, w_gate_a2, b_gate_a, sgu_ln_g, sgu_ln_b, sgu_w_s, sgu_b_s, gla_norm_g, w_branch_a, w_branch_b, w_merge, b_merge, w_out, ln1_g, ln1_b, w_router_group, b_router_group, w_router_expert, b_router_expert, w_exp_gate, w_exp_up, w_exp_down, ln2_g, ln2_b):
    batch, seq, d = x.shape
    t = batch * seq
    assert w_in.shape[0] == 1, "one layer"
    assert seq % SGU_CHUNK == 0 and seq % GLA_CHUNK == 0 and t % MOE_ROWS == 0
    sgu_width = sgu_ln_g.shape[1]
    key_dim = w_gate_a2.shape[2]
    val_dim = gla_norm_g.shape[1]
    rank = w_gate_a2.shape[1]
    xf = x.reshape(t, d)
    n_uv, n_h2 = 2 * sgu_width, 2 * key_dim + 2 * val_dim

    w_in_t = jnp.swapaxes(w_in.reshape(w_in.shape[1:]), 0, 1).astype(BF16)
    w_a_t = jnp.pad(w_in_t[n_uv + n_h2:, :], ((0, LANES - rank), (0, 0)))
    assert n_uv == 2 * sgu_w_s.shape[1] * SGU_CHUNK
    p, a_lr, s = _proj(xf, w_in_t, w_merge[0].astype(BF16), b_merge, w_a_t,
                       sgu_w_s[0], sgu_ln_g[0], sgu_ln_b[0], sgu_b_s[0], n_h2 - val_dim, val_dim, 1024)

    w_gate = jnp.pad(w_gate_a2[0].astype(BF16), ((0, LANES - rank), (0, 0)))
    o = _gla(p, 0, a_lr, w_gate, b_gate_a, gla_norm_g, batch, 2048)
    merged = _merge(s, o, w_branch_a[0].astype(BF16), w_branch_b[0].astype(BF16), p, n_h2, 1024, 1024)

    w_r = jnp.concatenate([w_router_group[0], w_router_expert[0]], axis=1)
    n_r = w_r.shape[1]
    w_r = jnp.pad(w_r, ((0, 0), (0, LANES - n_r))).astype(BF16)
    b_r = jnp.pad(jnp.concatenate([b_router_group, b_router_expert], axis=1), ((0, 0), (0, LANES - n_r)))
    h1, h1_slabs, route, counts = _out(merged, w_out[0].astype(BF16), xf, ln1_g, ln1_b, w_r, b_r, 512)

    assert t * TOP_K // MOE_ROWS <= 256, "per-expert block counts must stay exact in bf16"
    dest = _rank(route, counts, MOE_TOKEN_TILE)
    blocks_per_expert = (counts[0, :N_EXPERTS].astype(jnp.int32) + MOE_ROWS - 1) // MOE_ROWS
    block_ends = jnp.cumsum(blocks_per_expert)
    n_blocks = t * TOP_K // MOE_ROWS + N_EXPERTS
    n_used = block_ends[-1:]
    block_ids = jnp.minimum(jnp.arange(n_blocks, dtype=jnp.int32), n_used[0] - 1)
    experts = jnp.arange(N_EXPERTS, dtype=jnp.int32)
    block_expert = jnp.minimum(
        jnp.sum(block_ends[None, :] <= block_ids[:, None], axis=1), N_EXPERTS - 1).astype(jnp.int32)
    later = (experts[None, :] > block_expert[:, None]) & (blocks_per_expert[None, :] > 0)
    next_expert = jnp.min(jnp.where(later, experts[None, :], N_EXPERTS), axis=1)
    next_expert = jnp.where(next_expert < N_EXPERTS, next_expert, -1).astype(jnp.int32)
    tail_ids = n_used[0] + experts
    zrow = jnp.concatenate([
        jnp.where(blocks_per_expert > 0, (block_ends - 1) * MOE_ROWS, -1),
        jnp.where(tail_ids < n_blocks, tail_ids * MOE_ROWS, -1)]).astype(jnp.int32)

    xs = _dispatch(h1_slabs, dest, zrow, n_blocks * MOE_ROWS)
    y = _experts(xs, block_expert, next_expert, n_used.astype(jnp.int32),
                 w_exp_gate[0], w_exp_up[0], w_exp_down[0])
    out = _combine(h1, route, dest, y, ln2_g, ln2_b)
    return out.reshape(batch, seq, d)
```

```python
import functools

import jax
import jax.numpy as jnp
from jax import lax
from jax.experimental import pallas as pl
from jax.experimental.pallas import tpu as pltpu

F32 = jnp.float32
BF16 = jnp.bfloat16

SGU_CHUNK = 128
GLA_HEADS = 4
GLA_CHUNK = 64
GLA_GATE_NORM = 16.0
GLA_HEADS_PER_STEP = 2
GLA_GROUP_CHUNKS = 4
N_GROUPS = 8
EXPERTS_PER_GROUP = 8
N_EXPERTS = N_GROUPS * EXPERTS_PER_GROUP
TOP_K = 2
LN_EPS = 1e-5
DEEPNORM_ALPHA = 2.0 ** 0.25

LANES = 128
MOE_ROWS = 256
MOE_TOKEN_TILE = 512
ROW_DMA_UNROLL = 8
DISPATCH_SLOTS = 3
PROJ_SUBTILES = 2
OUT_SUBTILES = 2
COMBINE_CHUNK = 32
V7X_VMEM_BYTES = 64 * 2 ** 20


def _vmem_limit(nbytes):
    return int(min(max(2 * nbytes, 16 * 2 ** 20), V7X_VMEM_BYTES - 8 * 2 ** 20))


def _layer_norm(y, g, b):
    mu = jnp.mean(y, axis=-1, keepdims=True)
    var = jnp.mean(jnp.square(y - mu), axis=-1, keepdims=True)
    return (y - mu) * lax.rsqrt(var + LN_EPS) * g + b


def _gelu(x):
    return 0.5 * x * (1.0 + lax.erf(x * (2.0 ** -0.5)))


def _sigmoid(x):
    return 0.5 * (jnp.tanh(0.5 * x) + 1.0)


SLAB_ROWS = 8
HIGH_HALF = 0xFFFF0000


def _bf16_bits(x):
    return lax.bitcast_convert_type(x.astype(BF16).astype(F32), jnp.uint32)


def _store_slabs(ref, row0, x):
    rows, d = x.shape
    assert d == 2 * SLAB_ROWS * LANES
    for s in range(SLAB_ROWS):
        lo = _bf16_bits(x[:, s * LANES:(s + 1) * LANES])
        hi = _bf16_bits(x[:, d // 2 + s * LANES:d // 2 + (s + 1) * LANES])
        word = jnp.right_shift(lo, jnp.uint32(16)) | (hi & jnp.uint32(HIGH_HALF))
        ref[pl.ds(row0 * SLAB_ROWS + s, rows, stride=SLAB_ROWS), :] = word


def _load_slabs(ref, rows, row0=0):
    out = []
    for s in range(SLAB_ROWS):
        word = ref[pl.ds(row0 * SLAB_ROWS + s, rows, stride=SLAB_ROWS), :]
        lo = lax.bitcast_convert_type(jnp.left_shift(word, jnp.uint32(16)), F32)
        hi = lax.bitcast_convert_type(word & jnp.uint32(HIGH_HALF), F32)
        out.append((lo, hi))
    return out


def _proj_kernel(x_ref, w_ref, wm_ref, bm_ref, wa_ref, ws_ref, lg_ref, lb_ref, bs_ref,
                 o_ref, a_ref, s_ref, xb_ref, u_ref, v_ref, *, plain_blocks, silu_blocks, n_blocks):
    j = pl.program_id(1)
    sgu_blocks = 2
    silu_start = sgu_blocks + plain_blocks
    sigmoid_start = silu_start + silu_blocks
    contract_last = (((1,), (1,)), ((), ()))
    tm, tn = o_ref.shape
    sub = tn // PROJ_SUBTILES
    c = SGU_CHUNK
    groups = tn // c
    assert sgu_blocks + groups <= n_blocks

    def x_dot_w(cols):
        return lax.dot_general(xb_ref[...], w_ref[cols, :], contract_last, preferred_element_type=F32)

    def x_dot_wm(cols):
        return jnp.dot(xb_ref[...], wm_ref[:, cols], preferred_element_type=F32)

    def sgu_group():
        g = jnp.clip(j - sgu_blocks, 0, groups - 1)
        cols = pl.ds(pl.multiple_of(g * c, c), c)
        row = lax.broadcasted_iota(jnp.int32, (c, c), 0)
        col = lax.broadcasted_iota(jnp.int32, (c, c), 1)
        w = jnp.where(row >= col, ws_ref[g], 0.0).astype(BF16)
        ln_g, ln_b, bias = lg_ref[g], lb_ref[g], bs_ref[g]
        for ci in range(tm // c):
            rows = slice(ci * c, (ci + 1) * c)
            vn = _layer_norm(v_ref[rows, cols].astype(F32), ln_g, ln_b)
            mixed = jnp.dot(w, vn.astype(BF16), preferred_element_type=F32) + bias
            s_ref[rows, cols] = (u_ref[rows, cols].astype(F32) * mixed).astype(s_ref.dtype)

    def block(store):
        for cols in (slice(si * sub, (si + 1) * sub) for si in range(PROJ_SUBTILES)):
            store(cols)

    def store_u(cols):
        u_ref[:, cols] = _gelu(x_dot_w(cols)).astype(u_ref.dtype)

    def store_v(cols):
        v_ref[:, cols] = _gelu(x_dot_w(cols)).astype(v_ref.dtype)

    def store_plain(cols):
        o_ref[:, cols] = x_dot_w(cols).astype(o_ref.dtype)

    def store_silu(cols):
        acc = x_dot_w(cols)
        o_ref[:, cols] = (acc * _sigmoid(acc)).astype(o_ref.dtype)

    def store_sigmoid(cols):
        o_ref[:, cols] = _sigmoid(x_dot_wm(cols) + bm_ref[:, cols]).astype(o_ref.dtype)

    @pl.when(j == 0)
    def _():
        xb_ref[...] = x_ref[...].astype(BF16)
        a_ref[...] = lax.dot_general(
            xb_ref[...], wa_ref[...], contract_last, preferred_element_type=F32).astype(a_ref.dtype)
        block(store_u)

    @pl.when(j == 1)
    def _():
        block(store_v)

    @pl.when((j >= sgu_blocks) & (j < silu_start))
    def _():
        block(store_plain)
        sgu_group()

    @pl.when((j >= silu_start) & (j < sigmoid_start))
    def _():
        block(store_silu)
        sgu_group()

    @pl.when(j >= sigmoid_start)
    def _():
        block(store_sigmoid)
        sgu_group()


def _proj(x, w_t, w_m, b_m, w_a_t, w_s, ln_g, ln_b, b_s, n_plain, n_silu, tm):
    m, k = x.shape
    ng, c, _ = w_s.shape
    width = tn = ng * c
    n_m = w_m.shape[1]
    tm = min(tm, m)
    assert n_plain % tn == 0 and n_silu % tn == 0 and n_m % tn == 0 and tm % c == 0
    in_blocks = 2 + (n_plain + n_silu) // tn
    p_block = lambda i, j: (i, jnp.maximum(j - 2, 0))
    nbytes = tm * k * (4 + 1) + 2 * k * tn * 2 + tm * tn * (2 + 2 + 2 + 1) + k * LANES * 2
    return pl.pallas_call(
        functools.partial(_proj_kernel, plain_blocks=n_plain // tn, silu_blocks=n_silu // tn,
                          n_blocks=in_blocks + n_m // tn),
        grid=(m // tm, in_blocks + n_m // tn),
        in_specs=[
            pl.BlockSpec((tm, k), lambda i, j: (i, 0)),
            pl.BlockSpec((tn, k), lambda i, j: (jnp.minimum(j, in_blocks - 1), 0)),
            pl.BlockSpec((k, tn), lambda i, j: (0, jnp.maximum(j - in_blocks, 0))),
            pl.BlockSpec((1, tn), lambda i, j: (0, jnp.maximum(j - in_blocks, 0))),
            pl.BlockSpec((LANES, k), lambda i, j: (0, 0)),
            pl.BlockSpec((ng, c, c), lambda i, j: (0, 0, 0)),
            pl.BlockSpec((ng, 1, c), lambda i, j: (0, 0, 0)),
            pl.BlockSpec((ng, 1, c), lambda i, j: (0, 0, 0)),
            pl.BlockSpec((ng, c, 1), lambda i, j: (0, 0, 0)),
        ],
        out_specs=[pl.BlockSpec((tm, tn), p_block), pl.BlockSpec((tm, LANES), lambda i, j: (i, 0)),
                   pl.BlockSpec((tm, width), lambda i, j: (i, 0))],
        out_shape=[jax.ShapeDtypeStruct((m, n_plain + n_silu + n_m), BF16),
                   jax.ShapeDtypeStruct((m, LANES), BF16), jax.ShapeDtypeStruct((m, width), BF16)],
        scratch_shapes=[pltpu.VMEM((tm, k), BF16), pltpu.VMEM((tm, width), BF16), pltpu.VMEM((tm, width), BF16)],
        compiler_params=pltpu.CompilerParams(
            dimension_semantics=("parallel", "arbitrary"), vmem_limit_bytes=_vmem_limit(nbytes)),
        name="proj",
    )(x, w_t, w_m, b_m, w_a_t, w_s, ln_g.reshape(ng, 1, c), ln_b.reshape(ng, 1, c), b_s.reshape(ng, c, 1))


def _lane_block(index, width):
    return slice(index * width, (index + 1) * width)


def _gla_heads(heads, q_ref, k_ref, v_ref, g_ref, a_ref, wg_ref, bg_ref, ng_ref, st_ref, store):
    c = GLA_CHUNK
    ts = q_ref.shape[0]
    _, dv, dk = st_ref.shape
    grp = min(ts, GLA_GROUP_CHUNKS * c)
    ncg = grp // c
    contract_last = (((1,), (1,)), ((), ()))
    contract_first = (((0,), (0,)), ((), ()))

    row = lax.broadcasted_iota(jnp.int32, (grp, grp), 0)
    col = lax.broadcasted_iota(jnp.int32, (grp, grp), 1)
    shift = c.bit_length() - 1
    ones_tril = ((row >= col) & (jnp.right_shift(row, shift) == jnp.right_shift(col, shift))).astype(BF16)
    qrow = lax.broadcasted_iota(jnp.int32, (c, grp), 0)
    kcol = lax.broadcasted_iota(jnp.int32, (c, grp), 1)
    visible = [(kcol < j * c) | ((kcol < (j + 1) * c) & (kcol - j * c <= qrow)) for j in range(ncg)]
    zero_keys = jnp.zeros((c, dk), BF16)

    for hd in heads:
        kcols = _lane_block(hd, dk)
        vcols = _lane_block(hd, dv)
        z = jnp.dot(a_ref[...], wg_ref[:, kcols], preferred_element_type=F32) + bg_ref[:, kcols]
        log_a = (jnp.minimum(z, 0.0) - jnp.log1p(jnp.exp(-jnp.abs(z)))) * (1.0 / GLA_GATE_NORM)
        la_hi = log_a.astype(BF16)
        la_lo = (log_a - la_hi.astype(F32)).astype(BF16)
        la_split = jnp.concatenate([la_hi, la_lo], axis=1)
        state_t = st_ref[hd]
        for gi in range(ts // grp):
            rows = slice(gi * grp, (gi + 1) * grp)
            r = jnp.dot(ones_tril, la_split[rows, :], preferred_element_type=F32)
            b = r[:, :dk] + r[:, dk:]
            q = q_ref[rows, kcols].astype(F32) * (dk ** -0.5)
            k = k_ref[rows, kcols].astype(F32)
            v = v_ref[rows, vcols]
            q_dec = q * jnp.exp(b)
            k_inv = (k * jnp.exp(-b)).astype(BF16)
            chunk = [slice(j * c, (j + 1) * c) for j in range(ncg)]
            b_last = [b[(j + 1) * c - 1:(j + 1) * c, :] for j in range(ncg)]
            before = [jnp.zeros_like(b_last[0])]
            for j in range(ncg):
                before.append(before[j] + b_last[j])
            k_to_end = [k[chunk[j], :] * jnp.exp(b_last[j] - b[chunk[j], :]) for j in range(ncg)]

            q_grp = jnp.concatenate(
                [q_dec[chunk[j], :] * jnp.exp(before[j]) for j in range(ncg)], axis=0).astype(BF16)
            o = lax.dot_general(q_grp, state_t.astype(BF16), contract_last, preferred_element_type=F32)

            q_dec = q_dec.astype(BF16)
            attn = []
            for j in range(ncg):
                keys = [(k_to_end[i] * jnp.exp(before[j] - before[i + 1])).astype(BF16) for i in range(j)]
                keys += [k_inv[chunk[j], :]] + [zero_keys] * (ncg - 1 - j)
                keys = jnp.concatenate(keys, axis=0) if ncg > 1 else keys[0]
                scores = lax.dot_general(q_dec[chunk[j], :], keys, contract_last, preferred_element_type=F32)
                attn.append(jnp.where(visible[j], scores, 0.0).astype(BF16))
            attn = jnp.concatenate(attn, axis=0) if ncg > 1 else attn[0]
            o = o + jnp.dot(attn, v, preferred_element_type=F32)

            k_grp = jnp.concatenate(
                [(k_to_end[j] * jnp.exp(before[ncg] - before[j + 1])).astype(BF16) for j in range(ncg)], axis=0)
            state_t = state_t * jnp.exp(before[ncg]) + lax.dot_general(
                v, k_grp, contract_first, preferred_element_type=F32)

            o = o * lax.rsqrt(jnp.mean(jnp.square(o), axis=-1, keepdims=True) + LN_EPS) * ng_ref[:, vcols]
            store(rows, hd, o * g_ref[rows, vcols].astype(F32))
        st_ref[hd] = state_t


def _gla_kernel(q_ref, k_ref, v_ref, g_ref, a_ref, wg_ref, bg_ref, ng_ref, o_ref, st_ref):
    heads, dv, _ = st_ref.shape

    @pl.when(pl.program_id(2) == 0)
    def _():
        st_ref[...] = jnp.zeros_like(st_ref)

    def store(rows, head, o):
        o_ref[rows, _lane_block(head, dv)] = o.astype(o_ref.dtype)

    _gla_heads(range(heads), q_ref, k_ref, v_ref, g_ref, a_ref, wg_ref, bg_ref, ng_ref, st_ref, store)


def _gla(h2, col0, a_lr, w_gate, b_gate, norm_g, batch, ts):
    t = h2.shape[0]
    seq = t // batch
    nh = GLA_HEADS
    key_dim = w_gate.shape[1]
    dk = key_dim // nh
    val_dim = norm_g.shape[1]
    dv = val_dim // nh
    ts = min(ts, seq)
    ns = seq // ts
    hps = GLA_HEADS_PER_STEP
    wk, wv = hps * dk, hps * dv
    assert nh % hps == 0 and col0 % wk == 0 and (col0 + 2 * key_dim) % wv == 0
    qb = col0 // wk
    kb, vb, gb = qb + key_dim // wk, (col0 + 2 * key_dim) // wv, (col0 + 2 * key_dim + val_dim) // wv
    tok = lambda b, h, s: b * ns + s
    return pl.pallas_call(
        _gla_kernel,
        grid=(batch, nh // hps, ns),
        in_specs=[
            pl.BlockSpec((ts, wk), lambda b, h, s: (tok(b, h, s), qb + h)),
            pl.BlockSpec((ts, wk), lambda b, h, s: (tok(b, h, s), kb + h)),
            pl.BlockSpec((ts, wv), lambda b, h, s: (tok(b, h, s), vb + h)),
            pl.BlockSpec((ts, wv), lambda b, h, s: (tok(b, h, s), gb + h)),
            pl.BlockSpec((ts, LANES), lambda b, h, s: (tok(b, h, s), 0)),
            pl.BlockSpec((LANES, wk), lambda b, h, s: (0, h)),
            pl.BlockSpec((1, wk), lambda b, h, s: (0, h)),
            pl.BlockSpec((1, wv), lambda b, h, s: (0, h)),
        ],
        out_specs=pl.BlockSpec((ts, wv), lambda b, h, s: (tok(b, h, s), h)),
        out_shape=jax.ShapeDtypeStruct((t, val_dim), BF16),
        scratch_shapes=[pltpu.VMEM((hps, dv, dk), F32)],
        compiler_params=pltpu.CompilerParams(
            dimension_semantics=("parallel", "parallel", "arbitrary"),
            vmem_limit_bytes=_vmem_limit(hps * ts * (dk * 48 + dv * 16))),
        name="gla",
    )(h2, h2, h2, h2, a_lr, w_gate, b_gate, norm_g)


def _merge_kernel(s_ref, o_ref, wa_ref, wb_ref, ga_ref, gb_ref, out_ref):
    half = out_ref.shape[1] // 2
    for cols in (slice(0, half), slice(half, 2 * half)):
        ya = jnp.dot(s_ref[...], wa_ref[:, cols], preferred_element_type=F32)
        yb = jnp.dot(o_ref[...], wb_ref[:, cols], preferred_element_type=F32)
        out_ref[:, cols] = (ga_ref[:, cols].astype(F32) * ya
                            + gb_ref[:, cols].astype(F32) * yb).astype(out_ref.dtype)


def _merge(s, o, w_a, w_b, gates, gate_col0, tm, tn):
    t, ka = s.shape
    kb = o.shape[1]
    d = w_a.shape[1]
    tm, tn = min(tm, t), min(tn, d)
    gj = gate_col0 // tn
    nbytes = tm * (ka + kb) * 2 + (ka + kb) * tn * 2 + tm * tn * (2 * 2 + 2 + 8)
    return pl.pallas_call(
        _merge_kernel,
        grid=(t // tm, d // tn),
        in_specs=[
            pl.BlockSpec((tm, ka), lambda i, j: (i, 0)),
            pl.BlockSpec((tm, kb), lambda i, j: (i, 0)),
            pl.BlockSpec((ka, tn), lambda i, j: (0, j)),
            pl.BlockSpec((kb, tn), lambda i, j: (0, j)),
            pl.BlockSpec((tm, tn), lambda i, j: (i, gj + j)),
            pl.BlockSpec((tm, tn), lambda i, j: (i, gj + j + d // tn)),
        ],
        out_specs=pl.BlockSpec((tm, tn), lambda i, j: (i, j)),
        out_shape=jax.ShapeDtypeStruct((t, d), BF16),
        compiler_params=pltpu.CompilerParams(
            dimension_semantics=("parallel", "arbitrary"), vmem_limit_bytes=_vmem_limit(nbytes)),
        name="merge",
    )(s, o, w_a, w_b, gates, gates)


def _route(logits):
    lane = lax.broadcasted_iota(jnp.int32, logits.shape, 1).astype(F32)
    neg = float("-inf")
    big = float(LANES)
    gl = jnp.where(lane < N_GROUPS, logits, neg)
    gmax = jnp.max(gl, axis=1, keepdims=True)
    gidx = jnp.min(jnp.where(gl == gmax, lane, big), axis=1, keepdims=True)
    p_group = 1.0 / jnp.sum(jnp.exp(gl - gmax), axis=1, keepdims=True)
    lo = N_GROUPS + EXPERTS_PER_GROUP * gidx
    el = jnp.where((lane >= lo) & (lane < lo + EXPERTS_PER_GROUP), logits, neg)
    v1 = jnp.max(el, axis=1, keepdims=True)
    i1 = jnp.min(jnp.where(el == v1, lane, big), axis=1, keepdims=True)
    el2 = jnp.where(lane == i1, neg, el)
    v2 = jnp.max(el2, axis=1, keepdims=True)
    i2 = jnp.min(jnp.where(el2 == v2, lane, big), axis=1, keepdims=True)
    t = jnp.exp(v2 - v1)
    w1 = p_group / (1.0 + t)
    w2 = p_group * t / (1.0 + t)
    return jnp.where(lane == 0, i1 - N_GROUPS,
                     jnp.where(lane == 1, i2 - N_GROUPS,
                               jnp.where(lane == 2, w1, jnp.where(lane == 3, w2, 0.0))))


def _expert_hits(route):
    lane = lax.broadcasted_iota(jnp.int32, route.shape, 1).astype(F32)
    return [lane == route[:, k:k + 1] for k in range(TOP_K)]


def _out_kernel(m_ref, w_ref, x_ref, g_ref, b_ref, wr_ref, br_ref, h_ref, hs_ref, r_ref, cnt_ref, mix_ref):
    i = pl.program_id(0)

    @pl.when(i == 0)
    def _():
        cnt_ref[...] = jnp.zeros_like(cnt_ref)
        mix_ref[1] = jnp.zeros(mix_ref.shape[1:], F32)

    has_prev = (i > 0).astype(F32)
    sub = m_ref.shape[0] // OUT_SUBTILES

    def step(cur):
        for si in range(OUT_SUBTILES):
            rows = slice(si * sub, (si + 1) * sub)
            mix_ref[cur, rows, :] = jnp.dot(m_ref[rows, :], w_ref[...], preferred_element_type=F32)
            h = _layer_norm(DEEPNORM_ALPHA * x_ref[rows, :] + mix_ref[1 - cur, rows, :], g_ref[...], b_ref[...])
            h_ref[rows, :] = h
            _store_slabs(hs_ref, si * sub, h)
            logits = jnp.dot(h.astype(BF16), wr_ref[...], preferred_element_type=F32) + br_ref[...]
            route = _route(logits)
            r_ref[rows, :] = route
            cnt_ref[...] += has_prev * sum(
                jnp.sum(hit.astype(F32), axis=0, keepdims=True) for hit in _expert_hits(route))

    for parity in range(2):
        pl.when(lax.rem(i, 2) == parity)(functools.partial(step, parity))


def _out(merged, w_out, x, ln_g, ln_b, w_r, b_r, tm):
    t, d = x.shape
    tm = min(tm, t)
    n = t // tm
    nbytes = d * d * 2 + tm * d * (2 + 4 + 4 + 2 + 8 + 8) + d * LANES * 2
    cur = lambda i: (jnp.minimum(i, n - 1), 0)
    prev = lambda i: (jnp.maximum(i - 1, 0), 0)
    return pl.pallas_call(
        _out_kernel,
        grid=(n + 1,),
        in_specs=[
            pl.BlockSpec((tm, d), cur),
            pl.BlockSpec((d, d), lambda i: (0, 0)),
            pl.BlockSpec((tm, d), prev),
            pl.BlockSpec((1, d), lambda i: (0, 0)),
            pl.BlockSpec((1, d), lambda i: (0, 0)),
            pl.BlockSpec((d, LANES), lambda i: (0, 0)),
            pl.BlockSpec((1, LANES), lambda i: (0, 0)),
        ],
        out_specs=[pl.BlockSpec((tm, d), prev),
                   pl.BlockSpec((tm * SLAB_ROWS, LANES), prev),
                   pl.BlockSpec((tm, LANES), prev),
                   pl.BlockSpec((1, LANES), lambda i: (0, 0))],
        out_shape=[jax.ShapeDtypeStruct((t, d), F32),
                   jax.ShapeDtypeStruct((t * SLAB_ROWS, LANES), jnp.uint32),
                   jax.ShapeDtypeStruct((t, LANES), F32),
                   jax.ShapeDtypeStruct((1, LANES), F32)],
        scratch_shapes=[pltpu.VMEM((2, tm, d), F32)],
        compiler_params=pltpu.CompilerParams(
            dimension_semantics=("arbitrary",), vmem_limit_bytes=_vmem_limit(nbytes)),
        name="out_ln_route",
    )(merged, w_out, x, ln_g, ln_b, w_r, b_r)


def _rank_kernel(r_ref, cnt_ref, dest_ref, next_ref):
    rows = r_ref.shape[0]

    @pl.when(pl.program_id(0) == 0)
    def _():
        blocks = jnp.floor((cnt_ref[...] + (MOE_ROWS - 1)) * (1.0 / MOE_ROWS))
        k = lax.broadcasted_iota(jnp.int32, (LANES, LANES), 0)
        e = lax.broadcasted_iota(jnp.int32, (LANES, LANES), 1)
        blocks8 = jnp.broadcast_to(blocks, (8, LANES)).astype(BF16)
        first_block = jnp.dot(blocks8, (k < e).astype(BF16), preferred_element_type=F32)
        next_ref[...] = first_block[0:1, :] * MOE_ROWS

    hits = _expert_hits(r_ref[...])
    cnt = sum(hit.astype(F32) for hit in hits)
    row = lax.broadcasted_iota(jnp.int32, (rows, rows), 0)
    col = lax.broadcasted_iota(jnp.int32, (rows, rows), 1)
    earlier = (row > col).astype(BF16)
    slot = jnp.dot(earlier, cnt.astype(BF16), preferred_element_type=F32) + next_ref[...]
    dest = [jnp.sum(jnp.where(hit, slot, 0.0), axis=1, keepdims=True) for hit in hits]
    lane = lax.broadcasted_iota(jnp.int32, (rows, LANES), 1)
    by_token = jnp.where(lane == 0, dest[0], jnp.where(lane == 1, dest[1], 0.0))
    dest_ref[0] = jnp.transpose(by_token)[:TOP_K, :].astype(jnp.int32)
    next_ref[...] += jnp.sum(cnt, axis=0, keepdims=True)


def _rank(route, counts, tm):
    t = route.shape[0]
    tm = min(tm, t)
    return pl.pallas_call(
        _rank_kernel,
        grid=(t // tm,),
        in_specs=[pl.BlockSpec((tm, LANES), lambda i: (i, 0)), pl.BlockSpec((1, LANES), lambda i: (0, 0))],
        out_specs=pl.BlockSpec((1, TOP_K, tm), lambda i: (i, 0, 0)),
        out_shape=jax.ShapeDtypeStruct((t // tm, TOP_K, tm), jnp.int32),
        scratch_shapes=[pltpu.VMEM((1, LANES), F32)],
        compiler_params=pltpu.CompilerParams(dimension_semantics=("arbitrary",)),
        name="rank",
    )(route, counts)


def _slab_copy(src, src_row, dst, dst_row, sem):
    s0 = pl.multiple_of(src_row * SLAB_ROWS, SLAB_ROWS)
    d0 = pl.multiple_of(dst_row * SLAB_ROWS, SLAB_ROWS)
    return pltpu.make_async_copy(src.at[pl.ds(s0, SLAB_ROWS), :], dst.at[pl.ds(d0, SLAB_ROWS), :], sem)


def _dispatch_kernel(zrow_ref, dest_ref, hs_ref, xs_ref, buf_ref, zero_ref, lsem, ssem, zsem, *, n_steps):
    i = pl.program_id(0)
    rows = buf_ref.shape[1] // SLAB_ROWS
    zrows = zero_ref.shape[0]

    def zero_copy(e):
        start_row = pl.multiple_of(jnp.maximum(zrow_ref[e], 0) * SLAB_ROWS, zrows)
        return pltpu.make_async_copy(zero_ref, xs_ref.at[pl.ds(start_row, zrows), :], zsem)

    def load(tile, slot):
        src = hs_ref.at[pl.ds(pl.multiple_of(tile * rows * SLAB_ROWS, SLAB_ROWS), rows * SLAB_ROWS), :]
        return pltpu.make_async_copy(src, buf_ref.at[slot], lsem.at[slot])

    def drain(slot):
        for k in range(TOP_K):
            pltpu.make_async_copy(buf_ref.at[slot], xs_ref.at[pl.ds(0, rows * SLAB_ROWS), :], ssem.at[slot]).wait()

    @pl.when(i == 0)
    def _():
        zero_ref[...] = jnp.zeros_like(zero_ref)

        def start(e, carry):
            @pl.when(zrow_ref[e] >= 0)
            def _():
                zero_copy(e).start()
            return carry

        def wait(e, carry):
            @pl.when(zrow_ref[e] >= 0)
            def _():
                zero_copy(e).wait()
            return carry

        lax.fori_loop(0, zrow_ref.shape[0], start, 0)
        for ahead in range(min(DISPATCH_SLOTS - 1, n_steps)):
            load(ahead, ahead).start()
        lax.fori_loop(0, zrow_ref.shape[0], wait, 0)

    def step(slot):
        load(i, slot).wait()

        def start(r, carry):
            for k in range(TOP_K):
                _slab_copy(buf_ref.at[slot], r, xs_ref, dest_ref[0, k, r], ssem.at[slot]).start(priority=k)
            return carry

        lax.fori_loop(0, rows, start, 0, unroll=ROW_DMA_UNROLL)
        free = (slot + DISPATCH_SLOTS - 1) % DISPATCH_SLOTS

        @pl.when(i > 0)
        def _():
            drain(free)

        @pl.when(i + DISPATCH_SLOTS - 1 < n_steps)
        def _():
            load(i + DISPATCH_SLOTS - 1, free).start()

        @pl.when(i == n_steps - 1)
        def _():
            drain(slot)

    for slot in range(DISPATCH_SLOTS):
        pl.when(lax.rem(i, DISPATCH_SLOTS) == slot)(functools.partial(step, slot))


def _dispatch(hs, dest, zrow, n_rows):
    n, _, tm = dest.shape
    grid_spec = pltpu.PrefetchScalarGridSpec(
        num_scalar_prefetch=1,
        grid=(n,),
        in_specs=[
            pl.BlockSpec((1, TOP_K, tm), lambda i, z: (i, 0, 0), memory_space=pltpu.SMEM),
            pl.BlockSpec(memory_space=pl.ANY),
        ],
        out_specs=pl.BlockSpec(memory_space=pl.ANY),
        scratch_shapes=[pltpu.VMEM((DISPATCH_SLOTS, tm * SLAB_ROWS, LANES), jnp.uint32),
                        pltpu.VMEM((MOE_ROWS * SLAB_ROWS, LANES), jnp.uint32),
                        pltpu.SemaphoreType.DMA((DISPATCH_SLOTS,)), pltpu.SemaphoreType.DMA((DISPATCH_SLOTS,)),
                        pltpu.SemaphoreType.DMA],
    )
    return pl.pallas_call(
        functools.partial(_dispatch_kernel, n_steps=n),
        grid_spec=grid_spec,
        out_shape=jax.ShapeDtypeStruct((n_rows * SLAB_ROWS, LANES), jnp.uint32),
        compiler_params=pltpu.CompilerParams(dimension_semantics=("arbitrary",)),
        name="dispatch",
    )(zrow, dest, hs)


def _expert_kernel(be_ref, nx_ref, nu_ref, x_ref, w1_ref, w3_ref, w2_ref, y_ref,
                   w1s_ref, w3s_ref, w2s_ref, w1b_ref, w3b_ref, w2b_ref, wsem):
    i = pl.program_id(0)
    d = w1_ref.shape[1]
    rows = x_ref.shape[0] // SLAB_ROWS

    def fetch(e):
        return [pltpu.make_async_copy(src.at[e], dst, wsem.at[n])
                for n, (src, dst) in enumerate(((w1_ref, w1s_ref), (w3_ref, w3s_ref), (w2_ref, w2s_ref)))]

    @pl.when(i == 0)
    def _():
        for copy in fetch(be_ref[0]):
            copy.start(priority=1)

    @pl.when((i == 0) | (be_ref[i] != be_ref[jnp.maximum(i - 1, 0)]))
    def _():
        for copy in fetch(be_ref[i]):
            copy.wait()
        for s in range(SLAB_ROWS):
            for half, src0 in enumerate((s * LANES, d // 2 + s * LANES)):
                dst0 = (2 * s + half) * LANES
                w1b_ref[dst0:dst0 + LANES, :] = w1s_ref[src0:src0 + LANES, :].astype(BF16)
                w3b_ref[dst0:dst0 + LANES, :] = w3s_ref[src0:src0 + LANES, :].astype(BF16)
        w2b_ref[...] = w2s_ref[...].astype(BF16)

        @pl.when(nx_ref[i] >= 0)
        def _():
            for copy in fetch(nx_ref[i]):
                copy.start(priority=1)

    @pl.when(i < nu_ref[0])
    def _():
        x = jnp.concatenate([part.astype(BF16) for pair in _load_slabs(x_ref, rows) for part in pair], axis=1)
        a = jnp.dot(x, w1b_ref[...], preferred_element_type=F32)
        b = jnp.dot(x, w3b_ref[...], preferred_element_type=F32)
        mid = (a * _sigmoid(a) * b).astype(BF16)
        _store_slabs(y_ref, 0, jnp.dot(mid, w2b_ref[...], preferred_element_type=F32))

    @pl.when(i >= nu_ref[0])
    def _():
        y_ref[...] = jnp.zeros_like(y_ref)


def _experts(xs, block_expert, next_expert, n_used, w1, w3, w2):
    _, d, de = w1.shape
    nb = xs.shape[0] // (MOE_ROWS * SLAB_ROWS)
    nbytes = MOE_ROWS * d * (2 + 2 + 2 + 4) + 3 * d * de * (2 + 1) + MOE_ROWS * de * 12
    grid_spec = pltpu.PrefetchScalarGridSpec(
        num_scalar_prefetch=3,
        grid=(nb,),
        in_specs=[
            pl.BlockSpec((MOE_ROWS * SLAB_ROWS, LANES), lambda i, be, nx, nu: (jnp.minimum(i, nu[0] - 1), 0)),
            pl.BlockSpec(memory_space=pl.ANY),
            pl.BlockSpec(memory_space=pl.ANY),
            pl.BlockSpec(memory_space=pl.ANY),
        ],
        out_specs=pl.BlockSpec((MOE_ROWS * SLAB_ROWS, LANES), lambda i, be, nx, nu: (i, 0)),
        scratch_shapes=[pltpu.VMEM((d, de), F32), pltpu.VMEM((d, de), F32), pltpu.VMEM((de, d), F32),
                        pltpu.VMEM((d, de), BF16), pltpu.VMEM((d, de), BF16), pltpu.VMEM((de, d), BF16),
                        pltpu.SemaphoreType.DMA((3,))],
    )
    return pl.pallas_call(
        _expert_kernel,
        grid_spec=grid_spec,
        out_shape=jax.ShapeDtypeStruct(xs.shape, jnp.uint32),
        compiler_params=pltpu.CompilerParams(
            dimension_semantics=("arbitrary",), vmem_limit_bytes=_vmem_limit(nbytes)),
        name="experts",
    )(block_expert, next_expert, n_used, xs, w1, w3, w2)


def _combine_kernel(dest_ref, dnext_ref, h_ref, r_ref, y_ref, g_ref, b_ref, o_ref, buf_ref, sem, *, n_steps):
    i = pl.program_id(0)
    rows = h_ref.shape[0]
    chunk = min(rows, COMBINE_CHUNK)

    def gather(dref, slot, r, k):
        return _slab_copy(y_ref, dref[0, k, r], buf_ref.at[slot, k], r, sem.at[slot])

    def wait(slot):
        for k in range(TOP_K):
            pltpu.make_async_copy(y_ref.at[pl.ds(0, rows * SLAB_ROWS), :], buf_ref.at[slot, k],
                                  sem.at[slot]).wait()

    @pl.when(i == 0)
    def _():
        def start(r, carry):
            for k in range(TOP_K):
                gather(dest_ref, 0, r, k).start(priority=k)
            return carry

        lax.fori_loop(0, rows, start, 0, unroll=ROW_DMA_UNROLL)

    def step(slot):
        wait(slot)
        for c0 in range(0, rows, chunk):
            route = r_ref[c0:c0 + chunk, :]
            slabs = [_load_slabs(buf_ref.at[slot, k], chunk, c0) for k in range(TOP_K)]
            lo, hi = [], []
            for s in range(SLAB_ROWS):
                for half, out in enumerate((lo, hi)):
                    out.append(sum(route[:, 2 + k:3 + k] * slabs[k][s][half] for k in range(TOP_K)))
            moe = jnp.concatenate(lo + hi, axis=1)
            o_ref[c0:c0 + chunk, :] = _layer_norm(
                DEEPNORM_ALPHA * h_ref[c0:c0 + chunk, :] + moe, g_ref[...], b_ref[...])
            for r in range(c0, c0 + chunk):
                for k in range(TOP_K):
                    gather(dnext_ref, 1 - slot, r, k).start(priority=k)

    for parity in range(2):
        pl.when(lax.rem(i, 2) == parity)(functools.partial(step, parity))

    @pl.when(i == n_steps - 1)
    def _():
        wait(n_steps % 2)


def _combine(h, route, dest, y, ln_g, ln_b):
    t, d = h.shape
    n, _, tm = dest.shape
    return pl.pallas_call(
        functools.partial(_combine_kernel, n_steps=n),
        grid=(n,),
        in_specs=[
            pl.BlockSpec((1, TOP_K, tm), lambda i: (i, 0, 0), memory_space=pltpu.SMEM),
            pl.BlockSpec((1, TOP_K, tm), lambda i: (jnp.minimum(i + 1, n - 1), 0, 0), memory_space=pltpu.SMEM),
            pl.BlockSpec((tm, d), lambda i: (i, 0)),
            pl.BlockSpec((tm, LANES), lambda i: (i, 0)),
            pl.BlockSpec(memory_space=pl.ANY),
            pl.BlockSpec((1, d), lambda i: (0, 0)),
            pl.BlockSpec((1, d), lambda i: (0, 0)),
        ],
        out_specs=pl.BlockSpec((tm, d), lambda i: (i, 0)),
        out_shape=jax.ShapeDtypeStruct((t, d), F32),
        scratch_shapes=[pltpu.VMEM((2, TOP_K, tm * SLAB_ROWS, LANES), jnp.uint32), pltpu.SemaphoreType.DMA((2,))],
        compiler_params=pltpu.CompilerParams(dimension_semantics=("arbitrary",)),
        name="combine",
    )(dest, dest, h, route, y, ln_g, ln_b)


def kernel(x, w_in, w_gate_a2, b_gate_a, sgu_ln_g, sgu_ln_b, sgu_w_s, sgu_b_s, gla_norm_g, w_branch_a, w_branch_b, w_merge, b_merge, w_out, ln1_g, ln1_b, w_router_group, b_router_group, w_router_expert, b_router_expert, w_exp_gate, w_exp_up, w_exp_down, ln2_g, ln2_b):
    batch, seq, d = x.shape
    t = batch * seq
    assert w_in.shape[0] == 1, "one layer"
    assert seq % SGU_CHUNK == 0 and seq % GLA_CHUNK == 0 and t % MOE_ROWS == 0
    sgu_width = sgu_ln_g.shape[1]
    key_dim = w_gate_a2.shape[2]
    val_dim = gla_norm_g.shape[1]
    rank = w_gate_a2.shape[1]
    xf = x.reshape(t, d)
    n_uv, n_h2 = 2 * sgu_width, 2 * key_dim + 2 * val_dim

    w_in_t = jnp.swapaxes(w_in.reshape(w_in.shape[1:]), 0, 1).astype(BF16)
    w_a_t = jnp.pad(w_in_t[n_uv + n_h2:, :], ((0, LANES - rank), (0, 0)))
    assert n_uv == 2 * sgu_w_s.shape[1] * SGU_CHUNK
    p, a_lr, s = _proj(xf, w_in_t, w_merge[0].astype(BF16), b_merge, w_a_t,
                       sgu_w_s[0], sgu_ln_g[0], sgu_ln_b[0], sgu_b_s[0], n_h2 - val_dim, val_dim, 1024)

    w_gate = jnp.pad(w_gate_a2[0].astype(BF16), ((0, LANES - rank), (0, 0)))
    o = _gla(p, 0, a_lr, w_gate, b_gate_a, gla_norm_g, batch, 2048)
    merged = _merge(s, o, w_branch_a[0].astype(BF16), w_branch_b[0].astype(BF16), p, n_h2, 1024, 1024)

    w_r = jnp.concatenate([w_router_group[0], w_router_expert[0]], axis=1)
    n_r = w_r.shape[1]
    w_r = jnp.pad(w_r, ((0, 0), (0, LANES - n_r))).astype(BF16)
    b_r = jnp.pad(jnp.concatenate([b_router_group, b_router_expert], axis=1), ((0, 0), (0, LANES - n_r)))
    h1, h1_slabs, route, counts = _out(merged, w_out[0].astype(BF16), xf, ln1_g, ln1_b, w_r, b_r, 512)

    assert t * TOP_K // MOE_ROWS <= 256, "per-expert block counts must stay exact in bf16"
    dest = _rank(route, counts, MOE_TOKEN_TILE)
    blocks_per_expert = (counts[0, :N_EXPERTS].astype(jnp.int32) + MOE_ROWS - 1) // MOE_ROWS
    block_ends = jnp.cumsum(blocks_per_expert)
    n_blocks = t * TOP_K // MOE_ROWS + N_EXPERTS
    n_used = block_ends[-1:]
    block_ids = jnp.minimum(jnp.arange(n_blocks, dtype=jnp.int32), n_used[0] - 1)
    experts = jnp.arange(N_EXPERTS, dtype=jnp.int32)
    block_expert = jnp.minimum(
        jnp.sum(block_ends[None, :] <= block_ids[:, None], axis=1), N_EXPERTS - 1).astype(jnp.int32)
    later = (experts[None, :] > block_expert[:, None]) & (blocks_per_expert[None, :] > 0)
    next_expert = jnp.min(jnp.where(later, experts[None, :], N_EXPERTS), axis=1)
    next_expert = jnp.where(next_expert < N_EXPERTS, next_expert, -1).astype(jnp.int32)
    tail_ids = n_used[0] + experts
    zrow = jnp.concatenate([
        jnp.where(blocks_per_expert > 0, (block_ends - 1) * MOE_ROWS, -1),
        jnp.where(tail_ids < n_blocks, tail_ids * MOE_ROWS, -1)]).astype(jnp.int32)

    xs = _dispatch(h1_slabs, dest, zrow, n_blocks * MOE_ROWS)
    y = _experts(xs, block_expert, next_expert, n_used.astype(jnp.int32),
                 w_exp_gate[0], w_exp_up[0], w_exp_down[0])
    out = _combine(h1, route, dest, y, ln2_g, ln2_b)
    return out.reshape(batch, seq, d)
```

```python
import functools

import jax
import jax.numpy as jnp
from jax import lax
from jax.experimental import pallas as pl
from jax.experimental.pallas import tpu as pltpu

F32 = jnp.float32
BF16 = jnp.bfloat16

SGU_CHUNK = 128
GLA_HEADS = 4
GLA_CHUNK = 64
GLA_GATE_NORM = 16.0
GLA_HEADS_PER_STEP = 2
GLA_GROUP_CHUNKS = 4
N_GROUPS = 8
EXPERTS_PER_GROUP = 8
N_EXPERTS = N_GROUPS * EXPERTS_PER_GROUP
TOP_K = 2
LN_EPS = 1e-5
DEEPNORM_ALPHA = 2.0 ** 0.25

LANES = 128
MOE_ROWS = 256
MOE_TOKEN_TILE = 512
ROW_DMA_UNROLL = 8
DISPATCH_SLOTS = 3
PROJ_SUBTILES = 2
OUT_SUBTILES = 2
COMBINE_SLOTS = 3
COMBINE_CHUNK = 32
V7X_VMEM_BYTES = 64 * 2 ** 20


def _vmem_limit(nbytes):
    return int(min(max(2 * nbytes, 16 * 2 ** 20), V7X_VMEM_BYTES - 8 * 2 ** 20))


def _layer_norm(y, g, b):
    mu = jnp.mean(y, axis=-1, keepdims=True)
    var = jnp.mean(jnp.square(y - mu), axis=-1, keepdims=True)
    return (y - mu) * lax.rsqrt(var + LN_EPS) * g + b


def _gelu(x):
    return 0.5 * x * (1.0 + lax.erf(x * (2.0 ** -0.5)))


def _sigmoid(x):
    return 0.5 * (jnp.tanh(0.5 * x) + 1.0)


SLAB_ROWS = 8
HIGH_HALF = 0xFFFF0000


def _bf16_bits(x):
    return lax.bitcast_convert_type(x.astype(BF16).astype(F32), jnp.uint32)


def _store_slabs(ref, row0, x):
    rows, d = x.shape
    assert d == 2 * SLAB_ROWS * LANES
    for s in range(SLAB_ROWS):
        lo = _bf16_bits(x[:, s * LANES:(s + 1) * LANES])
        hi = _bf16_bits(x[:, d // 2 + s * LANES:d // 2 + (s + 1) * LANES])
        word = jnp.right_shift(lo, jnp.uint32(16)) | (hi & jnp.uint32(HIGH_HALF))
        ref[pl.ds(row0 * SLAB_ROWS + s, rows, stride=SLAB_ROWS), :] = word


def _load_slabs(ref, rows, row0=0):
    out = []
    for s in range(SLAB_ROWS):
        word = ref[pl.ds(row0 * SLAB_ROWS + s, rows, stride=SLAB_ROWS), :]
        lo = lax.bitcast_convert_type(jnp.left_shift(word, jnp.uint32(16)), F32)
        hi = lax.bitcast_convert_type(word & jnp.uint32(HIGH_HALF), F32)
        out.append((lo, hi))
    return out


def _proj_kernel(x_ref, w_ref, wm_ref, bm_ref, wa_ref, ws_ref, lg_ref, lb_ref, bs_ref,
                 o_ref, a_ref, s_ref, xb_ref, u_ref, v_ref, *, plain_blocks, silu_blocks, n_blocks):
    j = pl.program_id(1)
    sgu_blocks = 2
    silu_start = sgu_blocks + plain_blocks
    sigmoid_start = silu_start + silu_blocks
    contract_last = (((1,), (1,)), ((), ()))
    tm, tn = o_ref.shape
    sub = tn // PROJ_SUBTILES
    c = SGU_CHUNK
    groups = tn // c
    assert sgu_blocks + groups <= n_blocks

    def x_dot_w(cols):
        return lax.dot_general(xb_ref[...], w_ref[cols, :], contract_last, preferred_element_type=F32)

    def x_dot_wm(cols):
        return jnp.dot(xb_ref[...], wm_ref[:, cols], preferred_element_type=F32)

    def sgu_group():
        g = jnp.clip(j - sgu_blocks, 0, groups - 1)
        cols = pl.ds(pl.multiple_of(g * c, c), c)
        row = lax.broadcasted_iota(jnp.int32, (c, c), 0)
        col = lax.broadcasted_iota(jnp.int32, (c, c), 1)
        w = jnp.where(row >= col, ws_ref[g], 0.0).astype(BF16)
        ln_g, ln_b, bias = lg_ref[g], lb_ref[g], bs_ref[g]
        for ci in range(tm // c):
            rows = slice(ci * c, (ci + 1) * c)
            vn = _layer_norm(v_ref[rows, cols].astype(F32), ln_g, ln_b)
            mixed = jnp.dot(w, vn.astype(BF16), preferred_element_type=F32) + bias
            s_ref[rows, cols] = (u_ref[rows, cols].astype(F32) * mixed).astype(s_ref.dtype)

    def block(store):
        for cols in (slice(si * sub, (si + 1) * sub) for si in range(PROJ_SUBTILES)):
            store(cols)

    def store_u(cols):
        u_ref[:, cols] = _gelu(x_dot_w(cols)).astype(u_ref.dtype)

    def store_v(cols):
        v_ref[:, cols] = _gelu(x_dot_w(cols)).astype(v_ref.dtype)

    def store_plain(cols):
        o_ref[:, cols] = x_dot_w(cols).astype(o_ref.dtype)

    def store_silu(cols):
        acc = x_dot_w(cols)
        o_ref[:, cols] = (acc * _sigmoid(acc)).astype(o_ref.dtype)

    def store_sigmoid(cols):
        o_ref[:, cols] = _sigmoid(x_dot_wm(cols) + bm_ref[:, cols]).astype(o_ref.dtype)

    @pl.when(j == 0)
    def _():
        xb_ref[...] = x_ref[...].astype(BF16)
        a_ref[...] = lax.dot_general(
            xb_ref[...], wa_ref[...], contract_last, preferred_element_type=F32).astype(a_ref.dtype)
        block(store_u)

    @pl.when(j == 1)
    def _():
        block(store_v)

    @pl.when((j >= sgu_blocks) & (j < silu_start))
    def _():
        block(store_plain)
        sgu_group()

    @pl.when((j >= silu_start) & (j < sigmoid_start))
    def _():
        block(store_silu)
        sgu_group()

    @pl.when(j >= sigmoid_start)
    def _():
        block(store_sigmoid)
        sgu_group()


def _proj(x, w_t, w_m, b_m, w_a_t, w_s, ln_g, ln_b, b_s, n_plain, n_silu, tm):
    m, k = x.shape
    ng, c, _ = w_s.shape
    width = tn = ng * c
    n_m = w_m.shape[1]
    tm = min(tm, m)
    assert n_plain % tn == 0 and n_silu % tn == 0 and n_m % tn == 0 and tm % c == 0
    in_blocks = 2 + (n_plain + n_silu) // tn
    p_block = lambda i, j: (i, jnp.maximum(j - 2, 0))
    nbytes = tm * k * (4 + 1) + 2 * k * tn * 2 + tm * tn * (2 + 2 + 2 + 1) + k * LANES * 2
    return pl.pallas_call(
        functools.partial(_proj_kernel, plain_blocks=n_plain // tn, silu_blocks=n_silu // tn,
                          n_blocks=in_blocks + n_m // tn),
        grid=(m // tm, in_blocks + n_m // tn),
        in_specs=[
            pl.BlockSpec((tm, k), lambda i, j: (i, 0)),
            pl.BlockSpec((tn, k), lambda i, j: (jnp.minimum(j, in_blocks - 1), 0)),
            pl.BlockSpec((k, tn), lambda i, j: (0, jnp.maximum(j - in_blocks, 0))),
            pl.BlockSpec((1, tn), lambda i, j: (0, jnp.maximum(j - in_blocks, 0))),
            pl.BlockSpec((LANES, k), lambda i, j: (0, 0)),
            pl.BlockSpec((ng, c, c), lambda i, j: (0, 0, 0)),
            pl.BlockSpec((ng, 1, c), lambda i, j: (0, 0, 0)),
            pl.BlockSpec((ng, 1, c), lambda i, j: (0, 0, 0)),
            pl.BlockSpec((ng, c, 1), lambda i, j: (0, 0, 0)),
        ],
        out_specs=[pl.BlockSpec((tm, tn), p_block), pl.BlockSpec((tm, LANES), lambda i, j: (i, 0)),
                   pl.BlockSpec((tm, width), lambda i, j: (i, 0))],
        out_shape=[jax.ShapeDtypeStruct((m, n_plain + n_silu + n_m), BF16),
                   jax.ShapeDtypeStruct((m, LANES), BF16), jax.ShapeDtypeStruct((m, width), BF16)],
        scratch_shapes=[pltpu.VMEM((tm, k), BF16), pltpu.VMEM((tm, width), BF16), pltpu.VMEM((tm, width), BF16)],
        compiler_params=pltpu.CompilerParams(
            dimension_semantics=("parallel", "arbitrary"), vmem_limit_bytes=_vmem_limit(nbytes)),
        name="proj",
    )(x, w_t, w_m, b_m, w_a_t, w_s, ln_g.reshape(ng, 1, c), ln_b.reshape(ng, 1, c), b_s.reshape(ng, c, 1))


def _lane_block(index, width):
    return slice(index * width, (index + 1) * width)


def _gla_heads(heads, q_ref, k_ref, v_ref, g_ref, a_ref, wg_ref, bg_ref, ng_ref, st_ref, store):
    c = GLA_CHUNK
    ts = q_ref.shape[0]
    _, dv, dk = st_ref.shape
    grp = min(ts, GLA_GROUP_CHUNKS * c)
    ncg = grp // c
    contract_last = (((1,), (1,)), ((), ()))
    contract_first = (((0,), (0,)), ((), ()))

    row = lax.broadcasted_iota(jnp.int32, (grp, grp), 0)
    col = lax.broadcasted_iota(jnp.int32, (grp, grp), 1)
    shift = c.bit_length() - 1
    ones_tril = ((row >= col) & (jnp.right_shift(row, shift) == jnp.right_shift(col, shift))).astype(BF16)
    qrow = lax.broadcasted_iota(jnp.int32, (c, grp), 0)
    kcol = lax.broadcasted_iota(jnp.int32, (c, grp), 1)
    visible = [(kcol < j * c) | ((kcol < (j + 1) * c) & (kcol - j * c <= qrow)) for j in range(ncg)]
    zero_keys = jnp.zeros((c, dk), BF16)

    for hd in heads:
        kcols = _lane_block(hd, dk)
        vcols = _lane_block(hd, dv)
        z = jnp.dot(a_ref[...], wg_ref[:, kcols], preferred_element_type=F32) + bg_ref[:, kcols]
        log_a = (jnp.minimum(z, 0.0) - jnp.log1p(jnp.exp(-jnp.abs(z)))) * (1.0 / GLA_GATE_NORM)
        la_hi = log_a.astype(BF16)
        la_lo = (log_a - la_hi.astype(F32)).astype(BF16)
        la_split = jnp.concatenate([la_hi, la_lo], axis=1)
        state_t = st_ref[hd]
        for gi in range(ts // grp):
            rows = slice(gi * grp, (gi + 1) * grp)
            r = jnp.dot(ones_tril, la_split[rows, :], preferred_element_type=F32)
            b = r[:, :dk] + r[:, dk:]
            q = q_ref[rows, kcols].astype(F32) * (dk ** -0.5)
            k = k_ref[rows, kcols].astype(F32)
            v = v_ref[rows, vcols]
            q_dec = q * jnp.exp(b)
            k_inv = (k * jnp.exp(-b)).astype(BF16)
            chunk = [slice(j * c, (j + 1) * c) for j in range(ncg)]
            b_last = [b[(j + 1) * c - 1:(j + 1) * c, :] for j in range(ncg)]
            before = [jnp.zeros_like(b_last[0])]
            for j in range(ncg):
                before.append(before[j] + b_last[j])
            k_to_end = [k[chunk[j], :] * jnp.exp(b_last[j] - b[chunk[j], :]) for j in range(ncg)]

            q_grp = jnp.concatenate(
                [q_dec[chunk[j], :] * jnp.exp(before[j]) for j in range(ncg)], axis=0).astype(BF16)
            o = lax.dot_general(q_grp, state_t.astype(BF16), contract_last, preferred_element_type=F32)

            q_dec = q_dec.astype(BF16)
            attn = []
            for j in range(ncg):
                keys = [(k_to_end[i] * jnp.exp(before[j] - before[i + 1])).astype(BF16) for i in range(j)]
                keys += [k_inv[chunk[j], :]] + [zero_keys] * (ncg - 1 - j)
                keys = jnp.concatenate(keys, axis=0) if ncg > 1 else keys[0]
                scores = lax.dot_general(q_dec[chunk[j], :], keys, contract_last, preferred_element_type=F32)
                attn.append(jnp.where(visible[j], scores, 0.0).astype(BF16))
            attn = jnp.concatenate(attn, axis=0) if ncg > 1 else attn[0]
            o = o + jnp.dot(attn, v, preferred_element_type=F32)

            k_grp = jnp.concatenate(
                [(k_to_end[j] * jnp.exp(before[ncg] - before[j + 1])).astype(BF16) for j in range(ncg)], axis=0)
            state_t = state_t * jnp.exp(before[ncg]) + lax.dot_general(
                v, k_grp, contract_first, preferred_element_type=F32)

            o = o * lax.rsqrt(jnp.mean(jnp.square(o), axis=-1, keepdims=True) + LN_EPS) * ng_ref[:, vcols]
            store(rows, hd, o * g_ref[rows, vcols].astype(F32))
        st_ref[hd] = state_t


def _gla_kernel(q_ref, k_ref, v_ref, g_ref, a_ref, wg_ref, bg_ref, ng_ref, o_ref, st_ref):
    heads, dv, _ = st_ref.shape

    @pl.when(pl.program_id(2) == 0)
    def _():
        st_ref[...] = jnp.zeros_like(st_ref)

    def store(rows, head, o):
        o_ref[rows, _lane_block(head, dv)] = o.astype(o_ref.dtype)

    _gla_heads(range(heads), q_ref, k_ref, v_ref, g_ref, a_ref, wg_ref, bg_ref, ng_ref, st_ref, store)


def _gla(h2, col0, a_lr, w_gate, b_gate, norm_g, batch, ts):
    t = h2.shape[0]
    seq = t // batch
    nh = GLA_HEADS
    key_dim = w_gate.shape[1]
    dk = key_dim // nh
    val_dim = norm_g.shape[1]
    dv = val_dim // nh
    ts = min(ts, seq)
    ns = seq // ts
    hps = GLA_HEADS_PER_STEP
    wk, wv = hps * dk, hps * dv
    assert nh % hps == 0 and col0 % wk == 0 and (col0 + 2 * key_dim) % wv == 0
    qb = col0 // wk
    kb, vb, gb = qb + key_dim // wk, (col0 + 2 * key_dim) // wv, (col0 + 2 * key_dim + val_dim) // wv
    tok = lambda b, h, s: b * ns + s
    return pl.pallas_call(
        _gla_kernel,
        grid=(batch, nh // hps, ns),
        in_specs=[
            pl.BlockSpec((ts, wk), lambda b, h, s: (tok(b, h, s), qb + h)),
            pl.BlockSpec((ts, wk), lambda b, h, s: (tok(b, h, s), kb + h)),
            pl.BlockSpec((ts, wv), lambda b, h, s: (tok(b, h, s), vb + h)),
            pl.BlockSpec((ts, wv), lambda b, h, s: (tok(b, h, s), gb + h)),
            pl.BlockSpec((ts, LANES), lambda b, h, s: (tok(b, h, s), 0)),
            pl.BlockSpec((LANES, wk), lambda b, h, s: (0, h)),
            pl.BlockSpec((1, wk), lambda b, h, s: (0, h)),
            pl.BlockSpec((1, wv), lambda b, h, s: (0, h)),
        ],
        out_specs=pl.BlockSpec((ts, wv), lambda b, h, s: (tok(b, h, s), h)),
        out_shape=jax.ShapeDtypeStruct((t, val_dim), BF16),
        scratch_shapes=[pltpu.VMEM((hps, dv, dk), F32)],
        compiler_params=pltpu.CompilerParams(
            dimension_semantics=("parallel", "parallel", "arbitrary"),
            vmem_limit_bytes=_vmem_limit(hps * ts * (dk * 48 + dv * 16))),
        name="gla",
    )(h2, h2, h2, h2, a_lr, w_gate, b_gate, norm_g)


def _merge_kernel(s_ref, o_ref, wa_ref, wb_ref, ga_ref, gb_ref, out_ref):
    half = out_ref.shape[1] // 2
    for cols in (slice(0, half), slice(half, 2 * half)):
        ya = jnp.dot(s_ref[...], wa_ref[:, cols], preferred_element_type=F32)
        yb = jnp.dot(o_ref[...], wb_ref[:, cols], preferred_element_type=F32)
        out_ref[:, cols] = (ga_ref[:, cols].astype(F32) * ya
                            + gb_ref[:, cols].astype(F32) * yb).astype(out_ref.dtype)


def _merge(s, o, w_a, w_b, gates, gate_col0, tm, tn):
    t, ka = s.shape
    kb = o.shape[1]
    d = w_a.shape[1]
    tm, tn = min(tm, t), min(tn, d)
    gj = gate_col0 // tn
    nbytes = tm * (ka + kb) * 2 + (ka + kb) * tn * 2 + tm * tn * (2 * 2 + 2 + 8)
    return pl.pallas_call(
        _merge_kernel,
        grid=(t // tm, d // tn),
        in_specs=[
            pl.BlockSpec((tm, ka), lambda i, j: (i, 0)),
            pl.BlockSpec((tm, kb), lambda i, j: (i, 0)),
            pl.BlockSpec((ka, tn), lambda i, j: (0, j)),
            pl.BlockSpec((kb, tn), lambda i, j: (0, j)),
            pl.BlockSpec((tm, tn), lambda i, j: (i, gj + j)),
            pl.BlockSpec((tm, tn), lambda i, j: (i, gj + j + d // tn)),
        ],
        out_specs=pl.BlockSpec((tm, tn), lambda i, j: (i, j)),
        out_shape=jax.ShapeDtypeStruct((t, d), BF16),
        compiler_params=pltpu.CompilerParams(
            dimension_semantics=("parallel", "arbitrary"), vmem_limit_bytes=_vmem_limit(nbytes)),
        name="merge",
    )(s, o, w_a, w_b, gates, gates)


def _route(logits):
    lane = lax.broadcasted_iota(jnp.int32, logits.shape, 1).astype(F32)
    neg = float("-inf")
    big = float(LANES)
    gl = jnp.where(lane < N_GROUPS, logits, neg)
    gmax = jnp.max(gl, axis=1, keepdims=True)
    gidx = jnp.min(jnp.where(gl == gmax, lane, big), axis=1, keepdims=True)
    p_group = 1.0 / jnp.sum(jnp.exp(gl - gmax), axis=1, keepdims=True)
    lo = N_GROUPS + EXPERTS_PER_GROUP * gidx
    el = jnp.where((lane >= lo) & (lane < lo + EXPERTS_PER_GROUP), logits, neg)
    v1 = jnp.max(el, axis=1, keepdims=True)
    i1 = jnp.min(jnp.where(el == v1, lane, big), axis=1, keepdims=True)
    el2 = jnp.where(lane == i1, neg, el)
    v2 = jnp.max(el2, axis=1, keepdims=True)
    i2 = jnp.min(jnp.where(el2 == v2, lane, big), axis=1, keepdims=True)
    t = jnp.exp(v2 - v1)
    w1 = p_group / (1.0 + t)
    w2 = p_group * t / (1.0 + t)
    return jnp.where(lane == 0, i1 - N_GROUPS,
                     jnp.where(lane == 1, i2 - N_GROUPS,
                               jnp.where(lane == 2, w1, jnp.where(lane == 3, w2, 0.0))))


def _expert_hits(route):
    lane = lax.broadcasted_iota(jnp.int32, route.shape, 1).astype(F32)
    return [lane == route[:, k:k + 1] for k in range(TOP_K)]


def _out_kernel(m_ref, w_ref, x_ref, g_ref, b_ref, wr_ref, br_ref, h_ref, hs_ref, r_ref, cnt_ref, mix_ref):
    i = pl.program_id(0)

    @pl.when(i == 0)
    def _():
        cnt_ref[...] = jnp.zeros_like(cnt_ref)
        mix_ref[1] = jnp.zeros(mix_ref.shape[1:], F32)

    has_prev = (i > 0).astype(F32)
    sub = m_ref.shape[0] // OUT_SUBTILES

    def step(cur):
        for si in range(OUT_SUBTILES):
            rows = slice(si * sub, (si + 1) * sub)
            mix_ref[cur, rows, :] = jnp.dot(m_ref[rows, :], w_ref[...], preferred_element_type=F32)
            h = _layer_norm(DEEPNORM_ALPHA * x_ref[rows, :] + mix_ref[1 - cur, rows, :], g_ref[...], b_ref[...])
            h_ref[rows, :] = h
            _store_slabs(hs_ref, si * sub, h)
            logits = jnp.dot(h.astype(BF16), wr_ref[...], preferred_element_type=F32) + br_ref[...]
            route = _route(logits)
            r_ref[rows, :] = route
            cnt_ref[...] += has_prev * sum(
                jnp.sum(hit.astype(F32), axis=0, keepdims=True) for hit in _expert_hits(route))

    for parity in range(2):
        pl.when(lax.rem(i, 2) == parity)(functools.partial(step, parity))


def _out(merged, w_out, x, ln_g, ln_b, w_r, b_r, tm):
    t, d = x.shape
    tm = min(tm, t)
    n = t // tm
    nbytes = d * d * 2 + tm * d * (2 + 4 + 4 + 2 + 8 + 8) + d * LANES * 2
    cur = lambda i: (jnp.minimum(i, n - 1), 0)
    prev = lambda i: (jnp.maximum(i - 1, 0), 0)
    return pl.pallas_call(
        _out_kernel,
        grid=(n + 1,),
        in_specs=[
            pl.BlockSpec((tm, d), cur),
            pl.BlockSpec((d, d), lambda i: (0, 0)),
            pl.BlockSpec((tm, d), prev),
            pl.BlockSpec((1, d), lambda i: (0, 0)),
            pl.BlockSpec((1, d), lambda i: (0, 0)),
            pl.BlockSpec((d, LANES), lambda i: (0, 0)),
            pl.BlockSpec((1, LANES), lambda i: (0, 0)),
        ],
        out_specs=[pl.BlockSpec((tm, d), prev),
                   pl.BlockSpec((tm * SLAB_ROWS, LANES), prev),
                   pl.BlockSpec((tm, LANES), prev),
                   pl.BlockSpec((1, LANES), lambda i: (0, 0))],
        out_shape=[jax.ShapeDtypeStruct((t, d), F32),
                   jax.ShapeDtypeStruct((t * SLAB_ROWS, LANES), jnp.uint32),
                   jax.ShapeDtypeStruct((t, LANES), F32),
                   jax.ShapeDtypeStruct((1, LANES), F32)],
        scratch_shapes=[pltpu.VMEM((2, tm, d), F32)],
        compiler_params=pltpu.CompilerParams(
            dimension_semantics=("arbitrary",), vmem_limit_bytes=_vmem_limit(nbytes)),
        name="out_ln_route",
    )(merged, w_out, x, ln_g, ln_b, w_r, b_r)


def _rank_kernel(r_ref, cnt_ref, dest_ref, next_ref):
    rows = r_ref.shape[0]

    @pl.when(pl.program_id(0) == 0)
    def _():
        blocks = jnp.floor((cnt_ref[...] + (MOE_ROWS - 1)) * (1.0 / MOE_ROWS))
        k = lax.broadcasted_iota(jnp.int32, (LANES, LANES), 0)
        e = lax.broadcasted_iota(jnp.int32, (LANES, LANES), 1)
        blocks8 = jnp.broadcast_to(blocks, (8, LANES)).astype(BF16)
        first_block = jnp.dot(blocks8, (k < e).astype(BF16), preferred_element_type=F32)
        next_ref[...] = first_block[0:1, :] * MOE_ROWS

    hits = _expert_hits(r_ref[...])
    cnt = sum(hit.astype(F32) for hit in hits)
    row = lax.broadcasted_iota(jnp.int32, (rows, rows), 0)
    col = lax.broadcasted_iota(jnp.int32, (rows, rows), 1)
    earlier = (row > col).astype(BF16)
    slot = jnp.dot(earlier, cnt.astype(BF16), preferred_element_type=F32) + next_ref[...]
    dest = [jnp.sum(jnp.where(hit, slot, 0.0), axis=1, keepdims=True) for hit in hits]
    lane = lax.broadcasted_iota(jnp.int32, (rows, LANES), 1)
    by_token = jnp.where(lane == 0, dest[0], jnp.where(lane == 1, dest[1], 0.0))
    dest_ref[0] = jnp.transpose(by_token)[:TOP_K, :].astype(jnp.int32)
    next_ref[...] += jnp.sum(cnt, axis=0, keepdims=True)


def _rank(route, counts, tm):
    t = route.shape[0]
    tm = min(tm, t)
    return pl.pallas_call(
        _rank_kernel,
        grid=(t // tm,),
        in_specs=[pl.BlockSpec((tm, LANES), lambda i: (i, 0)), pl.BlockSpec((1, LANES), lambda i: (0, 0))],
        out_specs=pl.BlockSpec((1, TOP_K, tm), lambda i: (i, 0, 0)),
        out_shape=jax.ShapeDtypeStruct((t // tm, TOP_K, tm), jnp.int32),
        scratch_shapes=[pltpu.VMEM((1, LANES), F32)],
        compiler_params=pltpu.CompilerParams(dimension_semantics=("arbitrary",)),
        name="rank",
    )(route, counts)


def _slab_copy(src, src_row, dst, dst_row, sem):
    s0 = pl.multiple_of(src_row * SLAB_ROWS, SLAB_ROWS)
    d0 = pl.multiple_of(dst_row * SLAB_ROWS, SLAB_ROWS)
    return pltpu.make_async_copy(src.at[pl.ds(s0, SLAB_ROWS), :], dst.at[pl.ds(d0, SLAB_ROWS), :], sem)


def _dispatch_kernel(zrow_ref, dest_ref, hs_ref, xs_ref, buf_ref, zero_ref, lsem, ssem, zsem, *, n_steps):
    i = pl.program_id(0)
    rows = buf_ref.shape[1] // SLAB_ROWS
    zrows = zero_ref.shape[0]

    def zero_copy(e):
        start_row = pl.multiple_of(jnp.maximum(zrow_ref[e], 0) * SLAB_ROWS, zrows)
        return pltpu.make_async_copy(zero_ref, xs_ref.at[pl.ds(start_row, zrows), :], zsem)

    def load(tile, slot):
        src = hs_ref.at[pl.ds(pl.multiple_of(tile * rows * SLAB_ROWS, SLAB_ROWS), rows * SLAB_ROWS), :]
        return pltpu.make_async_copy(src, buf_ref.at[slot], lsem.at[slot])

    def drain(slot):
        for k in range(TOP_K):
            pltpu.make_async_copy(buf_ref.at[slot], xs_ref.at[pl.ds(0, rows * SLAB_ROWS), :], ssem.at[slot]).wait()

    @pl.when(i == 0)
    def _():
        zero_ref[...] = jnp.zeros_like(zero_ref)

        def start(e, carry):
            @pl.when(zrow_ref[e] >= 0)
            def _():
                zero_copy(e).start()
            return carry

        def wait(e, carry):
            @pl.when(zrow_ref[e] >= 0)
            def _():
                zero_copy(e).wait()
            return carry

        lax.fori_loop(0, zrow_ref.shape[0], start, 0)
        for ahead in range(min(DISPATCH_SLOTS - 1, n_steps)):
            load(ahead, ahead).start()
        lax.fori_loop(0, zrow_ref.shape[0], wait, 0)

    def step(slot):
        load(i, slot).wait()

        def start(r, carry):
            for k in range(TOP_K):
                _slab_copy(buf_ref.at[slot], r, xs_ref, dest_ref[0, k, r], ssem.at[slot]).start(priority=k)
            return carry

        lax.fori_loop(0, rows, start, 0, unroll=ROW_DMA_UNROLL)
        free = (slot + DISPATCH_SLOTS - 1) % DISPATCH_SLOTS

        @pl.when(i > 0)
        def _():
            drain(free)

        @pl.when(i + DISPATCH_SLOTS - 1 < n_steps)
        def _():
            load(i + DISPATCH_SLOTS - 1, free).start()

        @pl.when(i == n_steps - 1)
        def _():
            drain(slot)

    for slot in range(DISPATCH_SLOTS):
        pl.when(lax.rem(i, DISPATCH_SLOTS) == slot)(functools.partial(step, slot))


def _dispatch(hs, dest, zrow, n_rows):
    n, _, tm = dest.shape
    grid_spec = pltpu.PrefetchScalarGridSpec(
        num_scalar_prefetch=1,
        grid=(n,),
        in_specs=[
            pl.BlockSpec((1, TOP_K, tm), lambda i, z: (i, 0, 0), memory_space=pltpu.SMEM),
            pl.BlockSpec(memory_space=pl.ANY),
        ],
        out_specs=pl.BlockSpec(memory_space=pl.ANY),
        scratch_shapes=[pltpu.VMEM((DISPATCH_SLOTS, tm * SLAB_ROWS, LANES), jnp.uint32),
                        pltpu.VMEM((MOE_ROWS * SLAB_ROWS, LANES), jnp.uint32),
                        pltpu.SemaphoreType.DMA((DISPATCH_SLOTS,)), pltpu.SemaphoreType.DMA((DISPATCH_SLOTS,)),
                        pltpu.SemaphoreType.DMA],
    )
    return pl.pallas_call(
        functools.partial(_dispatch_kernel, n_steps=n),
        grid_spec=grid_spec,
        out_shape=jax.ShapeDtypeStruct((n_rows * SLAB_ROWS, LANES), jnp.uint32),
        compiler_params=pltpu.CompilerParams(dimension_semantics=("arbitrary",)),
        name="dispatch",
    )(zrow, dest, hs)


def _expert_kernel(be_ref, nx_ref, nu_ref, x_ref, w1_ref, w3_ref, w2_ref, y_ref,
                   w1s_ref, w3s_ref, w2s_ref, w1b_ref, w3b_ref, w2b_ref, wsem):
    i = pl.program_id(0)
    d = w1_ref.shape[1]
    rows = x_ref.shape[0] // SLAB_ROWS

    def fetch(e):
        return [pltpu.make_async_copy(src.at[e], dst, wsem.at[n])
                for n, (src, dst) in enumerate(((w1_ref, w1s_ref), (w3_ref, w3s_ref), (w2_ref, w2s_ref)))]

    @pl.when(i == 0)
    def _():
        for copy in fetch(be_ref[0]):
            copy.start(priority=1)

    @pl.when((i == 0) | (be_ref[i] != be_ref[jnp.maximum(i - 1, 0)]))
    def _():
        for copy in fetch(be_ref[i]):
            copy.wait()
        for s in range(SLAB_ROWS):
            for half, src0 in enumerate((s * LANES, d // 2 + s * LANES)):
                dst0 = (2 * s + half) * LANES
                w1b_ref[dst0:dst0 + LANES, :] = w1s_ref[src0:src0 + LANES, :].astype(BF16)
                w3b_ref[dst0:dst0 + LANES, :] = w3s_ref[src0:src0 + LANES, :].astype(BF16)
        w2b_ref[...] = w2s_ref[...].astype(BF16)

        @pl.when(nx_ref[i] >= 0)
        def _():
            for copy in fetch(nx_ref[i]):
                copy.start(priority=1)

    @pl.when(i < nu_ref[0])
    def _():
        x = jnp.concatenate([part.astype(BF16) for pair in _load_slabs(x_ref, rows) for part in pair], axis=1)
        a = jnp.dot(x, w1b_ref[...], preferred_element_type=F32)
        b = jnp.dot(x, w3b_ref[...], preferred_element_type=F32)
        mid = (a * _sigmoid(a) * b).astype(BF16)
        _store_slabs(y_ref, 0, jnp.dot(mid, w2b_ref[...], preferred_element_type=F32))

    @pl.when(i >= nu_ref[0])
    def _():
        y_ref[...] = jnp.zeros_like(y_ref)


def _experts(xs, block_expert, next_expert, n_used, w1, w3, w2):
    _, d, de = w1.shape
    nb = xs.shape[0] // (MOE_ROWS * SLAB_ROWS)
    nbytes = MOE_ROWS * d * (2 + 2 + 2 + 4) + 3 * d * de * (2 + 1) + MOE_ROWS * de * 12
    grid_spec = pltpu.PrefetchScalarGridSpec(
        num_scalar_prefetch=3,
        grid=(nb,),
        in_specs=[
            pl.BlockSpec((MOE_ROWS * SLAB_ROWS, LANES), lambda i, be, nx, nu: (jnp.minimum(i, nu[0] - 1), 0)),
            pl.BlockSpec(memory_space=pl.ANY),
            pl.BlockSpec(memory_space=pl.ANY),
            pl.BlockSpec(memory_space=pl.ANY),
        ],
        out_specs=pl.BlockSpec((MOE_ROWS * SLAB_ROWS, LANES), lambda i, be, nx, nu: (i, 0)),
        scratch_shapes=[pltpu.VMEM((d, de), F32), pltpu.VMEM((d, de), F32), pltpu.VMEM((de, d), F32),
                        pltpu.VMEM((d, de), BF16), pltpu.VMEM((d, de), BF16), pltpu.VMEM((de, d), BF16),
                        pltpu.SemaphoreType.DMA((3,))],
    )
    return pl.pallas_call(
        _expert_kernel,
        grid_spec=grid_spec,
        out_shape=jax.ShapeDtypeStruct(xs.shape, jnp.uint32),
        compiler_params=pltpu.CompilerParams(
            dimension_semantics=("arbitrary",), vmem_limit_bytes=_vmem_limit(nbytes)),
        name="experts",
    )(block_expert, next_expert, n_used, xs, w1, w3, w2)


def _combine_kernel(dest_ref, dnext_ref, dahead_ref, h_ref, r_ref, y_ref, g_ref, b_ref, o_ref, buf_ref, sem, *,
                    n_steps):
    i = pl.program_id(0)
    rows = h_ref.shape[0]
    chunk = min(rows, COMBINE_CHUNK)
    slots = buf_ref.shape[0]

    def gather(dref, slot, r, k):
        return _slab_copy(y_ref, dref[0, k, r], buf_ref.at[slot, k], r, sem.at[slot])

    def wait(slot):
        for k in range(TOP_K):
            pltpu.make_async_copy(y_ref.at[pl.ds(0, rows * SLAB_ROWS), :], buf_ref.at[slot, k],
                                  sem.at[slot]).wait()

    @pl.when(i == 0)
    def _():
        def start(r, carry):
            for k in range(TOP_K):
                gather(dest_ref, 0, r, k).start(priority=k)
                gather(dnext_ref, 1, r, k).start(priority=k)
            return carry

        lax.fori_loop(0, rows, start, 0, unroll=ROW_DMA_UNROLL)

    def step(slot):
        wait(slot)
        for c0 in range(0, rows, chunk):
            route = r_ref[c0:c0 + chunk, :]
            slabs = [_load_slabs(buf_ref.at[slot, k], chunk, c0) for k in range(TOP_K)]
            lo, hi = [], []
            for s in range(SLAB_ROWS):
                for half, out in enumerate((lo, hi)):
                    out.append(sum(route[:, 2 + k:3 + k] * slabs[k][s][half] for k in range(TOP_K)))
            moe = jnp.concatenate(lo + hi, axis=1)
            o_ref[c0:c0 + chunk, :] = _layer_norm(
                DEEPNORM_ALPHA * h_ref[c0:c0 + chunk, :] + moe, g_ref[...], b_ref[...])
            for r in range(c0, c0 + chunk):
                for k in range(TOP_K):
                    gather(dahead_ref, (slot + 2) % slots, r, k).start(priority=k)

    for slot in range(slots):
        pl.when(lax.rem(i, slots) == slot)(functools.partial(step, slot))

    @pl.when(i == n_steps - 1)
    def _():
        wait(n_steps % slots)
        wait((n_steps + 1) % slots)


def _combine(h, route, dest, y, ln_g, ln_b):
    t, d = h.shape
    n, _, tm = dest.shape
    return pl.pallas_call(
        functools.partial(_combine_kernel, n_steps=n),
        grid=(n,),
        in_specs=[
            pl.BlockSpec((1, TOP_K, tm), lambda i: (i, 0, 0), memory_space=pltpu.SMEM),
            pl.BlockSpec((1, TOP_K, tm), lambda i: (jnp.minimum(i + 1, n - 1), 0, 0), memory_space=pltpu.SMEM),
            pl.BlockSpec((1, TOP_K, tm), lambda i: (jnp.minimum(i + 2, n - 1), 0, 0), memory_space=pltpu.SMEM),
            pl.BlockSpec((tm, d), lambda i: (i, 0)),
            pl.BlockSpec((tm, LANES), lambda i: (i, 0)),
            pl.BlockSpec(memory_space=pl.ANY),
            pl.BlockSpec((1, d), lambda i: (0, 0)),
            pl.BlockSpec((1, d), lambda i: (0, 0)),
        ],
        out_specs=pl.BlockSpec((tm, d), lambda i: (i, 0)),
        out_shape=jax.ShapeDtypeStruct((t, d), F32),
        scratch_shapes=[pltpu.VMEM((COMBINE_SLOTS, TOP_K, tm * SLAB_ROWS, LANES), jnp.uint32),
                        pltpu.SemaphoreType.DMA((COMBINE_SLOTS,))],
        compiler_params=pltpu.CompilerParams(
            dimension_semantics=("arbitrary",),
            vmem_limit_bytes=_vmem_limit(tm * d * (4 + 4 + 4) + COMBINE_SLOTS * TOP_K * tm * d)),
        name="combine",
    )(dest, dest, dest, h, route, y, ln_g, ln_b)


def kernel(x, w_in, w_gate_a2, b_gate_a, sgu_ln_g, sgu_ln_b, sgu_w_s, sgu_b_s, gla_norm_g, w_branch_a, w_branch_b, w_merge, b_merge, w_out, ln1_g, ln1_b, w_router_group, b_router_group, w_router_expert, b_router_expert, w_exp_gate, w_exp_up, w_exp_down, ln2_g, ln2_b):
    batch, seq, d = x.shape
    t = batch * seq
    assert w_in.shape[0] == 1, "one layer"
    assert seq % SGU_CHUNK == 0 and seq % GLA_CHUNK == 0 and t % MOE_ROWS == 0
    sgu_width = sgu_ln_g.shape[1]
    key_dim = w_gate_a2.shape[2]
    val_dim = gla_norm_g.shape[1]
    rank = w_gate_a2.shape[1]
    xf = x.reshape(t, d)
    n_uv, n_h2 = 2 * sgu_width, 2 * key_dim + 2 * val_dim

    w_in_t = jnp.swapaxes(w_in.reshape(w_in.shape[1:]), 0, 1).astype(BF16)
    w_a_t = jnp.pad(w_in_t[n_uv + n_h2:, :], ((0, LANES - rank), (0, 0)))
    assert n_uv == 2 * sgu_w_s.shape[1] * SGU_CHUNK
    p, a_lr, s = _proj(xf, w_in_t, w_merge[0].astype(BF16), b_merge, w_a_t,
                       sgu_w_s[0], sgu_ln_g[0], sgu_ln_b[0], sgu_b_s[0], n_h2 - val_dim, val_dim, 1024)

    w_gate = jnp.pad(w_gate_a2[0].astype(BF16), ((0, LANES - rank), (0, 0)))
    o = _gla(p, 0, a_lr, w_gate, b_gate_a, gla_norm_g, batch, 2048)
    merged = _merge(s, o, w_branch_a[0].astype(BF16), w_branch_b[0].astype(BF16), p, n_h2, 1024, 1024)

    w_r = jnp.concatenate([w_router_group[0], w_router_expert[0]], axis=1)
    n_r = w_r.shape[1]
    w_r = jnp.pad(w_r, ((0, 0), (0, LANES - n_r))).astype(BF16)
    b_r = jnp.pad(jnp.concatenate([b_router_group, b_router_expert], axis=1), ((0, 0), (0, LANES - n_r)))
    h1, h1_slabs, route, counts = _out(merged, w_out[0].astype(BF16), xf, ln1_g, ln1_b, w_r, b_r, 512)

    assert t * TOP_K // MOE_ROWS <= 256, "per-expert block counts must stay exact in bf16"
    dest = _rank(route, counts, MOE_TOKEN_TILE)
    blocks_per_expert = (counts[0, :N_EXPERTS].astype(jnp.int32) + MOE_ROWS - 1) // MOE_ROWS
    block_ends = jnp.cumsum(blocks_per_expert)
    n_blocks = t * TOP_K // MOE_ROWS + N_EXPERTS
    n_used = block_ends[-1:]
    block_ids = jnp.minimum(jnp.arange(n_blocks, dtype=jnp.int32), n_used[0] - 1)
    experts = jnp.arange(N_EXPERTS, dtype=jnp.int32)
    block_expert = jnp.minimum(
        jnp.sum(block_ends[None, :] <= block_ids[:, None], axis=1), N_EXPERTS - 1).astype(jnp.int32)
    later = (experts[None, :] > block_expert[:, None]) & (blocks_per_expert[None, :] > 0)
    next_expert = jnp.min(jnp.where(later, experts[None, :], N_EXPERTS), axis=1)
    next_expert = jnp.where(next_expert < N_EXPERTS, next_expert, -1).astype(jnp.int32)
    tail_ids = n_used[0] + experts
    zrow = jnp.concatenate([
        jnp.where(blocks_per_expert > 0, (block_ends - 1) * MOE_ROWS, -1),
        jnp.where(tail_ids < n_blocks, tail_ids * MOE_ROWS, -1)]).astype(jnp.int32)

    xs = _dispatch(h1_slabs, dest, zrow, n_blocks * MOE_ROWS)
    y = _experts(xs, block_expert, next_expert, n_used.astype(jnp.int32),
                 w_exp_gate[0], w_exp_up[0], w_exp_down[0])
    out = _combine(h1, route, dest, y, ln2_g, ln2_b)
    return out.reshape(batch, seq, d)
```
